```python
import jax, jax.numpy as jnp
from jax import lax
import numpy as np

D_MODEL = 1024
BATCH = 4
SEQ = 4096
DEPTH = 1
DEC_BATCH = 32
DEC_SEQ = 4
PAST_LEN = 16384
PAGE_SIZE = 128

RET_HEADS = 4
RET_DK = 256
RET_DV = 512
RET_CHUNK = 128
ATT_HEADS = 8
ATT_KV_HEADS = 2
ATT_HD = 128
IDX_HEADS = 8
IDX_HD = 64
TOPK_ATTN = 256
Q_BLOCK = 128
ROPE_THETA = 10000.0
PEER_HEADS = 8
PEER_NKEYS = 128
PEER_N = PEER_NKEYS * PEER_NKEYS
PEER_DKEY = 256
PEER_TOPK = 16
PEER_BLOCK = 128
PLE_DIM = 256
LN_EPS = 1e-5
ALPHA = (2.0 * DEPTH) ** 0.25
BETA = (8.0 * DEPTH) ** -0.25
NEG = -1e30
SPLITS = (RET_HEADS * RET_DK, RET_HEADS * RET_DK, RET_HEADS * RET_DV, RET_HEADS * RET_DV,
          ATT_HEADS * ATT_HD, ATT_KV_HEADS * ATT_HD, ATT_KV_HEADS * ATT_HD,
          IDX_HEADS * IDX_HD, IDX_HD, IDX_HEADS, D_MODEL, D_MODEL)
VALUE_SPLITS = (2, 6)

kernel_name = 'hybrid_retention_dsa_peer_step'


def layer_norm(x, g, b):
    xf = x.astype(jnp.float32)
    mu = jnp.mean(xf, axis=-1, keepdims=True)
    var = jnp.mean(jnp.square(xf - mu), axis=-1, keepdims=True)
    y = (xf - mu) * lax.rsqrt(var + LN_EPS) * g.astype(jnp.float32) + b.astype(jnp.float32)
    return y.astype(x.dtype)


def rope(x, pos):
    half = x.shape[-1] // 2
    inv = ROPE_THETA ** (-jnp.arange(half, dtype=jnp.float32) / half)
    ang = pos.astype(jnp.float32)[:, None] * inv[None, :]
    cos, sin = jnp.cos(ang)[:, None, :], jnp.sin(ang)[:, None, :]
    xf = x.astype(jnp.float32)
    x1, x2 = xf[..., :half], xf[..., half:]
    return jnp.concatenate([x1 * cos - x2 * sin, x2 * cos + x1 * sin], axis=-1).astype(x.dtype)


def project_in(x, w_in, pos, ik_g, ik_b):
    B, S, _ = x.shape
    offs = np.cumsum(SPLITS)[:-1].tolist()
    rq, rk, rv, rg, aq, ak, av, iq, ik, iw, ga, gb = jnp.split(x @ w_in, offs, axis=-1)
    rq = rope(rq.reshape(B, S, RET_HEADS, RET_DK), pos)
    rk = rope(rk.reshape(B, S, RET_HEADS, RET_DK), pos) * (RET_DK ** -0.5)
    rv = rv.reshape(B, S, RET_HEADS, RET_DV)
    aq = rope(aq.reshape(B, S, ATT_HEADS, ATT_HD), pos)
    ak = rope(ak.reshape(B, S, ATT_KV_HEADS, ATT_HD), pos)
    av = av.reshape(B, S, ATT_KV_HEADS, ATT_HD)
    iq = rope(iq.reshape(B, S, IDX_HEADS, IDX_HD), pos)
    ik = rope(layer_norm(ik, ik_g, ik_b)[:, :, None, :], pos)[:, :, 0, :]
    iw = iw * (IDX_HEADS ** -0.5)
    return rq, rk, rv, rg, aq, ak, av, iq, ik, iw, ga, gb


def retention_log_decay():
    return jnp.log1p(-jnp.exp2(-5.0 - jnp.arange(RET_HEADS, dtype=jnp.float32)))


def retention_chunk(state, qkv):
    q, k, v = (t.astype(jnp.float32) for t in qkv)
    C = q.shape[2]
    lg = retention_log_decay()[:, None]
    i = jnp.arange(C, dtype=jnp.float32)
    diff = i[:, None] - i[None, :]
    decay = jnp.where(diff >= 0, jnp.exp(lg[:, :, None] * jnp.maximum(diff, 0.0)), 0.0)
    inner = jnp.einsum('bhij,bhjv->bhiv', jnp.einsum('bhid,bhjd->bhij', q, k) * decay, v)
    cross = jnp.einsum('bhid,bhdv->bhiv', q, state) * jnp.exp(lg * (i + 1.0))[None, :, :, None]
    k_dec = k * jnp.exp(lg * (C - 1.0 - i))[None, :, :, None]
    new_state = state * jnp.exp(lg * C)[None, :, :, None] + jnp.einsum('bhjd,bhjv->bhdv', k_dec, v)
    return new_state, inner + cross


def retention_prompt(q, k, v):
    B, S = q.shape[:2]
    nc = S // RET_CHUNK

    def chunks(t):
        return t.reshape(B, nc, RET_CHUNK, RET_HEADS, t.shape[-1]).transpose(1, 0, 3, 2, 4)

    s0 = jnp.zeros((B, RET_HEADS, RET_DK, RET_DV), jnp.float32)
    s_fin, o = lax.scan(retention_chunk, s0, (chunks(q), chunks(k), chunks(v)))
    return o.transpose(1, 0, 3, 2, 4).reshape(B, S, RET_HEADS, RET_DV), s_fin


def retention_sample(q, k, v, state):
    s_new, o = retention_chunk(state.astype(jnp.float32),
                               (q.transpose(0, 2, 1, 3), k.transpose(0, 2, 1, 3), v.transpose(0, 2, 1, 3)))
    return o.transpose(0, 2, 1, 3), s_new


def retention_output(o, gate, gn_g, gn_b):
    B, S = o.shape[:2]
    mu = jnp.mean(o, axis=-1, keepdims=True)
    var = jnp.mean(jnp.square(o - mu), axis=-1, keepdims=True)
    on = ((o - mu) * lax.rsqrt(var + LN_EPS)).reshape(B, S, RET_HEADS * RET_DV)
    on = on * gn_g.astype(jnp.float32) + gn_b.astype(jnp.float32)
    return (jax.nn.silu(gate.astype(jnp.float32)) * on).astype(gate.dtype)


def indexer_scores(q_idx, w_idx, k_idx):
    s = jax.nn.relu(jnp.einsum('bthd,bld->bthl', q_idx.astype(jnp.float32), k_idx.astype(jnp.float32)) * (IDX_HD ** -0.5))
    return jnp.einsum('bth,bthl->btl', w_idx.astype(jnp.float32), s)


def sparse_attend(q, k_sel, v_sel, valid):
    B, T = q.shape[:2]
    qg = q.reshape(B, T, ATT_KV_HEADS, ATT_HEADS // ATT_KV_HEADS, ATT_HD).astype(jnp.float32)
    s = jnp.einsum('btgrd,btkgd->btgrk', qg, k_sel.astype(jnp.float32)) * (ATT_HD ** -0.5)
    s = jnp.where(valid[:, :, None, None, :], s, NEG)
    p = jax.nn.softmax(s, axis=-1)
    o = jnp.einsum('btgrk,btkgd->btgrd', p, v_sel.astype(jnp.float32))
    return o.reshape(B, T, ATT_HEADS * ATT_HD).astype(q.dtype)


gather_rows = jax.vmap(lambda t, i: t[i])


def attention_prompt(q, k, v, q_idx, w_idx, k_idx):
    B, S = q.shape[:2]
    topk = min(TOPK_ATTN, S // 4)
    nb = S // Q_BLOCK
    key_pos = jnp.arange(S, dtype=jnp.int32)

    def block(args):
        qb, qib, wb, b0 = args
        qpos = b0 + jnp.arange(Q_BLOCK, dtype=jnp.int32)
        scores = indexer_scores(qib, wb, k_idx)
        scores = jnp.where((key_pos[None, :] <= qpos[:, None])[None], scores, -jnp.inf)
        _, idx = lax.top_k(scores, topk)
        valid = idx <= qpos[None, :, None]
        return sparse_attend(qb, gather_rows(k, idx), gather_rows(v, idx), valid)

    def blocks(t):
        return t.reshape(B, nb, Q_BLOCK, *t.shape[2:]).swapaxes(0, 1)

    starts = jnp.arange(nb, dtype=jnp.int32) * Q_BLOCK
    o = lax.map(block, (blocks(q), blocks(q_idx), blocks(w_idx), starts))
    return o.swapaxes(0, 1).reshape(B, S, ATT_HEADS * ATT_HD)


def attention_sample(q, k_new, v_new, q_idx, w_idx, k_idx_new, cache_k, cache_v, cache_kidx, page_table):
    Bd, T = q.shape[:2]
    L = PAST_LEN + T
    topk = min(TOPK_ATTN, L // 4)
    kidx_past = cache_kidx[page_table].reshape(Bd, PAST_LEN, IDX_HD)
    kidx_all = jnp.concatenate([kidx_past, k_idx_new.astype(kidx_past.dtype)], axis=1)
    qpos = PAST_LEN + jnp.arange(T, dtype=jnp.int32)
    scores = indexer_scores(q_idx, w_idx, kidx_all)
    scores = jnp.where((jnp.arange(L, dtype=jnp.int32)[None, :] <= qpos[:, None])[None], scores, -jnp.inf)
    _, idx = lax.top_k(scores, topk)
    valid = idx <= qpos[None, :, None]
    in_past = (idx < PAST_LEN)[..., None, None]
    pidx = jnp.minimum(idx, PAST_LEN - 1)
    phys = page_table[jnp.arange(Bd)[:, None, None], pidx // PAGE_SIZE]
    off = pidx % PAGE_SIZE
    nidx = jnp.clip(idx - PAST_LEN, 0, T - 1)
    k_sel = jnp.where(in_past, cache_k[phys, off], gather_rows(k_new, nidx).astype(cache_k.dtype))
    v_sel = jnp.where(in_past, cache_v[phys, off], gather_rows(v_new, nidx).astype(cache_v.dtype))
    return sparse_attend(q, k_sel, v_sel, valid)


def peer(x, wq, subkeys, u, v):
    B, S, D = x.shape
    xt = x.reshape(B * S, D)
    n = xt.shape[0]
    xt = jnp.pad(xt, ((0, (-n) % PEER_BLOCK), (0, 0)))
    half = PEER_DKEY // 2

    def block(xb):
        T = xb.shape[0]
        q = (xb @ wq).reshape(T, PEER_HEADS, PEER_DKEY).astype(jnp.float32)
        s1 = jnp.einsum('thd,hnd->thn', q[..., :half], subkeys[:, 0].astype(jnp.float32))
        s2 = jnp.einsum('thd,hnd->thn', q[..., half:], subkeys[:, 1].astype(jnp.float32))
        v1, i1 = lax.top_k(s1, PEER_TOPK)
        v2, i2 = lax.top_k(s2, PEER_TOPK)
        cand = (v1[..., :, None] + v2[..., None, :]).reshape(T, PEER_HEADS, PEER_TOPK * PEER_TOPK)
        cidx = (i1[..., :, None] * PEER_NKEYS + i2[..., None, :]).reshape(T, PEER_HEADS, PEER_TOPK * PEER_TOPK)
        sc, sel = lax.top_k(cand, PEER_TOPK)
        eidx = jnp.take_along_axis(cidx, sel, axis=-1)
        g = jax.nn.softmax(sc, axis=-1)
        act = jax.nn.gelu(jnp.einsum('td,thkd->thk', xb.astype(jnp.float32), u[eidx].astype(jnp.float32)), approximate=False)
        return jnp.einsum('thk,thkd->td', g * act, v[eidx].astype(jnp.float32)).astype(x.dtype)

    out = lax.map(block, xt.reshape(-1, PEER_BLOCK, D))
    return out.reshape(-1, D)[:n].reshape(B, S, D)


def finish_layer(x, p_emb, ret_o, att_o, ga, gb, w_ret_o, w_att_o, w_out, ln1_g, ln1_b,
                 peer_wq, peer_subkeys, peer_u, peer_v, w_ple_gate, w_ple, ln2_g, ln2_b):
    branch = jax.nn.sigmoid(ga) * (ret_o @ w_ret_o) + jax.nn.sigmoid(gb) * (att_o @ w_att_o)
    x = layer_norm(ALPHA * x + branch @ w_out, ln1_g, ln1_b)
    ple = jax.nn.sigmoid(x @ w_ple_gate) * (p_emb @ w_ple)
    return layer_norm(ALPHA * x + peer(x, peer_wq, peer_subkeys, peer_u, peer_v) + ple, ln2_g, ln2_b)


def setup_inputs(seed: int = 0) -> dict:
    key = jax.random.key(seed)
    ks = jax.random.split(key, 32)
    f32 = jnp.float32

    def nrm(k, shape, s):
        return jax.random.normal(k, shape, f32) * s

    n_pages = PAST_LEN // PAGE_SIZE
    n_used = DEC_BATCH * n_pages
    n_phys = n_used + n_used // 4 + 1
    page_table = jax.random.permutation(ks[0], n_phys)[:n_used].reshape(DEC_BATCH, n_pages).astype(jnp.int32)
    in_cols = sum(SPLITS)
    col_scale = jnp.concatenate([jnp.full((c,), BETA if j in VALUE_SPLITS else 1.0, f32) for j, c in enumerate(SPLITS)])
    return {
        'x_prompt': nrm(ks[1], (BATCH, SEQ, D_MODEL), 1.0),
        'x_sample': nrm(ks[2], (DEC_BATCH, DEC_SEQ, D_MODEL), 1.0),
        'cache_k': nrm(ks[3], (DEPTH, n_phys, PAGE_SIZE, ATT_KV_HEADS, ATT_HD), 1.0),
        'cache_v': nrm(ks[4], (DEPTH, n_phys, PAGE_SIZE, ATT_KV_HEADS, ATT_HD), BETA),
        'cache_kidx': nrm(ks[5], (DEPTH, n_phys, PAGE_SIZE, IDX_HD), 1.0),
        'state_ret': nrm(ks[6], (DEPTH, DEC_BATCH, RET_HEADS, RET_DK, RET_DV), 0.3),
        'page_table': page_table,
        'p_prompt': nrm(ks[7], (DEPTH, BATCH, SEQ, PLE_DIM), 1.0),
        'p_sample': nrm(ks[8], (DEPTH, DEC_BATCH, DEC_SEQ, PLE_DIM), 1.0),
        'w_in': nrm(ks[9], (DEPTH, D_MODEL, in_cols), D_MODEL ** -0.5) * col_scale,
        'idx_k_g': 1.0 + nrm(ks[10], (DEPTH, IDX_HD), 0.02),
        'idx_k_b': nrm(ks[11], (DEPTH, IDX_HD), 0.02),
        'gn_g': 1.0 + nrm(ks[12], (DEPTH, RET_HEADS * RET_DV), 0.02),
        'gn_b': nrm(ks[13], (DEPTH, RET_HEADS * RET_DV), 0.02),
        'w_ret_o': nrm(ks[14], (DEPTH, RET_HEADS * RET_DV, D_MODEL), BETA * (RET_HEADS * RET_DV) ** -0.5),
        'w_att_o': nrm(ks[15], (DEPTH, ATT_HEADS * ATT_HD, D_MODEL), BETA * (ATT_HEADS * ATT_HD) ** -0.5),
        'w_out': nrm(ks[16], (DEPTH, D_MODEL, D_MODEL), BETA * D_MODEL ** -0.5),
        'ln1_g': 1.0 + nrm(ks[17], (DEPTH, D_MODEL), 0.02),
        'ln1_b': nrm(ks[18], (DEPTH, D_MODEL), 0.02),
        'peer_wq': nrm(ks[19], (DEPTH, D_MODEL, PEER_HEADS * PEER_DKEY), D_MODEL ** -0.5),
        'peer_subkeys': nrm(ks[20], (DEPTH, PEER_HEADS, 2, PEER_NKEYS, PEER_DKEY // 2), (PEER_DKEY // 2) ** -0.5),
        'peer_u': nrm(ks[21], (DEPTH, PEER_N, D_MODEL), D_MODEL ** -0.5),
        'peer_v': nrm(ks[22], (DEPTH, PEER_N, D_MODEL), BETA * PEER_HEADS ** -0.5),
        'w_ple_gate': nrm(ks[23], (DEPTH, D_MODEL, D_MODEL), D_MODEL ** -0.5),
        'w_ple': nrm(ks[24], (DEPTH, PLE_DIM, D_MODEL), BETA * PLE_DIM ** -0.5),
        'ln2_g': 1.0 + nrm(ks[25], (DEPTH, D_MODEL), 0.02),
        'ln2_b': nrm(ks[26], (DEPTH, D_MODEL), 0.02),
    }


def reference(x_prompt, x_sample, cache_k, cache_v, cache_kidx, state_ret, page_table,
              p_prompt, p_sample, w_in, idx_k_g, idx_k_b, gn_g, gn_b, w_ret_o, w_att_o, w_out,
              ln1_g, ln1_b, peer_wq, peer_subkeys, peer_u, peer_v, w_ple_gate, w_ple, ln2_g, ln2_b):
    pos_p = jnp.arange(x_prompt.shape[1], dtype=jnp.int32)
    pos_s = PAST_LEN + jnp.arange(x_sample.shape[1], dtype=jnp.int32)
    xp, xs = x_prompt, x_sample
    kp, vp, ikp, rp, ksm, vsm, iksm, rsm = [], [], [], [], [], [], [], []
    for i in range(DEPTH):
        tail = (w_ret_o[i], w_att_o[i], w_out[i], ln1_g[i], ln1_b[i], peer_wq[i], peer_subkeys[i],
                peer_u[i], peer_v[i], w_ple_gate[i], w_ple[i], ln2_g[i], ln2_b[i])
        rq, rk, rv, rg, aq, ak, av, iq, ik, iw, ga, gb = project_in(xp, w_in[i], pos_p, idx_k_g[i], idx_k_b[i])
        o_ret, s_fin = retention_prompt(rq, rk, rv)
        ret_o = retention_output(o_ret, rg, gn_g[i], gn_b[i])
        att_o = attention_prompt(aq, ak, av, iq, iw, ik)
        xp = finish_layer(xp, p_prompt[i], ret_o, att_o, ga, gb, *tail)
        kp.append(ak)
        vp.append(av)
        ikp.append(ik)
        rp.append(s_fin.astype(state_ret.dtype))
        rq, rk, rv, rg, aq, ak, av, iq, ik, iw, ga, gb = project_in(xs, w_in[i], pos_s, idx_k_g[i], idx_k_b[i])
        o_ret, s_new = retention_sample(rq, rk, rv, state_ret[i])
        ret_o = retention_output(o_ret, rg, gn_g[i], gn_b[i])
        att_o = attention_sample(aq, ak, av, iq, iw, ik, cache_k[i], cache_v[i], cache_kidx[i], page_table)
        xs = finish_layer(xs, p_sample[i], ret_o, att_o, ga, gb, *tail)
        ksm.append(ak)
        vsm.append(av)
        iksm.append(ik)
        rsm.append(s_new.astype(state_ret.dtype))
    return (xp, xs, jnp.stack(kp), jnp.stack(vp), jnp.stack(ikp), jnp.stack(rp),
            jnp.stack(ksm), jnp.stack(vsm), jnp.stack(iksm), jnp.stack(rsm))
```

```python
import functools

import jax
import jax.numpy as jnp
from jax import lax
from jax.experimental import pallas as pl
from jax.experimental.pallas import tpu as pltpu

F32 = jnp.float32
BF16 = jnp.bfloat16

RET_HEADS = 4
RET_DK = 256
RET_DV = 512
RET_CHUNK = 128
ATT_HEADS = 8
ATT_KV_HEADS = 2
ATT_HD = 128
IDX_HEADS = 8
IDX_HD = 64
TOPK_ATTN = 256
ROPE_THETA = 10000.0
PEER_HEADS = 8
PEER_NKEYS = 128
PEER_DKEY = 256
PEER_TOPK = 16
LN_EPS = 1e-5
NEG = -1e30
INF = float("inf")

LANES = 128
BF16_SUBLANES = 16
VMEM_LIMIT = 48 * 1024 * 1024
BISECT_ITERS = 40
SAMPLE_PAGES_PER_STEP = 8


def _dot(a, b):
    return jnp.dot(a, b, preferred_element_type=F32)


def _dot_nt(a, b):
    return lax.dot_general(a, b, (((1,), (1,)), ((), ())), preferred_element_type=F32)


def _dot_tn(a, b):
    return lax.dot_general(a, b, (((0,), (0,)), ((), ())), preferred_element_type=F32)


def _params(sem):
    return pltpu.CompilerParams(dimension_semantics=sem, vmem_limit_bytes=VMEM_LIMIT)


def _rope_tables(pos, d, width):
    half = d // 2
    inv = ROPE_THETA ** (-jnp.arange(half, dtype=F32) / half)
    ang = pos.astype(F32)[:, None] * inv[None, :]
    cos, sin = jnp.cos(ang), jnp.sin(ang)
    cosf = jnp.concatenate([cos, cos], axis=-1)
    sinf = jnp.concatenate([-sin, sin], axis=-1)
    reps = width // d
    return jnp.tile(cosf, (1, reps)), jnp.tile(sinf, (1, reps))


def _partner(y, d):
    if d == 2 * LANES:
        return jnp.concatenate([y[:, LANES:], y[:, :LANES]], axis=-1)
    if d == LANES:
        return pltpu.roll(y, LANES // 2, axis=1)
    lane = lax.broadcasted_iota(jnp.int32, y.shape, 1)
    first_half = (lane & (d - 1)) < (d // 2)
    return jnp.where(first_half, pltpu.roll(y, LANES - d // 2, axis=1), pltpu.roll(y, d // 2, axis=1))


def _rope(y, cosf, sinf, d):
    return y * cosf + _partner(y, d) * sinf


def _ret_qk_kernel(x_ref, w_ref, cos_ref, sin_ref, q_ref, k_ref):
    x = x_ref[...]
    cosf, sinf = cos_ref[...], sin_ref[...]
    for h in range(2 * RET_HEADS):
        y = _dot(x, w_ref[:, h * RET_DK:(h + 1) * RET_DK])
        r = _rope(y, cosf, sinf, RET_DK)
        if h < RET_HEADS:
            q_ref[:, h * RET_DK:(h + 1) * RET_DK] = r.astype(BF16)
        else:
            hh = h - RET_HEADS
            k_ref[:, hh * RET_DK:(hh + 1) * RET_DK] = (r * (RET_DK ** -0.5)).astype(BF16)


def _plain_kernel(x_ref, w_ref, o_ref, *, sigmoid):
    y = _dot(x_ref[...], w_ref[...])
    if sigmoid:
        y = jax.nn.sigmoid(y)
    o_ref[...] = y.astype(o_ref.dtype)


def _att_proj_kernel(x_ref, wq_ref, wk_ref, wv_ref, cos_ref, sin_ref,
                     q_ref, k_ref, v_ref, kb_ref, vb_ref):
    x = x_ref[...]
    cosf, sinf = cos_ref[...], sin_ref[...]
    for h in range(ATT_HEADS):
        sl = slice(h * ATT_HD, (h + 1) * ATT_HD)
        q_ref[:, sl] = _rope(_dot(x, wq_ref[:, sl]), cosf, sinf, ATT_HD).astype(BF16)
    for h in range(ATT_KV_HEADS):
        sl = slice(h * ATT_HD, (h + 1) * ATT_HD)
        k = _rope(_dot(x, wk_ref[:, sl]), cosf, sinf, ATT_HD)
        k_ref[:, sl] = k
        kb_ref[:, sl] = k.astype(BF16)
    v = _dot(x, wv_ref[...])
    v_ref[...] = v
    vb_ref[...] = v.astype(BF16)


def _idx_proj_kernel(x_ref, wq_ref, wk_ref, ww_ref, cos_ref, sin_ref, g_ref, b_ref,
                     q_ref, k_ref, kb_ref, w_ref):
    x = x_ref[...]
    cosf, sinf = cos_ref[...], sin_ref[...]
    for c in range(IDX_HEADS * IDX_HD // LANES):
        sl = slice(c * LANES, (c + 1) * LANES)
        q_ref[:, sl] = _rope(_dot(x, wq_ref[:, sl]), cosf, sinf, IDX_HD).astype(BF16)
    y = _dot(x, wk_ref[...])
    mu = jnp.mean(y, axis=-1, keepdims=True)
    var = jnp.mean(jnp.square(y - mu), axis=-1, keepdims=True)
    yn = (y - mu) * lax.rsqrt(var + LN_EPS) * g_ref[...] + b_ref[...]
    k = yn * cosf + pltpu.roll(yn, IDX_HD // 2, axis=1) * sinf
    k_ref[...] = k[:, :IDX_HD]
    kb_ref[...] = k[:, :IDX_HD].astype(BF16)
    w = _dot(x, ww_ref[...])
    w_ref[...] = w[:, :IDX_HEADS] * (IDX_HEADS ** -0.5)


def _project(xb, w, ikg, ikb, pos_rows, tm):
    n, dm = xb.shape
    nt = n // tm
    rows = pos_rows.shape[0]
    tab_blocks = rows // tm
    hq = RET_HEADS * RET_DK
    hv = RET_HEADS * RET_DV
    o = 0
    w_rqk = w[:, o:o + 2 * hq]; o += 2 * hq
    w_rv = w[:, o:o + hv]; o += hv
    w_rg = w[:, o:o + hv]; o += hv
    w_aq = w[:, o:o + ATT_HEADS * ATT_HD]; o += ATT_HEADS * ATT_HD
    w_ak = w[:, o:o + ATT_KV_HEADS * ATT_HD]; o += ATT_KV_HEADS * ATT_HD
    w_av = w[:, o:o + ATT_KV_HEADS * ATT_HD]; o += ATT_KV_HEADS * ATT_HD
    w_iq = w[:, o:o + IDX_HEADS * IDX_HD]; o += IDX_HEADS * IDX_HD
    w_ik = w[:, o:o + IDX_HD]; o += IDX_HD
    w_iw = w[:, o:o + IDX_HEADS]; o += IDX_HEADS
    w_gab = w[:, o:]

    x_spec = pl.BlockSpec((tm, dm), lambda i: (i, 0))

    def tab_spec(width):
        return pl.BlockSpec((tm, width), lambda i: (i % tab_blocks, 0))

    def full(shape):
        return pl.BlockSpec(shape, lambda i: (0,) * len(shape))

    def row_spec(width):
        return pl.BlockSpec((tm, width), lambda i: (i, 0))

    cos256, sin256 = _rope_tables(pos_rows, RET_DK, RET_DK)
    rq, rk = pl.pallas_call(
        _ret_qk_kernel,
        grid=(nt,),
        in_specs=[x_spec, full((dm, 2 * hq)), tab_spec(RET_DK), tab_spec(RET_DK)],
        out_specs=[row_spec(hq), row_spec(hq)],
        out_shape=[jax.ShapeDtypeStruct((n, hq), BF16)] * 2,
        compiler_params=_params(("parallel",)),
        name="proj_ret_qk",
    )(xb, w_rqk, cos256, sin256)

    def plain(wc, dtype, sigmoid, name):
        cols = wc.shape[1]
        tn = min(1024, cols)
        return pl.pallas_call(
            functools.partial(_plain_kernel, sigmoid=sigmoid),
            grid=(cols // tn, nt),
            in_specs=[pl.BlockSpec((tm, dm), lambda j, i: (i, 0)),
                      pl.BlockSpec((dm, tn), lambda j, i: (0, j))],
            out_specs=pl.BlockSpec((tm, tn), lambda j, i: (i, j)),
            out_shape=jax.ShapeDtypeStruct((n, cols), dtype),
            compiler_params=_params(("parallel", "parallel")),
            name=name,
        )(xb, wc)

    rv = plain(w_rv, BF16, False, "proj_ret_v")
    rg = plain(w_rg, F32, False, "proj_ret_gate")
    sgab = plain(w_gab, F32, True, "proj_branch_gates")

    cos128, sin128 = _rope_tables(pos_rows, ATT_HD, ATT_HD)
    kvw = ATT_KV_HEADS * ATT_HD
    aq, ak, av, akb, avb = pl.pallas_call(
        _att_proj_kernel,
        grid=(nt,),
        in_specs=[x_spec, full((dm, ATT_HEADS * ATT_HD)), full((dm, kvw)), full((dm, kvw)),
                  tab_spec(ATT_HD), tab_spec(ATT_HD)],
        out_specs=[row_spec(ATT_HEADS * ATT_HD), row_spec(kvw), row_spec(kvw), row_spec(kvw), row_spec(kvw)],
        out_shape=[jax.ShapeDtypeStruct((n, ATT_HEADS * ATT_HD), BF16),
                   jax.ShapeDtypeStruct((n, kvw), F32), jax.ShapeDtypeStruct((n, kvw), F32),
                   jax.ShapeDtypeStruct((n, kvw), BF16), jax.ShapeDtypeStruct((n, kvw), BF16)],
        compiler_params=_params(("parallel",)),
        name="proj_att",
    )(xb, w_aq, w_ak, w_av, cos128, sin128)

    cos64, sin64 = _rope_tables(pos_rows, IDX_HD, LANES)
    w_ik2 = jnp.concatenate([w_ik, w_ik], axis=1)
    w_iw_pad = jnp.pad(w_iw, ((0, 0), (0, LANES - IDX_HEADS)))
    iq, ik, ikb, iw = pl.pallas_call(
        _idx_proj_kernel,
        grid=(nt,),
        in_specs=[x_spec, full((dm, IDX_HEADS * IDX_HD)), full((dm, LANES)), full((dm, LANES)),
                  tab_spec(LANES), tab_spec(LANES), full((1, LANES)), full((1, LANES))],
        out_specs=[row_spec(IDX_HEADS * IDX_HD), row_spec(IDX_HD), row_spec(IDX_HD), row_spec(IDX_HEADS)],
        out_shape=[jax.ShapeDtypeStruct((n, IDX_HEADS * IDX_HD), BF16),
                   jax.ShapeDtypeStruct((n, IDX_HD), F32), jax.ShapeDtypeStruct((n, IDX_HD), BF16),
                   jax.ShapeDtypeStruct((n, IDX_HEADS), F32)],
        compiler_params=_params(("parallel",)),
        name="proj_idx",
    )(xb, w_iq, w_ik2, w_iw_pad, cos64, sin64, ikg, ikb)
    return dict(rq=rq, rk=rk, rv=rv, rg=rg, sgab=sgab, aq=aq, ak=ak, av=av, akb=akb, avb=avb,
                iq=iq, ik=ik, ikb=ikb, iw=iw)


def _retention_kernel(sdec_ref, q_ref, k_ref, v_ref, gate_ref, gg_ref, gb_ref,
                      d_ref, cdec_ref, kdec_ref, si_ref, o_ref, so_ref):
    h = pl.program_id(1)
    c = pl.program_id(2)

    @pl.when(c == 0)
    def _():
        so_ref[0, 0] = si_ref[0, 0]

    q = q_ref[...]
    k = k_ref[...]
    v = v_ref[...]
    state = so_ref[0, 0]
    s = _dot_nt(q, k)
    inner = _dot((s * d_ref[0]).astype(BF16), v)
    cross = _dot(q, state.astype(BF16)) * cdec_ref[0]
    o = inner + cross
    kd = (k.astype(F32) * kdec_ref[0]).astype(BF16)
    so_ref[0, 0] = state * sdec_ref[h] + _dot_tn(kd, v)
    mu = jnp.mean(o, axis=-1, keepdims=True)
    var = jnp.mean(jnp.square(o - mu), axis=-1, keepdims=True)
    on = (o - mu) * lax.rsqrt(var + LN_EPS) * gg_ref[...] + gb_ref[...]
    o_ref[...] = (jax.nn.silu(gate_ref[...]) * on).astype(o_ref.dtype)


def _retention(rq, rk, rv, rg, gn_g, gn_b, state_in, chunk, chunk_eff):
    n = rq.shape[0]
    nb = state_in.shape[0]
    nc = n // (nb * chunk)
    lg = jnp.log1p(-jnp.exp2(-5.0 - jnp.arange(RET_HEADS, dtype=F32)))[:, None]
    i = jnp.arange(chunk, dtype=F32)
    diff = i[:, None] - i[None, :]
    decay = jnp.where(diff >= 0, jnp.exp(lg[:, :, None] * jnp.maximum(diff, 0.0)), 0.0)
    cdec = jnp.exp(lg * (i + 1.0))[:, :, None]
    kdec = jnp.where(i < chunk_eff, jnp.exp(lg * jnp.maximum(chunk_eff - 1.0 - i, 0.0)), 0.0)[:, :, None]
    sdec = jnp.exp(lg[:, 0] * chunk_eff)

    o, s_out = pl.pallas_call(
        _retention_kernel,
        grid=(nb, RET_HEADS, nc),
        in_specs=[
            pl.BlockSpec(memory_space=pltpu.SMEM),
            pl.BlockSpec((chunk, RET_DK), lambda b, h, c: (b * nc + c, h)),
            pl.BlockSpec((chunk, RET_DK), lambda b, h, c: (b * nc + c, h)),
            pl.BlockSpec((chunk, RET_DV), lambda b, h, c: (b * nc + c, h)),
            pl.BlockSpec((chunk, RET_DV), lambda b, h, c: (b * nc + c, h)),
            pl.BlockSpec((1, RET_DV), lambda b, h, c: (0, h)),
            pl.BlockSpec((1, RET_DV), lambda b, h, c: (0, h)),
            pl.BlockSpec((1, chunk, chunk), lambda b, h, c: (h, 0, 0)),
            pl.BlockSpec((1, chunk, 1), lambda b, h, c: (h, 0, 0)),
            pl.BlockSpec((1, chunk, 1), lambda b, h, c: (h, 0, 0)),
            pl.BlockSpec((1, 1, RET_DK, RET_DV), lambda b, h, c: (b, h, 0, 0)),
        ],
        out_specs=[
            pl.BlockSpec((chunk, RET_DV), lambda b, h, c: (b * nc + c, h)),
            pl.BlockSpec((1, 1, RET_DK, RET_DV), lambda b, h, c: (b, h, 0, 0)),
        ],
        out_shape=[jax.ShapeDtypeStruct((n, RET_HEADS * RET_DV), BF16),
                   jax.ShapeDtypeStruct(state_in.shape, F32)],
        compiler_params=_params(("parallel", "parallel", "arbitrary")),
        name="retention",
    )(sdec, rq, rk, rv, rg, gn_g, gn_b, decay, cdec, kdec, state_in)
    return o, s_out


def _tie_select(sct, tau, need, run, tri):
    gt = sct > tau
    eq = sct == tau
    eqf = jnp.where(eq, 1.0, 0.0)
    prefix = _dot(eqf.astype(BF16), tri)
    sel = gt | (eq & ((run + prefix) <= need))
    return sel, run + jnp.sum(eqf, axis=-1, keepdims=True)


def _attn_prompt_kernel(aq_ref, iq_ref, iw_ref, k_ref, v_ref, kidx_ref, tri_ref, o_ref,
                        sc_ref, m_ref, l_ref, acc_ref, *, tq, topk):
    i = pl.program_id(1)
    nkv = i + 1
    qpos = i * tq + lax.broadcasted_iota(jnp.int32, (tq, 1), 0)
    kq = jnp.minimum(topk, qpos + 1).astype(F32)
    group = ATT_HEADS // ATT_KV_HEADS

    def score_body(j, carry):
        mn, mx = carry
        kx = kidx_ref[pl.ds(pl.multiple_of(j * tq, tq), tq), :]
        s = jnp.zeros((tq, tq), F32)
        for h in range(IDX_HEADS):
            d = _dot_nt(iq_ref[:, h * IDX_HD:(h + 1) * IDX_HD], kx) * (IDX_HD ** -0.5)
            s = s + iw_ref[:, h:h + 1] * jnp.maximum(d, 0.0)
        kpos = j * tq + lax.broadcasted_iota(jnp.int32, (tq, tq), 1)
        vis = kpos <= qpos
        sc_ref[j] = jnp.where(vis, s, -INF)
        mn = jnp.minimum(mn, jnp.min(jnp.where(vis, s, INF), axis=-1, keepdims=True))
        mx = jnp.maximum(mx, jnp.max(jnp.where(vis, s, -INF), axis=-1, keepdims=True))
        return mn, mx

    mn, mx = lax.fori_loop(0, nkv, score_body,
                           (jnp.full((tq, 1), INF, F32), jnp.full((tq, 1), -INF, F32)))

    def bisect_body(_, carry):
        lo, hi = carry
        mid = 0.5 * (lo + hi)

        def cnt_body(j, c):
            return c + jnp.sum(jnp.where(sc_ref[j] >= mid, 1.0, 0.0), axis=-1, keepdims=True)

        ge = lax.fori_loop(0, nkv, cnt_body, jnp.zeros((tq, 1), F32)) >= kq
        return jnp.where(ge, mid, lo), jnp.where(ge, hi, mid)

    lo, _ = lax.fori_loop(0, BISECT_ITERS, bisect_body, (mn, mx + 1.0))

    def snap_body(j, t):
        sct = sc_ref[j]
        return jnp.minimum(t, jnp.min(jnp.where(sct >= lo, sct, INF), axis=-1, keepdims=True))

    tau = lax.fori_loop(0, nkv, snap_body, jnp.full((tq, 1), INF, F32))

    def gt_body(j, c):
        return c + jnp.sum(jnp.where(sc_ref[j] > tau, 1.0, 0.0), axis=-1, keepdims=True)

    need = kq - lax.fori_loop(0, nkv, gt_body, jnp.zeros((tq, 1), F32))

    m_ref[...] = jnp.full(m_ref.shape, NEG, F32)
    l_ref[...] = jnp.zeros(l_ref.shape, F32)
    acc_ref[...] = jnp.zeros(acc_ref.shape, F32)
    tri = tri_ref[...]

    def att_body(j, run):
        sel, run = _tie_select(sc_ref[j], tau, need, run, tri)
        rows = pl.ds(pl.multiple_of(j * tq, tq), tq)
        for h in range(ATT_HEADS):
            g = h // group
            kg = k_ref[rows, g * ATT_HD:(g + 1) * ATT_HD]
            vg = v_ref[rows, g * ATT_HD:(g + 1) * ATT_HD]
            s = _dot_nt(aq_ref[:, h * ATT_HD:(h + 1) * ATT_HD], kg) * (ATT_HD ** -0.5)
            sm = jnp.where(sel, s, NEG)
            m_old = m_ref[h]
            m_new = jnp.maximum(m_old, jnp.max(sm, axis=-1, keepdims=True))
            p = jnp.where(sel, jnp.exp(sm - m_new), 0.0)
            alpha = jnp.exp(m_old - m_new)
            l_ref[h] = alpha * l_ref[h] + jnp.sum(p, axis=-1, keepdims=True)
            acc_ref[h] = alpha * acc_ref[h] + _dot(p.astype(BF16), vg)
            m_ref[h] = m_new
        return run

    lax.fori_loop(0, nkv, att_body, jnp.zeros((tq, 1), F32))
    for h in range(ATT_HEADS):
        o_ref[:, h * ATT_HD:(h + 1) * ATT_HD] = (acc_ref[h] / l_ref[h]).astype(o_ref.dtype)


def _attention_prompt(aq, iq, iw, akb, avb, ikb, nb, seq):
    tq = 256 if seq % 256 == 0 else seq
    nq = seq // tq
    topk = min(TOPK_ATTN, seq // 4)
    n = aq.shape[0]
    tri = jnp.triu(jnp.ones((tq, tq), F32)).astype(BF16)
    kvw = ATT_KV_HEADS * ATT_HD
    return pl.pallas_call(
        functools.partial(_attn_prompt_kernel, tq=tq, topk=topk),
        grid=(nb, nq),
        in_specs=[
            pl.BlockSpec((tq, ATT_HEADS * ATT_HD), lambda b, i: (b * nq + i, 0)),
            pl.BlockSpec((tq, IDX_HEADS * IDX_HD), lambda b, i: (b * nq + i, 0)),
            pl.BlockSpec((tq, IDX_HEADS), lambda b, i: (b * nq + i, 0)),
            pl.BlockSpec((seq, kvw), lambda b, i: (b, 0)),
            pl.BlockSpec((seq, kvw), lambda b, i: (b, 0)),
            pl.BlockSpec((seq, IDX_HD), lambda b, i: (b, 0)),
            pl.BlockSpec((tq, tq), lambda b, i: (0, 0)),
        ],
        out_specs=pl.BlockSpec((tq, ATT_HEADS * ATT_HD), lambda b, i: (b * nq + i, 0)),
        out_shape=jax.ShapeDtypeStruct((n, ATT_HEADS * ATT_HD), BF16),
        scratch_shapes=[
            pltpu.VMEM((nq, tq, tq), F32),
            pltpu.VMEM((ATT_HEADS, tq, 1), F32),
            pltpu.VMEM((ATT_HEADS, tq, 1), F32),
            pltpu.VMEM((ATT_HEADS, tq, ATT_HD), F32),
        ],
        compiler_params=_params(("parallel", "arbitrary")),
        name="attention_prompt",
    )(aq, iq, iw, akb, avb, ikb, tri)


def _page_scores(iq_ref, iw_ref, kx):
    s = jnp.zeros((iq_ref.shape[1], kx.shape[0]), F32)
    for h in range(IDX_HEADS):
        d = _dot_nt(iq_ref[0, :, h * IDX_HD:(h + 1) * IDX_HD], kx) * (IDX_HD ** -0.5)
        s = s + iw_ref[0, :, h:h + 1] * jnp.maximum(d, 0.0)
    return s


def _sample_scores_kernel(pt_ref, iq_ref, iw_ref, *refs, npp, n_new, topk):
    page_refs = refs[:npp]
    knew_ref = refs[npp]
    sc_ref, tau_ref, need_ref = refs[npp + 1:]
    s = pl.program_id(1)
    npages = sc_ref.shape[1] - 1
    tp, page = sc_ref.shape[2], sc_ref.shape[3]

    for p in range(npp):
        sc_ref[0, s * npp + p] = _page_scores(iq_ref, iw_ref, page_refs[p][0].astype(BF16))

    @pl.when(s == pl.num_programs(1) - 1)
    def _():
        snew = _page_scores(iq_ref, iw_ref, knew_ref[0].astype(BF16))
        t = jnp.minimum(lax.broadcasted_iota(jnp.int32, (tp, page), 0), n_new - 1)
        j = lax.broadcasted_iota(jnp.int32, (tp, page), 1)
        sc_ref[0, npages] = jnp.where(j <= t, snew, -INF)

        def red(fn, init):
            return lax.fori_loop(0, npages + 1, fn, jnp.full((tp, 1), init, F32))

        mn = red(lambda p, a: jnp.minimum(a, jnp.min(
            jnp.where(sc_ref[0, p] == -INF, INF, sc_ref[0, p]), axis=-1, keepdims=True)), INF)
        mx = red(lambda p, a: jnp.maximum(a, jnp.max(sc_ref[0, p], axis=-1, keepdims=True)), -INF)
        kq = jnp.full((tp, 1), float(topk), F32)

        def bisect_body(_, carry):
            lo, hi = carry
            mid = 0.5 * (lo + hi)
            cnt = red(lambda p, a: a + jnp.sum(jnp.where(sc_ref[0, p] >= mid, 1.0, 0.0),
                                               axis=-1, keepdims=True), 0.0)
            ge = cnt >= kq
            return jnp.where(ge, mid, lo), jnp.where(ge, hi, mid)

        lo, _ = lax.fori_loop(0, BISECT_ITERS, bisect_body, (mn, mx + 1.0))
        tau = red(lambda p, a: jnp.minimum(a, jnp.min(
            jnp.where(sc_ref[0, p] >= lo, sc_ref[0, p], INF), axis=-1, keepdims=True)), INF)
        cgt = red(lambda p, a: a + jnp.sum(jnp.where(sc_ref[0, p] > tau, 1.0, 0.0),
                                           axis=-1, keepdims=True), 0.0)
        tau_ref[0] = jnp.broadcast_to(tau, (tp, page))
        need_ref[0] = jnp.broadcast_to(kq - cgt, (tp, page))


def _sample_attend_kernel(pt_ref, q_ref, sc_ref, tau_ref, need_ref, tri_ref, *refs, npp):
    k_refs = refs[:npp]
    v_refs = refs[npp:2 * npp]
    knew_ref, vnew_ref, o_ref, m_ref, l_ref, acc_ref, run_ref = refs[2 * npp:]
    s = pl.program_id(1)
    npages = sc_ref.shape[1] - 1
    group = ATT_HEADS // ATT_KV_HEADS
    grows = q_ref.shape[1] // ATT_KV_HEADS

    @pl.when(s == 0)
    def _():
        m_ref[...] = jnp.full(m_ref.shape, NEG, F32)
        l_ref[...] = jnp.zeros(l_ref.shape, F32)
        acc_ref[...] = jnp.zeros(acc_ref.shape, F32)
        run_ref[...] = jnp.zeros(run_ref.shape, F32)

    tau = tau_ref[0][:, :1]
    need = need_ref[0][:, :1]
    tri = tri_ref[...]

    def attend(page, kpage, vpage):
        sel, run = _tie_select(sc_ref[0, page], tau, need, run_ref[...], tri)
        run_ref[...] = run
        selw = jnp.where(sel, 1.0, 0.0)
        selg = jnp.concatenate([selw] * group, axis=0) > 0.5
        for g in range(ATT_KV_HEADS):
            kg = kpage[:, g * ATT_HD:(g + 1) * ATT_HD].astype(BF16)
            vg = vpage[:, g * ATT_HD:(g + 1) * ATT_HD].astype(BF16)
            sg = _dot_nt(q_ref[0, g * grows:(g + 1) * grows, :], kg) * (ATT_HD ** -0.5)
            sm = jnp.where(selg, sg, NEG)
            m_old = m_ref[g]
            m_new = jnp.maximum(m_old, jnp.max(sm, axis=-1, keepdims=True))
            p = jnp.where(selg, jnp.exp(sm - m_new), 0.0)
            alpha = jnp.exp(m_old - m_new)
            l_ref[g] = alpha * l_ref[g] + jnp.sum(p, axis=-1, keepdims=True)
            acc_ref[g] = alpha * acc_ref[g] + _dot(p.astype(BF16), vg)
            m_ref[g] = m_new

    for p in range(npp):
        attend(s * npp + p, k_refs[p][0], v_refs[p][0])

    @pl.when(s == pl.num_programs(1) - 1)
    def _():
        attend(npages, knew_ref[0], vnew_ref[0])
        for g in range(ATT_KV_HEADS):
            o_ref[0, g * grows:(g + 1) * grows, :] = acc_ref[g] / l_ref[g]


def _attention_sample(aq, iq, iw, ak, av, ik, cache_k, cache_v, cache_kidx, page_table, nb, t_new):
    n_phys, page = cache_kidx.shape[0], cache_kidx.shape[1]
    npages = page_table.shape[1]
    npp = SAMPLE_PAGES_PER_STEP if npages % SAMPLE_PAGES_PER_STEP == 0 else 1
    nsteps = npages // npp
    tp = BF16_SUBLANES
    topk = min(TOPK_ATTN, (npages * page + t_new) // 4)
    kvw = ATT_KV_HEADS * ATT_HD

    def pad_tokens(a):
        a = a.reshape(nb, t_new, a.shape[-1])
        return jnp.pad(a, ((0, 0), (0, tp - t_new), (0, 0)))

    def pad_page(a):
        a = a.reshape(nb, t_new, a.shape[-1])
        return jnp.pad(a, ((0, 0), (0, page - t_new), (0, 0)))

    iq_p, iw_p = pad_tokens(iq), pad_tokens(iw)
    q_p = pad_tokens(aq).reshape(nb, tp, ATT_HEADS, ATT_HD).transpose(0, 2, 1, 3).reshape(nb, ATT_HEADS * tp, ATT_HD)
    knew, vnew, kinew = pad_page(ak), pad_page(av), pad_page(ik)
    ck = cache_k.reshape(n_phys, page, kvw)
    cv = cache_v.reshape(n_phys, page, kvw)
    tri = jnp.triu(jnp.ones((page, page), F32)).astype(BF16)

    def page_spec(width, p):
        return pl.BlockSpec((1, page, width), lambda b, s, pt, p=p: (pt[b, s * npp + p], 0, 0))

    def seq_spec(shape):
        return pl.BlockSpec((1,) + shape, lambda b, s, pt: (b,) + (0,) * len(shape))

    sc, tau, need = pl.pallas_call(
        functools.partial(_sample_scores_kernel, npp=npp, n_new=t_new, topk=topk),
        grid_spec=pltpu.PrefetchScalarGridSpec(
            num_scalar_prefetch=1,
            grid=(nb, nsteps),
            in_specs=[seq_spec((tp, IDX_HEADS * IDX_HD)), seq_spec((tp, IDX_HEADS))]
                     + [page_spec(IDX_HD, p) for p in range(npp)] + [seq_spec((page, IDX_HD))],
            out_specs=[seq_spec((npages + 1, tp, page)), seq_spec((tp, page)), seq_spec((tp, page))],
        ),
        out_shape=[jax.ShapeDtypeStruct((nb, npages + 1, tp, page), F32),
                   jax.ShapeDtypeStruct((nb, tp, page), F32),
                   jax.ShapeDtypeStruct((nb, tp, page), F32)],
        compiler_params=_params(("parallel", "arbitrary")),
        name="sample_scores",
    )(page_table, iq_p, iw_p, *([cache_kidx] * npp), kinew)

    grows = (ATT_HEADS // ATT_KV_HEADS) * tp
    o = pl.pallas_call(
        functools.partial(_sample_attend_kernel, npp=npp),
        grid_spec=pltpu.PrefetchScalarGridSpec(
            num_scalar_prefetch=1,
            grid=(nb, nsteps),
            in_specs=[seq_spec((ATT_HEADS * tp, ATT_HD)), seq_spec((npages + 1, tp, page)),
                      seq_spec((tp, page)), seq_spec((tp, page)),
                      pl.BlockSpec((page, page), lambda b, s, pt: (0, 0))]
                     + [page_spec(kvw, p) for p in range(npp)] + [page_spec(kvw, p) for p in range(npp)]
                     + [seq_spec((page, kvw)), seq_spec((page, kvw))],
            out_specs=seq_spec((ATT_HEADS * tp, ATT_HD)),
            scratch_shapes=[
                pltpu.VMEM((ATT_KV_HEADS, grows, 1), F32),
                pltpu.VMEM((ATT_KV_HEADS, grows, 1), F32),
                pltpu.VMEM((ATT_KV_HEADS, grows, ATT_HD), F32),
                pltpu.VMEM((tp, 1), F32),
            ],
        ),
        out_shape=jax.ShapeDtypeStruct((nb, ATT_HEADS * tp, ATT_HD), F32),
        compiler_params=_params(("parallel", "arbitrary")),
        name="sample_attend",
    )(page_table, q_p, sc, tau, need, tri, *([ck] * npp), *([cv] * npp), knew, vnew)
    o = o.reshape(nb, ATT_HEADS, tp, ATT_HD)[:, :, :t_new].transpose(0, 2, 1, 3)
    return o.reshape(nb * t_new, ATT_HEADS * ATT_HD).astype(BF16)


def _layer_norm(y, g, b):
    mu = jnp.mean(y, axis=-1, keepdims=True)
    var = jnp.mean(jnp.square(y - mu), axis=-1, keepdims=True)
    return (y - mu) * lax.rsqrt(var + LN_EPS) * g + b


def _mix_kernel(x_ref, ret_ref, att_ref, sga_ref, sgb_ref, wr_ref, wa_ref, wo_ref, g_ref, b_ref,
                o_ref, ot_ref, *, alpha):
    branch = sga_ref[...] * _dot(ret_ref[...], wr_ref[...]) + sgb_ref[...] * _dot(att_ref[...], wa_ref[...])
    y = alpha * x_ref[...] + _dot(branch.astype(BF16), wo_ref[...])
    x1 = _layer_norm(y, g_ref[...], b_ref[...])
    o_ref[...] = x1
    ot_ref[...] = x1.T.astype(BF16)


def _mix(x, ret_o, att_o, sgab, w_ret_o, w_att_o, w_out, g, b, tm, alpha):
    n, dm = x.shape
    hv = RET_HEADS * RET_DV
    ha = ATT_HEADS * ATT_HD

    def full(shape):
        return pl.BlockSpec(shape, lambda i: (0,) * len(shape))

    return pl.pallas_call(
        functools.partial(_mix_kernel, alpha=alpha),
        grid=(n // tm,),
        in_specs=[pl.BlockSpec((tm, dm), lambda i: (i, 0)),
                  pl.BlockSpec((tm, hv), lambda i: (i, 0)),
                  pl.BlockSpec((tm, ha), lambda i: (i, 0)),
                  pl.BlockSpec((tm, dm), lambda i: (i, 0)),
                  pl.BlockSpec((tm, dm), lambda i: (i, 1)),
                  full((hv, dm)), full((ha, dm)), full((dm, dm)), full((1, dm)), full((1, dm))],
        out_specs=[pl.BlockSpec((tm, dm), lambda i: (i, 0)),
                   pl.BlockSpec((dm, tm), lambda i: (0, i))],
        out_shape=[jax.ShapeDtypeStruct((n, dm), F32), jax.ShapeDtypeStruct((dm, n), BF16)],
        compiler_params=_params(("parallel",)),
        name="branch_mix_ln1",
    )(x, ret_o, att_o, sgab, sgab, w_ret_o, w_att_o, w_out, g, b)


def _top_values(cur, k):
    rows = []
    for _ in range(k):
        m = jnp.max(cur, axis=0, keepdims=True)
        rows.append(m)
        cur = jnp.where(cur == m, -INF, cur)
    return rows


def _peer_route_kernel(xt_ref, wqt_ref, sk_ref, s1_ref, s2_ref, a_ref, e2_ref, tau_ref):
    xt = xt_ref[...]
    half = PEER_DKEY // 2
    for h in range(PEER_HEADS):
        qt = _dot(wqt_ref[h * PEER_DKEY:(h + 1) * PEER_DKEY, :], xt).astype(BF16)
        s1 = _dot(sk_ref[h, 0], qt[:half])
        s2 = _dot(sk_ref[h, 1], qt[half:])
        v1 = _top_values(s1, PEER_TOPK)
        v2 = jnp.concatenate(_top_values(s2, PEER_TOPK), axis=0)
        cand = jnp.concatenate([r + v2 for r in v1], axis=0)
        best = _top_values(cand, PEER_TOPK)
        tau = best[-1]
        mx = best[0]
        z = jnp.sum(jnp.where(cand >= tau, jnp.exp(cand - mx), 0.0), axis=0, keepdims=True)
        s1_ref[h] = s1
        s2_ref[h] = s2
        a_ref[h] = jnp.exp(s1 - v1[0]) / z
        e2_ref[h] = jnp.exp(s2 - v2[:1])
        tau_ref[h] = jnp.broadcast_to(tau, (8, tau.shape[1]))


def _peer_dense_kernel(xt_ref, s1_ref, s2_ref, a_ref, e2_ref, tau_ref, u_ref, vt_ref, o_ref, *, nsub):
    e = pl.program_id(1)

    @pl.when(e == 0)
    def _():
        o_ref[...] = jnp.zeros(o_ref.shape, F32)

    xt = xt_ref[...]
    nk = PEER_NKEYS
    acc = None
    for c in range(nsub):
        i1 = e * nsub + c
        act = _dot(u_ref[c * nk:(c + 1) * nk, :], xt)
        act = 0.5 * act * (1.0 + lax.erf(act * (2.0 ** -0.5)))
        g = jnp.zeros(act.shape, F32)
        for h in range(PEER_HEADS):
            s = s1_ref[h, pl.ds(i1, 1), :] + s2_ref[h]
            w = e2_ref[h] * a_ref[h, pl.ds(i1, 1), :]
            g = g + jnp.where(s >= tau_ref[h, :1, :], w, 0.0)
        part = _dot(vt_ref[:, c * nk:(c + 1) * nk], (g * act).astype(BF16))
        acc = part if acc is None else acc + part
    o_ref[...] += acc


def _peer(x1t, wqt, subkeys, u, vt, tt):
    dm, n = x1t.shape
    nt = n // tt
    nk = PEER_NKEYS
    hshape = (PEER_HEADS, nk, n)
    hspec = pl.BlockSpec((PEER_HEADS, nk, tt), lambda i: (0, 0, i))
    s1, s2, a, e2, tau = pl.pallas_call(
        _peer_route_kernel,
        grid=(nt,),
        in_specs=[pl.BlockSpec((dm, tt), lambda i: (0, i)),
                  pl.BlockSpec(wqt.shape, lambda i: (0, 0)),
                  pl.BlockSpec(subkeys.shape, lambda i: (0, 0, 0, 0))],
        out_specs=[hspec, hspec, hspec, hspec, pl.BlockSpec((PEER_HEADS, 8, tt), lambda i: (0, 0, i))],
        out_shape=[jax.ShapeDtypeStruct(hshape, F32)] * 4 + [jax.ShapeDtypeStruct((PEER_HEADS, 8, n), F32)],
        compiler_params=_params(("parallel",)),
        name="peer_route",
    )(x1t, wqt, subkeys)

    nsub = 4
    hspec2 = pl.BlockSpec((PEER_HEADS, nk, tt), lambda i, e: (0, 0, i))
    return pl.pallas_call(
        functools.partial(_peer_dense_kernel, nsub=nsub),
        grid=(nt, nk // nsub),
        in_specs=[pl.BlockSpec((dm, tt), lambda i, e: (0, i)),
                  hspec2, hspec2, hspec2, hspec2,
                  pl.BlockSpec((PEER_HEADS, 8, tt), lambda i, e: (0, 0, i)),
                  pl.BlockSpec((nsub * nk, dm), lambda i, e: (e, 0)),
                  pl.BlockSpec((dm, nsub * nk), lambda i, e: (0, e))],
        out_specs=pl.BlockSpec((dm, tt), lambda i, e: (0, i)),
        out_shape=jax.ShapeDtypeStruct((dm, n), F32),
        compiler_params=_params(("parallel", "arbitrary")),
        name="peer_dense",
    )(x1t, s1, s2, a, e2, tau, u, vt)


def _final_kernel(x1_ref, pt_ref, p_ref, wg_ref, wp_ref, g_ref, b_ref, o_ref, *, alpha):
    x1 = x1_ref[...]
    ple = jax.nn.sigmoid(_dot(x1.astype(BF16), wg_ref[...])) * _dot(p_ref[...], wp_ref[...])
    y = alpha * x1 + pt_ref[...].T + ple
    o_ref[...] = _layer_norm(y, g_ref[...], b_ref[...])


def _final(x1, peer_t, p_emb, w_gate, w_ple, g, b, tm, alpha):
    n, dm = x1.shape
    pd = p_emb.shape[1]

    def full(shape):
        return pl.BlockSpec(shape, lambda i: (0,) * len(shape))

    return pl.pallas_call(
        functools.partial(_final_kernel, alpha=alpha),
        grid=(n // tm,),
        in_specs=[pl.BlockSpec((tm, dm), lambda i: (i, 0)),
                  pl.BlockSpec((dm, tm), lambda i: (0, i)),
                  pl.BlockSpec((tm, pd), lambda i: (i, 0)),
                  full((dm, dm)), full((pd, dm)), full((1, dm)), full((1, dm))],
        out_specs=pl.BlockSpec((tm, dm), lambda i: (i, 0)),
        out_shape=jax.ShapeDtypeStruct((n, dm), F32),
        compiler_params=_params(("parallel",)),
        name="ple_ln2",
    )(x1, peer_t, p_emb, w_gate, w_ple, g, b)


def _layer(x, p_emb, pos_rows, tm, weights, attention_fn, state_in, chunk, chunk_eff, alpha):
    n = x.shape[0]
    nb = state_in.shape[0]
    t_seq = n // nb
    pr = _project(x.astype(BF16), weights["w_in"], weights["ikg"], weights["ikb"], pos_rows, tm)

    def pad_chunks(a):
        if t_seq % chunk == 0:
            return a
        a = a.reshape(nb, t_seq, a.shape[-1])
        return jnp.pad(a, ((0, 0), (0, chunk - t_seq), (0, 0))).reshape(nb * chunk, a.shape[-1])

    ret_o, s_out = _retention(pad_chunks(pr["rq"]), pad_chunks(pr["rk"]), pad_chunks(pr["rv"]),
                              pad_chunks(pr["rg"]), weights["gn_g"], weights["gn_b"],
                              state_in, chunk, chunk_eff)
    if t_seq % chunk != 0:
        ret_o = ret_o.reshape(nb, chunk, -1)[:, :t_seq].reshape(n, -1)
    att_o = attention_fn(pr)
    x1, x1t = _mix(x, ret_o, att_o, pr["sgab"], weights["w_ret_o"], weights["w_att_o"], weights["w_out"],
                   weights["ln1_g"], weights["ln1_b"], tm, alpha)
    peer_t = _peer(x1t, weights["peer_wqt"], weights["peer_subkeys"], weights["peer_u"], weights["peer_vt"],
                   min(512, n))
    y = _final(x1, peer_t, p_emb.astype(BF16), weights["w_ple_gate"], weights["w_ple"],
               weights["ln2_g"], weights["ln2_b"], tm, alpha)
    return y, pr["ak"], pr["av"], pr["ik"], s_out


def kernel(x_prompt, x_sample, cache_k, cache_v, cache_kidx, state_ret, page_table, p_prompt, p_sample,
           w_in, idx_k_g, idx_k_b, gn_g, gn_b, w_ret_o, w_att_o, w_out, ln1_g, ln1_b, peer_wq,
           peer_subkeys, peer_u, peer_v, w_ple_gate, w_ple, ln2_g, ln2_b):
    depth = w_in.shape[0]
    assert depth == 1, "single-layer trunk"
    nbp, seq, dm = x_prompt.shape
    nbs, t_new, _ = x_sample.shape
    past_len = page_table.shape[1] * cache_k.shape[2]
    alpha = (2.0 * depth) ** 0.25
    i = 0
    weights = dict(
        w_in=w_in[i].astype(BF16),
        ikg=jnp.concatenate([idx_k_g[i], idx_k_g[i]])[None, :],
        ikb=jnp.concatenate([idx_k_b[i], idx_k_b[i]])[None, :],
        gn_g=gn_g[i][None, :], gn_b=gn_b[i][None, :],
        w_ret_o=w_ret_o[i].astype(BF16), w_att_o=w_att_o[i].astype(BF16), w_out=w_out[i].astype(BF16),
        ln1_g=ln1_g[i][None, :], ln1_b=ln1_b[i][None, :],
        peer_wqt=peer_wq[i].T.astype(BF16), peer_subkeys=peer_subkeys[i].astype(BF16),
        peer_u=peer_u[i].astype(BF16), peer_vt=peer_v[i].T.astype(BF16),
        w_ple_gate=w_ple_gate[i].astype(BF16), w_ple=w_ple[i].astype(BF16),
        ln2_g=ln2_g[i][None, :], ln2_b=ln2_b[i][None, :],
    )

    def attn_p(pr):
        return _attention_prompt(pr["aq"], pr["iq"], pr["iw"], pr["akb"], pr["avb"], pr["ikb"], nbp, seq)

    yp, kp, vp, ikp, rp = _layer(
        x_prompt.reshape(nbp * seq, dm), p_prompt[i].reshape(nbp * seq, -1),
        jnp.arange(seq, dtype=jnp.int32), min(512, seq), weights, attn_p,
        jnp.zeros((nbp,) + state_ret.shape[2:], F32), RET_CHUNK, RET_CHUNK, alpha)

    def attn_s(pr):
        return _attention_sample(pr["aq"], pr["iq"], pr["iw"], pr["ak"], pr["av"], pr["ik"],
                                 cache_k[i], cache_v[i], cache_kidx[i], page_table, nbs, t_new)

    ns = nbs * t_new
    pos_s = jnp.tile(past_len + jnp.arange(t_new, dtype=jnp.int32), nbs)
    ys, ks, vs, iks, rs = _layer(
        x_sample.reshape(ns, dm), p_sample[i].reshape(ns, -1), pos_s, ns, weights, attn_s,
        state_ret[i].astype(F32), BF16_SUBLANES, t_new, alpha)

    return (yp.reshape(nbp, seq, dm), ys.reshape(nbs, t_new, dm),
            kp.reshape(1, nbp, seq, ATT_KV_HEADS, ATT_HD), vp.reshape(1, nbp, seq, ATT_KV_HEADS, ATT_HD),
            ikp.reshape(1, nbp, seq, IDX_HD), rp[None].astype(state_ret.dtype),
            ks.reshape(1, nbs, t_new, ATT_KV_HEADS, ATT_HD), vs.reshape(1, nbs, t_new, ATT_KV_HEADS, ATT_HD),
            iks.reshape(1, nbs, t_new, IDX_HD), rs[None].astype(state_ret.dtype))
```

```python
import functools

import jax
import jax.numpy as jnp
from jax import lax
from jax.experimental import pallas as pl
from jax.experimental.pallas import tpu as pltpu

F32 = jnp.float32
BF16 = jnp.bfloat16

RET_HEADS = 4
RET_DK = 256
RET_DV = 512
RET_CHUNK = 128
ATT_HEADS = 8
ATT_KV_HEADS = 2
ATT_HD = 128
IDX_HEADS = 8
IDX_HD = 64
TOPK_ATTN = 256
ROPE_THETA = 10000.0
PEER_HEADS = 8
PEER_NKEYS = 128
PEER_DKEY = 256
PEER_TOPK = 16
LN_EPS = 1e-5
NEG = -1e30
INF = float("inf")

LANES = 128
BF16_SUBLANES = 16
VMEM_LIMIT = 48 * 1024 * 1024
BISECT_ITERS = 40
SAMPLE_PAGES_PER_STEP = 8


def _dot(a, b):
    return jnp.dot(a, b, preferred_element_type=F32)


def _dot_nt(a, b):
    return lax.dot_general(a, b, (((1,), (1,)), ((), ())), preferred_element_type=F32)


def _dot_tn(a, b):
    return lax.dot_general(a, b, (((0,), (0,)), ((), ())), preferred_element_type=F32)


def _params(sem):
    return pltpu.CompilerParams(dimension_semantics=sem, vmem_limit_bytes=VMEM_LIMIT)


def _rope_tables(pos, d, width):
    half = d // 2
    inv = ROPE_THETA ** (-jnp.arange(half, dtype=F32) / half)
    ang = pos.astype(F32)[:, None] * inv[None, :]
    cos, sin = jnp.cos(ang), jnp.sin(ang)
    cosf = jnp.concatenate([cos, cos], axis=-1)
    sinf = jnp.concatenate([-sin, sin], axis=-1)
    reps = width // d
    return jnp.tile(cosf, (1, reps)), jnp.tile(sinf, (1, reps))


def _partner(y, d):
    if d == 2 * LANES:
        return jnp.concatenate([y[:, LANES:], y[:, :LANES]], axis=-1)
    if d == LANES:
        return pltpu.roll(y, LANES // 2, axis=1)
    lane = lax.broadcasted_iota(jnp.int32, y.shape, 1)
    first_half = (lane & (d - 1)) < (d // 2)
    return jnp.where(first_half, pltpu.roll(y, LANES - d // 2, axis=1), pltpu.roll(y, d // 2, axis=1))


def _rope(y, cosf, sinf, d):
    return y * cosf + _partner(y, d) * sinf


def _ret_qk_kernel(x_ref, w_ref, cos_ref, sin_ref, q_ref, k_ref):
    x = x_ref[...]
    cosf, sinf = cos_ref[...], sin_ref[...]
    for h in range(2 * RET_HEADS):
        y = _dot(x, w_ref[:, h * RET_DK:(h + 1) * RET_DK])
        r = _rope(y, cosf, sinf, RET_DK)
        if h < RET_HEADS:
            q_ref[:, h * RET_DK:(h + 1) * RET_DK] = r.astype(BF16)
        else:
            hh = h - RET_HEADS
            k_ref[:, hh * RET_DK:(hh + 1) * RET_DK] = (r * (RET_DK ** -0.5)).astype(BF16)


def _plain_kernel(x_ref, w_ref, o_ref, *, sigmoid):
    y = _dot(x_ref[...], w_ref[...])
    if sigmoid:
        y = jax.nn.sigmoid(y)
    o_ref[...] = y.astype(o_ref.dtype)


def _att_proj_kernel(x_ref, wq_ref, wk_ref, wv_ref, cos_ref, sin_ref,
                     q_ref, k_ref, v_ref, kb_ref, vb_ref):
    x = x_ref[...]
    cosf, sinf = cos_ref[...], sin_ref[...]
    for h in range(ATT_HEADS):
        sl = slice(h * ATT_HD, (h + 1) * ATT_HD)
        q_ref[:, sl] = _rope(_dot(x, wq_ref[:, sl]), cosf, sinf, ATT_HD).astype(BF16)
    tm = x.shape[0]
    v = _dot(x, wv_ref[...])
    vb_ref[...] = v.astype(BF16)
    for h in range(ATT_KV_HEADS):
        sl = slice(h * ATT_HD, (h + 1) * ATT_HD)
        k = _rope(_dot(x, wk_ref[:, sl]), cosf, sinf, ATT_HD)
        kb_ref[:, sl] = k.astype(BF16)
        head_rows = pl.ds(h, tm, stride=ATT_KV_HEADS)
        k_ref[head_rows, :] = k
        v_ref[head_rows, :] = v[:, sl]


def _idx_proj_kernel(x_ref, wq_ref, wk_ref, ww_ref, cos_ref, sin_ref, g_ref, b_ref,
                     q_ref, k_ref, kb_ref, w_ref):
    x = x_ref[...]
    cosf, sinf = cos_ref[...], sin_ref[...]
    for c in range(IDX_HEADS * IDX_HD // LANES):
        sl = slice(c * LANES, (c + 1) * LANES)
        q_ref[:, sl] = _rope(_dot(x, wq_ref[:, sl]), cosf, sinf, IDX_HD).astype(BF16)
    y = _dot(x, wk_ref[...])
    mu = jnp.mean(y, axis=-1, keepdims=True)
    var = jnp.mean(jnp.square(y - mu), axis=-1, keepdims=True)
    yn = (y - mu) * lax.rsqrt(var + LN_EPS) * g_ref[...] + b_ref[...]
    k = yn * cosf + pltpu.roll(yn, IDX_HD // 2, axis=1) * sinf
    k_ref[...] = k[:, :IDX_HD]
    kb_ref[...] = k[:, :IDX_HD].astype(BF16)
    w = _dot(x, ww_ref[...])
    w_ref[...] = w[:, :IDX_HEADS] * (IDX_HEADS ** -0.5)


def _project(xb, w, ikg, ikb, pos_rows, tm):
    n, dm = xb.shape
    nt = n // tm
    rows = pos_rows.shape[0]
    tab_blocks = rows // tm
    hq = RET_HEADS * RET_DK
    hv = RET_HEADS * RET_DV
    o = 0
    w_rqk = w[:, o:o + 2 * hq]; o += 2 * hq
    w_rv = w[:, o:o + hv]; o += hv
    w_rg = w[:, o:o + hv]; o += hv
    w_aq = w[:, o:o + ATT_HEADS * ATT_HD]; o += ATT_HEADS * ATT_HD
    w_ak = w[:, o:o + ATT_KV_HEADS * ATT_HD]; o += ATT_KV_HEADS * ATT_HD
    w_av = w[:, o:o + ATT_KV_HEADS * ATT_HD]; o += ATT_KV_HEADS * ATT_HD
    w_iq = w[:, o:o + IDX_HEADS * IDX_HD]; o += IDX_HEADS * IDX_HD
    w_ik = w[:, o:o + IDX_HD]; o += IDX_HD
    w_iw = w[:, o:o + IDX_HEADS]; o += IDX_HEADS
    w_gab = w[:, o:]

    x_spec = pl.BlockSpec((tm, dm), lambda i: (i, 0))

    def tab_spec(width):
        return pl.BlockSpec((tm, width), lambda i: (i % tab_blocks, 0))

    def full(shape):
        return pl.BlockSpec(shape, lambda i: (0,) * len(shape))

    def row_spec(width):
        return pl.BlockSpec((tm, width), lambda i: (i, 0))

    cos256, sin256 = _rope_tables(pos_rows, RET_DK, RET_DK)
    rq, rk = pl.pallas_call(
        _ret_qk_kernel,
        grid=(nt,),
        in_specs=[x_spec, full((dm, 2 * hq)), tab_spec(RET_DK), tab_spec(RET_DK)],
        out_specs=[row_spec(hq), row_spec(hq)],
        out_shape=[jax.ShapeDtypeStruct((n, hq), BF16)] * 2,
        compiler_params=_params(("parallel",)),
        name="proj_ret_qk",
    )(xb, w_rqk, cos256, sin256)

    def plain(wc, dtype, sigmoid, name):
        cols = wc.shape[1]
        tn = min(1024, cols)
        return pl.pallas_call(
            functools.partial(_plain_kernel, sigmoid=sigmoid),
            grid=(cols // tn, nt),
            in_specs=[pl.BlockSpec((tm, dm), lambda j, i: (i, 0)),
                      pl.BlockSpec((dm, tn), lambda j, i: (0, j))],
            out_specs=pl.BlockSpec((tm, tn), lambda j, i: (i, j)),
            out_shape=jax.ShapeDtypeStruct((n, cols), dtype),
            compiler_params=_params(("parallel", "parallel")),
            name=name,
        )(xb, wc)

    rv = plain(w_rv, BF16, False, "proj_ret_v")
    rg = plain(w_rg, F32, False, "proj_ret_gate")
    sgab = plain(w_gab, F32, True, "proj_branch_gates")

    cos128, sin128 = _rope_tables(pos_rows, ATT_HD, ATT_HD)
    kvw = ATT_KV_HEADS * ATT_HD
    aq, ak, av, akb, avb = pl.pallas_call(
        _att_proj_kernel,
        grid=(nt,),
        in_specs=[x_spec, full((dm, ATT_HEADS * ATT_HD)), full((dm, kvw)), full((dm, kvw)),
                  tab_spec(ATT_HD), tab_spec(ATT_HD)],
        out_specs=[row_spec(ATT_HEADS * ATT_HD),
                   pl.BlockSpec((tm * ATT_KV_HEADS, ATT_HD), lambda i: (i, 0)),
                   pl.BlockSpec((tm * ATT_KV_HEADS, ATT_HD), lambda i: (i, 0)),
                   row_spec(kvw), row_spec(kvw)],
        out_shape=[jax.ShapeDtypeStruct((n, ATT_HEADS * ATT_HD), BF16),
                   jax.ShapeDtypeStruct((n * ATT_KV_HEADS, ATT_HD), F32),
                   jax.ShapeDtypeStruct((n * ATT_KV_HEADS, ATT_HD), F32),
                   jax.ShapeDtypeStruct((n, kvw), BF16), jax.ShapeDtypeStruct((n, kvw), BF16)],
        compiler_params=_params(("parallel",)),
        name="proj_att",
    )(xb, w_aq, w_ak, w_av, cos128, sin128)

    cos64, sin64 = _rope_tables(pos_rows, IDX_HD, LANES)
    w_ik2 = jnp.concatenate([w_ik, w_ik], axis=1)
    w_iw_pad = jnp.pad(w_iw, ((0, 0), (0, LANES - IDX_HEADS)))
    iq, ik, ikb, iw = pl.pallas_call(
        _idx_proj_kernel,
        grid=(nt,),
        in_specs=[x_spec, full((dm, IDX_HEADS * IDX_HD)), full((dm, LANES)), full((dm, LANES)),
                  tab_spec(LANES), tab_spec(LANES), full((1, LANES)), full((1, LANES))],
        out_specs=[row_spec(IDX_HEADS * IDX_HD), row_spec(IDX_HD), row_spec(IDX_HD), row_spec(IDX_HEADS)],
        out_shape=[jax.ShapeDtypeStruct((n, IDX_HEADS * IDX_HD), BF16),
                   jax.ShapeDtypeStruct((n, IDX_HD), F32), jax.ShapeDtypeStruct((n, IDX_HD), BF16),
                   jax.ShapeDtypeStruct((n, IDX_HEADS), F32)],
        compiler_params=_params(("parallel",)),
        name="proj_idx",
    )(xb, w_iq, w_ik2, w_iw_pad, cos64, sin64, ikg, ikb)
    return dict(rq=rq, rk=rk, rv=rv, rg=rg, sgab=sgab, aq=aq, ak=ak, av=av, akb=akb, avb=avb,
                iq=iq, ik=ik, ikb=ikb, iw=iw)


def _retention_kernel(sdec_ref, q_ref, k_ref, v_ref, gate_ref, gg_ref, gb_ref,
                      d_ref, cdec_ref, kdec_ref, si_ref, o_ref, so_ref):
    h = pl.program_id(1)
    c = pl.program_id(2)

    @pl.when(c == 0)
    def _():
        so_ref[0, 0] = si_ref[0, 0]

    q = q_ref[...]
    k = k_ref[...]
    v = v_ref[...]
    state = so_ref[0, 0]
    s = _dot_nt(q, k)
    inner = _dot((s * d_ref[0]).astype(BF16), v)
    cross = _dot(q, state.astype(BF16)) * cdec_ref[0]
    o = inner + cross
    kd = (k.astype(F32) * kdec_ref[0]).astype(BF16)
    so_ref[0, 0] = state * sdec_ref[h] + _dot_tn(kd, v)
    mu = jnp.mean(o, axis=-1, keepdims=True)
    var = jnp.mean(jnp.square(o - mu), axis=-1, keepdims=True)
    on = (o - mu) * lax.rsqrt(var + LN_EPS) * gg_ref[...] + gb_ref[...]
    o_ref[...] = (jax.nn.silu(gate_ref[...]) * on).astype(o_ref.dtype)


def _retention(rq, rk, rv, rg, gn_g, gn_b, state_in, chunk, chunk_eff):
    n = rq.shape[0]
    nb = state_in.shape[0]
    nc = n // (nb * chunk)
    lg = jnp.log1p(-jnp.exp2(-5.0 - jnp.arange(RET_HEADS, dtype=F32)))[:, None]
    i = jnp.arange(chunk, dtype=F32)
    diff = i[:, None] - i[None, :]
    decay = jnp.where(diff >= 0, jnp.exp(lg[:, :, None] * jnp.maximum(diff, 0.0)), 0.0)
    cdec = jnp.exp(lg * (i + 1.0))[:, :, None]
    kdec = jnp.where(i < chunk_eff, jnp.exp(lg * jnp.maximum(chunk_eff - 1.0 - i, 0.0)), 0.0)[:, :, None]
    sdec = jnp.exp(lg[:, 0] * chunk_eff)

    o, s_out = pl.pallas_call(
        _retention_kernel,
        grid=(nb, RET_HEADS, nc),
        in_specs=[
            pl.BlockSpec(memory_space=pltpu.SMEM),
            pl.BlockSpec((chunk, RET_DK), lambda b, h, c: (b * nc + c, h)),
            pl.BlockSpec((chunk, RET_DK), lambda b, h, c: (b * nc + c, h)),
            pl.BlockSpec((chunk, RET_DV), lambda b, h, c: (b * nc + c, h)),
            pl.BlockSpec((chunk, RET_DV), lambda b, h, c: (b * nc + c, h)),
            pl.BlockSpec((1, RET_DV), lambda b, h, c: (0, h)),
            pl.BlockSpec((1, RET_DV), lambda b, h, c: (0, h)),
            pl.BlockSpec((1, chunk, chunk), lambda b, h, c: (h, 0, 0)),
            pl.BlockSpec((1, chunk, 1), lambda b, h, c: (h, 0, 0)),
            pl.BlockSpec((1, chunk, 1), lambda b, h, c: (h, 0, 0)),
            pl.BlockSpec((1, 1, RET_DK, RET_DV), lambda b, h, c: (b, h, 0, 0)),
        ],
        out_specs=[
            pl.BlockSpec((chunk, RET_DV), lambda b, h, c: (b * nc + c, h)),
            pl.BlockSpec((1, 1, RET_DK, RET_DV), lambda b, h, c: (b, h, 0, 0)),
        ],
        out_shape=[jax.ShapeDtypeStruct((n, RET_HEADS * RET_DV), BF16),
                   jax.ShapeDtypeStruct(state_in.shape, F32)],
        compiler_params=_params(("parallel", "parallel", "arbitrary")),
        name="retention",
    )(sdec, rq, rk, rv, rg, gn_g, gn_b, decay, cdec, kdec, state_in)
    return o, s_out


M_INIT = -1e29


def _lanes(col):
    return jnp.broadcast_to(col, (col.shape[0], LANES))


def _kth_largest(reduce_tiles, mn, mx, kq):
    kq_b = _lanes(kq)

    def bisect_body(_, carry):
        lo, hi = carry
        mid = 0.5 * (lo + hi)
        cnt = reduce_tiles(lambda a, t: a + jnp.where(t >= mid, 1.0, 0.0), 0.0)
        ge = _lanes(jnp.sum(cnt, axis=-1, keepdims=True)) >= kq_b
        return jnp.where(ge, mid, lo), jnp.where(ge, hi, mid)

    lo, _ = lax.fori_loop(0, BISECT_ITERS, bisect_body, (_lanes(mn), _lanes(mx + 1.0)))
    tau = jnp.min(reduce_tiles(lambda a, t: jnp.minimum(a, jnp.where(t >= lo, t, INF)), INF),
                  axis=-1, keepdims=True)
    tau_b = _lanes(tau)
    cgt = jnp.sum(reduce_tiles(lambda a, t: a + jnp.where(t > tau_b, 1.0, 0.0), 0.0),
                  axis=-1, keepdims=True)
    return tau, kq - cgt


def _tie_select(sct, tau, need, run, tri):
    gt = sct > tau
    eq = sct == tau
    eqf = jnp.where(eq, 1.0, 0.0)
    prefix = _dot(eqf.astype(BF16), tri)
    sel = gt | (eq & ((run + prefix) <= need))
    return sel, run + jnp.sum(eqf, axis=-1, keepdims=True)


def _softmax_step(s, m_ref, l_ref, acc_ref, idx, v):
    m_old = m_ref[idx]
    m_new = jnp.maximum(m_old, jnp.max(s, axis=-1, keepdims=True))
    p = jnp.exp(s - m_new)
    alpha = jnp.exp(m_old - m_new)
    l_ref[idx] = alpha * l_ref[idx] + jnp.sum(p, axis=-1, keepdims=True)
    acc_ref[idx] = alpha * acc_ref[idx] + _dot(p.astype(BF16), v)
    m_ref[idx] = m_new


def _attn_prompt_kernel(aq_ref, iq_ref, iw_ref, k_ref, v_ref, kidx_ref, tri_ref, o_ref,
                        sc_ref, m_ref, l_ref, acc_ref, *, tq, topk):
    i = pl.program_id(1)
    nkv = i + 1
    qpos = i * tq + lax.broadcasted_iota(jnp.int32, (tq, 1), 0)
    kq = jnp.minimum(topk, qpos + 1).astype(F32)
    group = ATT_HEADS // ATT_KV_HEADS

    def score_body(j, carry):
        mn, mx = carry
        kx = kidx_ref[pl.ds(pl.multiple_of(j * tq, tq), tq), :]
        s = jnp.zeros((tq, tq), F32)
        for h in range(IDX_HEADS):
            d = _dot_nt(iq_ref[:, h * IDX_HD:(h + 1) * IDX_HD], kx) * (IDX_HD ** -0.5)
            s = s + iw_ref[:, h:h + 1] * jnp.maximum(d, 0.0)
        kpos = j * tq + lax.broadcasted_iota(jnp.int32, (tq, tq), 1)
        vis = kpos <= qpos
        sc_ref[j] = jnp.where(vis, s, -INF)
        mn = jnp.minimum(mn, jnp.min(jnp.where(vis, s, INF), axis=-1, keepdims=True))
        mx = jnp.maximum(mx, jnp.max(jnp.where(vis, s, -INF), axis=-1, keepdims=True))
        return mn, mx

    mn, mx = lax.fori_loop(0, nkv, score_body,
                           (jnp.full((tq, 1), INF, F32), jnp.full((tq, 1), -INF, F32)))

    taus, needs = [], []
    for r in range(tq // LANES):
        rows = slice(r * LANES, (r + 1) * LANES)

        def reduce_tiles(fn, init, rows=rows):
            def body(j, acc):
                t = sc_ref[j, rows, :]
                for c in range(tq // LANES):
                    acc = fn(acc, t[:, c * LANES:(c + 1) * LANES])
                return acc
            return lax.fori_loop(0, nkv, body, jnp.full((LANES, LANES), init, F32))

        tau_r, need_r = _kth_largest(reduce_tiles, mn[rows], mx[rows], kq[rows])
        taus.append(tau_r)
        needs.append(need_r)
    tau = jnp.concatenate(taus, axis=0)
    need = jnp.concatenate(needs, axis=0)

    tri = tri_ref[...]

    def mask_body(j, run):
        sel, run = _tie_select(sc_ref[j], tau, need, run, tri)
        sc_ref[j] = jnp.where(sel, 0.0, NEG)
        return run

    lax.fori_loop(0, nkv, mask_body, jnp.zeros((tq, 1), F32))

    m_ref[...] = jnp.full(m_ref.shape, M_INIT, F32)
    l_ref[...] = jnp.zeros(l_ref.shape, F32)
    acc_ref[...] = jnp.zeros(acc_ref.shape, F32)

    def att_body(j, carry):
        madd = sc_ref[j]
        rows = pl.ds(pl.multiple_of(j * tq, tq), tq)
        for h in range(ATT_HEADS):
            g = h // group
            kg = k_ref[rows, g * ATT_HD:(g + 1) * ATT_HD]
            vg = v_ref[rows, g * ATT_HD:(g + 1) * ATT_HD]
            s = _dot_nt(aq_ref[:, h * ATT_HD:(h + 1) * ATT_HD], kg) * (ATT_HD ** -0.5) + madd
            _softmax_step(s, m_ref, l_ref, acc_ref, h, vg)
        return carry

    lax.fori_loop(0, nkv, att_body, 0)
    for h in range(ATT_HEADS):
        o_ref[:, h * ATT_HD:(h + 1) * ATT_HD] = (acc_ref[h] / l_ref[h]).astype(o_ref.dtype)


def _attention_prompt(aq, iq, iw, akb, avb, ikb, nb, seq):
    tq = min(256, seq)
    assert seq % tq == 0 and tq % LANES == 0
    nq = seq // tq
    topk = min(TOPK_ATTN, seq // 4)
    n = aq.shape[0]
    tri = jnp.triu(jnp.ones((tq, tq), F32)).astype(BF16)
    kvw = ATT_KV_HEADS * ATT_HD
    return pl.pallas_call(
        functools.partial(_attn_prompt_kernel, tq=tq, topk=topk),
        grid=(nb, nq),
        in_specs=[
            pl.BlockSpec((tq, ATT_HEADS * ATT_HD), lambda b, i: (b * nq + i, 0)),
            pl.BlockSpec((tq, IDX_HEADS * IDX_HD), lambda b, i: (b * nq + i, 0)),
            pl.BlockSpec((tq, IDX_HEADS), lambda b, i: (b * nq + i, 0)),
            pl.BlockSpec((seq, kvw), lambda b, i: (b, 0)),
            pl.BlockSpec((seq, kvw), lambda b, i: (b, 0)),
            pl.BlockSpec((seq, IDX_HD), lambda b, i: (b, 0)),
            pl.BlockSpec((tq, tq), lambda b, i: (0, 0)),
        ],
        out_specs=pl.BlockSpec((tq, ATT_HEADS * ATT_HD), lambda b, i: (b * nq + i, 0)),
        out_shape=jax.ShapeDtypeStruct((n, ATT_HEADS * ATT_HD), BF16),
        scratch_shapes=[
            pltpu.VMEM((nq, tq, tq), F32),
            pltpu.VMEM((ATT_HEADS, tq, 1), F32),
            pltpu.VMEM((ATT_HEADS, tq, 1), F32),
            pltpu.VMEM((ATT_HEADS, tq, ATT_HD), F32),
        ],
        compiler_params=_params(("parallel", "arbitrary")),
        name="attention_prompt",
    )(aq, iq, iw, akb, avb, ikb, tri)


def _page_scores(iq_ref, iw_ref, kx):
    s = jnp.zeros((iq_ref.shape[1], kx.shape[0]), F32)
    for h in range(IDX_HEADS):
        d = _dot_nt(iq_ref[0, :, h * IDX_HD:(h + 1) * IDX_HD], kx) * (IDX_HD ** -0.5)
        s = s + iw_ref[0, :, h:h + 1] * jnp.maximum(d, 0.0)
    return s


def _sample_scores_kernel(pt_ref, iq_ref, iw_ref, *refs, npp, n_new, topk):
    page_refs = refs[:npp]
    knew_ref = refs[npp]
    sc_ref, tau_ref, need_ref = refs[npp + 1:]
    s = pl.program_id(1)
    npages = sc_ref.shape[1] - 1
    tp, page = sc_ref.shape[2], sc_ref.shape[3]

    kx = jnp.concatenate([r[0] for r in page_refs], axis=0).astype(BF16)
    sc = _page_scores(iq_ref, iw_ref, kx)
    for p in range(npp):
        sc_ref[0, s * npp + p] = sc[:, p * page:(p + 1) * page]

    @pl.when(s == pl.num_programs(1) - 1)
    def _():
        snew = _page_scores(iq_ref, iw_ref, knew_ref[0].astype(BF16))
        t = jnp.minimum(lax.broadcasted_iota(jnp.int32, (tp, page), 0), n_new - 1)
        j = lax.broadcasted_iota(jnp.int32, (tp, page), 1)
        sc_ref[0, npages] = jnp.where(j <= t, snew, -INF)

        def reduce_tiles(fn, init):
            def body(c, acc):
                for p in range(npp):
                    acc = fn(acc, sc_ref[0, c * npp + p])
                return acc
            acc = lax.fori_loop(0, npages // npp, body, jnp.full((tp, page), init, F32))
            return fn(acc, sc_ref[0, npages])

        mn = jnp.min(reduce_tiles(lambda a, x: jnp.minimum(a, jnp.where(x == -INF, INF, x)), INF),
                     axis=-1, keepdims=True)
        mx = jnp.max(reduce_tiles(jnp.maximum, -INF), axis=-1, keepdims=True)
        tau, need = _kth_largest(reduce_tiles, mn, mx, jnp.full((tp, 1), float(topk), F32))
        tau_ref[0] = _lanes(tau)
        need_ref[0] = _lanes(need)


def _sample_attend_kernel(pt_ref, q_ref, sc_ref, tau_ref, need_ref, tri_ref, *refs, npp):
    k_refs = refs[:npp]
    v_refs = refs[npp:2 * npp]
    knew_ref, vnew_ref, o_ref, m_ref, l_ref, acc_ref, run_ref = refs[2 * npp:]
    s = pl.program_id(1)
    npages = sc_ref.shape[1] - 1
    page = sc_ref.shape[3]
    group = ATT_HEADS // ATT_KV_HEADS
    grows = q_ref.shape[1] // ATT_KV_HEADS

    @pl.when(s == 0)
    def _():
        m_ref[...] = jnp.full(m_ref.shape, M_INIT, F32)
        l_ref[...] = jnp.zeros(l_ref.shape, F32)
        acc_ref[...] = jnp.zeros(acc_ref.shape, F32)
        run_ref[...] = jnp.zeros(run_ref.shape, F32)

    tau = tau_ref[0][:, :1]
    need = need_ref[0][:, :1]
    tri = tri_ref[...]

    def attend(pages, krefs, vrefs):
        run = run_ref[...]
        madds = []
        for pg in pages:
            sel, run = _tie_select(sc_ref[0, pg], tau, need, run, tri)
            madds.append(jnp.where(sel, 0.0, NEG))
        run_ref[...] = run
        madd = jnp.concatenate(madds, axis=1)
        maddg = jnp.concatenate([madd] * group, axis=0)
        for g in range(ATT_KV_HEADS):
            head_rows = pl.ds(g, page, stride=ATT_KV_HEADS)
            kg = jnp.concatenate([r[0, head_rows, :] for r in krefs], axis=0).astype(BF16)
            vg = jnp.concatenate([r[0, head_rows, :] for r in vrefs], axis=0).astype(BF16)
            sg = _dot_nt(q_ref[0, g * grows:(g + 1) * grows, :], kg) * (ATT_HD ** -0.5) + maddg
            _softmax_step(sg, m_ref, l_ref, acc_ref, g, vg)

    attend([s * npp + p for p in range(npp)], k_refs, v_refs)

    @pl.when(s == pl.num_programs(1) - 1)
    def _():
        attend([npages], [knew_ref], [vnew_ref])
        for g in range(ATT_KV_HEADS):
            o_ref[0, g * grows:(g + 1) * grows, :] = acc_ref[g] / l_ref[g]


def _attention_sample(aq, iq, iw, ak, av, ik, cache_k, cache_v, cache_kidx, page_table, nb, t_new):
    n_phys, page = cache_kidx.shape[0], cache_kidx.shape[1]
    npages = page_table.shape[1]
    npp = SAMPLE_PAGES_PER_STEP if npages % SAMPLE_PAGES_PER_STEP == 0 else 1
    nsteps = npages // npp
    tp = BF16_SUBLANES
    topk = min(TOPK_ATTN, (npages * page + t_new) // 4)
    prow = page * ATT_KV_HEADS

    def pad_rows(a, per_seq, rows):
        a = a.reshape(nb, per_seq, a.shape[-1])
        return jnp.pad(a, ((0, 0), (0, rows - per_seq), (0, 0)))

    iq_p, iw_p = pad_rows(iq, t_new, tp), pad_rows(iw, t_new, tp)
    q_p = pad_rows(aq, t_new, tp).reshape(nb, tp, ATT_HEADS, ATT_HD).transpose(0, 2, 1, 3)
    q_p = q_p.reshape(nb, ATT_HEADS * tp, ATT_HD)
    knew = pad_rows(ak, t_new * ATT_KV_HEADS, prow)
    vnew = pad_rows(av, t_new * ATT_KV_HEADS, prow)
    kinew = pad_rows(ik, t_new, page)
    ck = cache_k.reshape(n_phys, prow, ATT_HD)
    cv = cache_v.reshape(n_phys, prow, ATT_HD)
    tri = jnp.triu(jnp.ones((page, page), F32)).astype(BF16)

    def page_spec(rows, width, p):
        return pl.BlockSpec((1, rows, width), lambda b, s, pt, p=p: (pt[b, s * npp + p], 0, 0))

    def seq_spec(shape):
        return pl.BlockSpec((1,) + shape, lambda b, s, pt: (b,) + (0,) * len(shape))

    sc, tau, need = pl.pallas_call(
        functools.partial(_sample_scores_kernel, npp=npp, n_new=t_new, topk=topk),
        grid_spec=pltpu.PrefetchScalarGridSpec(
            num_scalar_prefetch=1,
            grid=(nb, nsteps),
            in_specs=[seq_spec((tp, IDX_HEADS * IDX_HD)), seq_spec((tp, IDX_HEADS))]
                     + [page_spec(page, IDX_HD, p) for p in range(npp)] + [seq_spec((page, IDX_HD))],
            out_specs=[seq_spec((npages + 1, tp, page)), seq_spec((tp, page)), seq_spec((tp, page))],
        ),
        out_shape=[jax.ShapeDtypeStruct((nb, npages + 1, tp, page), F32),
                   jax.ShapeDtypeStruct((nb, tp, page), F32),
                   jax.ShapeDtypeStruct((nb, tp, page), F32)],
        compiler_params=_params(("parallel", "arbitrary")),
        name="sample_scores",
    )(page_table, iq_p, iw_p, *([cache_kidx] * npp), kinew)

    grows = (ATT_HEADS // ATT_KV_HEADS) * tp
    o = pl.pallas_call(
        functools.partial(_sample_attend_kernel, npp=npp),
        grid_spec=pltpu.PrefetchScalarGridSpec(
            num_scalar_prefetch=1,
            grid=(nb, nsteps),
            in_specs=[seq_spec((ATT_HEADS * tp, ATT_HD)), seq_spec((npages + 1, tp, page)),
                      seq_spec((tp, page)), seq_spec((tp, page)),
                      pl.BlockSpec((page, page), lambda b, s, pt: (0, 0))]
                     + [page_spec(prow, ATT_HD, p) for p in range(npp)]
                     + [page_spec(prow, ATT_HD, p) for p in range(npp)]
                     + [seq_spec((prow, ATT_HD)), seq_spec((prow, ATT_HD))],
            out_specs=seq_spec((ATT_HEADS * tp, ATT_HD)),
            scratch_shapes=[
                pltpu.VMEM((ATT_KV_HEADS, grows, 1), F32),
                pltpu.VMEM((ATT_KV_HEADS, grows, 1), F32),
                pltpu.VMEM((ATT_KV_HEADS, grows, ATT_HD), F32),
                pltpu.VMEM((tp, 1), F32),
            ],
        ),
        out_shape=jax.ShapeDtypeStruct((nb, ATT_HEADS * tp, ATT_HD), F32),
        compiler_params=_params(("parallel", "arbitrary")),
        name="sample_attend",
    )(page_table, q_p, sc, tau, need, tri, *([ck] * npp), *([cv] * npp), knew, vnew)
    o = o.reshape(nb, ATT_HEADS, tp, ATT_HD)[:, :, :t_new].transpose(0, 2, 1, 3)
    return o.reshape(nb * t_new, ATT_HEADS * ATT_HD).astype(BF16)


def _layer_norm(y, g, b):
    mu = jnp.mean(y, axis=-1, keepdims=True)
    var = jnp.mean(jnp.square(y - mu), axis=-1, keepdims=True)
    return (y - mu) * lax.rsqrt(var + LN_EPS) * g + b


def _mix_kernel(x_ref, ret_ref, att_ref, sga_ref, sgb_ref, wr_ref, wa_ref, wo_ref, g_ref, b_ref,
                o_ref, ot_ref, *, alpha):
    branch = sga_ref[...] * _dot(ret_ref[...], wr_ref[...]) + sgb_ref[...] * _dot(att_ref[...], wa_ref[...])
    y = alpha * x_ref[...] + _dot(branch.astype(BF16), wo_ref[...])
    x1 = _layer_norm(y, g_ref[...], b_ref[...])
    o_ref[...] = x1
    ot_ref[...] = x1.T.astype(BF16)


def _mix(x, ret_o, att_o, sgab, w_ret_o, w_att_o, w_out, g, b, tm, alpha):
    n, dm = x.shape
    hv = RET_HEADS * RET_DV
    ha = ATT_HEADS * ATT_HD

    def full(shape):
        return pl.BlockSpec(shape, lambda i: (0,) * len(shape))

    return pl.pallas_call(
        functools.partial(_mix_kernel, alpha=alpha),
        grid=(n // tm,),
        in_specs=[pl.BlockSpec((tm, dm), lambda i: (i, 0)),
                  pl.BlockSpec((tm, hv), lambda i: (i, 0)),
                  pl.BlockSpec((tm, ha), lambda i: (i, 0)),
                  pl.BlockSpec((tm, dm), lambda i: (i, 0)),
                  pl.BlockSpec((tm, dm), lambda i: (i, 1)),
                  full((hv, dm)), full((ha, dm)), full((dm, dm)), full((1, dm)), full((1, dm))],
        out_specs=[pl.BlockSpec((tm, dm), lambda i: (i, 0)),
                   pl.BlockSpec((dm, tm), lambda i: (0, i))],
        out_shape=[jax.ShapeDtypeStruct((n, dm), F32), jax.ShapeDtypeStruct((dm, n), BF16)],
        compiler_params=_params(("parallel",)),
        name="branch_mix_ln1",
    )(x, ret_o, att_o, sgab, sgab, w_ret_o, w_att_o, w_out, g, b)


PEER_EXPERT_ROWS = 8
PEER_ROW_TILE = 32


def _top_values(cur, k):
    rows = []
    for _ in range(k):
        m = jnp.max(cur, axis=0, keepdims=True)
        rows.append(m)
        cur = jnp.where(cur == m, -INF, cur)
    return rows


def _peer_route_kernel(xt_ref, wqt_ref, sk_ref, c_ref, a_ref, s2_ref, e2_ref):
    xt = xt_ref[...]
    half = PEER_DKEY // 2
    k1 = PEER_TOPK + 1
    pairs = [(a, b) for a in range(k1) for b in range(k1 // (a + 1))]
    for h in range(PEER_HEADS):
        qt = _dot(wqt_ref[h * PEER_DKEY:(h + 1) * PEER_DKEY, :], xt).astype(BF16)
        s1_ref = c_ref.at[h]
        s1_ref[...] = _dot(sk_ref[h, 0], qt[:half])
        s2_ref[h] = _dot(sk_ref[h, 1], qt[half:])
        for lc in range(xt.shape[1] // LANES):
            ls = slice(lc * LANES, (lc + 1) * LANES)
            s1 = s1_ref[:, ls]
            s2 = s2_ref[h, :, ls]
            v1 = _top_values(s1, k1)
            v2 = _top_values(s2, k1)
            rows = [v1[a] + v2[b] for a, b in pairs]
            rows += [jnp.full_like(rows[0], -INF)] * ((-len(rows)) % 8)
            cand = jnp.concatenate(rows, axis=0)
            best = _top_values(cand, k1)
            theta = 0.5 * (best[PEER_TOPK - 1] + best[PEER_TOPK])
            z = jnp.sum(jnp.where(cand > theta, jnp.exp(cand - best[0]), 0.0), axis=0, keepdims=True)
            c_ref[h, :, ls] = theta - s1
            a_ref[h, :, ls] = jnp.exp(s1 - v1[0]) / z
            e2_ref[h, :, ls] = jnp.exp(s2 - v2[0])


def _peer_dense_kernel(xt_ref, c_ref, a_ref, s2_ref, e2_ref, u_ref, vt_ref, o_ref, act_ref, gm_ref):
    e = pl.program_id(1)
    nk = PEER_NKEYS
    nsub = PEER_EXPERT_ROWS
    tt = xt_ref.shape[1]

    @pl.when(e == 0)
    def _():
        o_ref[...] = jnp.zeros(o_ref.shape, F32)

    act = _dot(u_ref[...], xt_ref[...])
    act_ref[...] = 0.5 * act * (1.0 + lax.erf(act * (2.0 ** -0.5)))

    for lc in range(tt // LANES):
        ls = slice(lc * LANES, (lc + 1) * LANES)
        for rt in range(nk // PEER_ROW_TILE):
            rs = slice(rt * PEER_ROW_TILE, (rt + 1) * PEER_ROW_TILE)
            g = [jnp.zeros((PEER_ROW_TILE, LANES), F32)] * nsub
            for h in range(PEER_HEADS):
                s2 = s2_ref[h, rs, ls]
                e2 = e2_ref[h, rs, ls]
                for c in range(nsub):
                    g[c] = g[c] + jnp.where(s2 > c_ref[h, c:c + 1, ls], e2 * a_ref[h, c:c + 1, ls], 0.0)
            for c in range(nsub):
                er = slice(c * nk + rt * PEER_ROW_TILE, c * nk + (rt + 1) * PEER_ROW_TILE)
                gm_ref[er, ls] = (g[c] * act_ref[er, ls]).astype(BF16)

    o_ref[...] += _dot(vt_ref[...], gm_ref[...])


def _peer(x1t, wqt, subkeys, u, vt, tt):
    dm, n = x1t.shape
    assert n % tt == 0 and tt % LANES == 0
    nt = n // tt
    nk = PEER_NKEYS
    nsub = PEER_EXPERT_ROWS
    hshape = (PEER_HEADS, nk, n)
    hspec = pl.BlockSpec((PEER_HEADS, nk, tt), lambda i: (0, 0, i))
    c, a, s2, e2 = pl.pallas_call(
        _peer_route_kernel,
        grid=(nt,),
        in_specs=[pl.BlockSpec((dm, tt), lambda i: (0, i)),
                  pl.BlockSpec(wqt.shape, lambda i: (0, 0)),
                  pl.BlockSpec(subkeys.shape, lambda i: (0, 0, 0, 0))],
        out_specs=[hspec] * 4,
        out_shape=[jax.ShapeDtypeStruct(hshape, F32)] * 4,
        compiler_params=_params(("parallel",)),
        name="peer_route",
    )(x1t, wqt, subkeys)

    full_spec = pl.BlockSpec((PEER_HEADS, nk, tt), lambda i, e: (0, 0, i))
    row_spec = pl.BlockSpec((PEER_HEADS, nsub, tt), lambda i, e: (0, e, i))
    return pl.pallas_call(
        _peer_dense_kernel,
        grid=(nt, nk // nsub),
        in_specs=[pl.BlockSpec((dm, tt), lambda i, e: (0, i)),
                  row_spec, row_spec, full_spec, full_spec,
                  pl.BlockSpec((nsub * nk, dm), lambda i, e: (e, 0)),
                  pl.BlockSpec((dm, nsub * nk), lambda i, e: (0, e))],
        out_specs=pl.BlockSpec((dm, tt), lambda i, e: (0, i)),
        out_shape=jax.ShapeDtypeStruct((dm, n), F32),
        scratch_shapes=[pltpu.VMEM((nsub * nk, tt), F32), pltpu.VMEM((nsub * nk, tt), BF16)],
        compiler_params=_params(("parallel", "arbitrary")),
        name="peer_dense",
    )(x1t, c, a, s2, e2, u, vt)


def _final_kernel(x1_ref, pt_ref, p_ref, wg_ref, wp_ref, g_ref, b_ref, o_ref, *, alpha):
    x1 = x1_ref[...]
    ple = jax.nn.sigmoid(_dot(x1.astype(BF16), wg_ref[...])) * _dot(p_ref[...], wp_ref[...])
    y = alpha * x1 + pt_ref[...].T + ple
    o_ref[...] = _layer_norm(y, g_ref[...], b_ref[...])


def _final(x1, peer_t, p_emb, w_gate, w_ple, g, b, tm, alpha):
    n, dm = x1.shape
    pd = p_emb.shape[1]

    def full(shape):
        return pl.BlockSpec(shape, lambda i: (0,) * len(shape))

    return pl.pallas_call(
        functools.partial(_final_kernel, alpha=alpha),
        grid=(n // tm,),
        in_specs=[pl.BlockSpec((tm, dm), lambda i: (i, 0)),
                  pl.BlockSpec((dm, tm), lambda i: (0, i)),
                  pl.BlockSpec((tm, pd), lambda i: (i, 0)),
                  full((dm, dm)), full((pd, dm)), full((1, dm)), full((1, dm))],
        out_specs=pl.BlockSpec((tm, dm), lambda i: (i, 0)),
        out_shape=jax.ShapeDtypeStruct((n, dm), F32),
        compiler_params=_params(("parallel",)),
        name="ple_ln2",
    )(x1, peer_t, p_emb, w_gate, w_ple, g, b)


def _layer(x, p_emb, pos_rows, tm, weights, attention_fn, state_in, chunk, chunk_eff, alpha):
    n = x.shape[0]
    nb = state_in.shape[0]
    t_seq = n // nb
    pr = _project(x.astype(BF16), weights["w_in"], weights["ikg"], weights["ikb"], pos_rows, tm)

    def pad_chunks(a):
        if t_seq % chunk == 0:
            return a
        a = a.reshape(nb, t_seq, a.shape[-1])
        return jnp.pad(a, ((0, 0), (0, chunk - t_seq), (0, 0))).reshape(nb * chunk, a.shape[-1])

    ret_o, s_out = _retention(pad_chunks(pr["rq"]), pad_chunks(pr["rk"]), pad_chunks(pr["rv"]),
                              pad_chunks(pr["rg"]), weights["gn_g"], weights["gn_b"],
                              state_in, chunk, chunk_eff)
    if t_seq % chunk != 0:
        ret_o = ret_o.reshape(nb, chunk, -1)[:, :t_seq].reshape(n, -1)
    att_o = attention_fn(pr)
    x1, x1t = _mix(x, ret_o, att_o, pr["sgab"], weights["w_ret_o"], weights["w_att_o"], weights["w_out"],
                   weights["ln1_g"], weights["ln1_b"], tm, alpha)
    peer_t = _peer(x1t, weights["peer_wqt"], weights["peer_subkeys"], weights["peer_u"], weights["peer_vt"],
                   min(512, n))
    y = _final(x1, peer_t, p_emb.astype(BF16), weights["w_ple_gate"], weights["w_ple"],
               weights["ln2_g"], weights["ln2_b"], tm, alpha)
    return y, pr["ak"], pr["av"], pr["ik"], s_out


def kernel(x_prompt, x_sample, cache_k, cache_v, cache_kidx, state_ret, page_table, p_prompt, p_sample,
           w_in, idx_k_g, idx_k_b, gn_g, gn_b, w_ret_o, w_att_o, w_out, ln1_g, ln1_b, peer_wq,
           peer_subkeys, peer_u, peer_v, w_ple_gate, w_ple, ln2_g, ln2_b):
    depth = w_in.shape[0]
    assert depth == 1, "single-layer trunk"
    nbp, seq, dm = x_prompt.shape
    nbs, t_new, _ = x_sample.shape
    past_len = page_table.shape[1] * cache_k.shape[2]
    alpha = (2.0 * depth) ** 0.25
    i = 0
    weights = dict(
        w_in=w_in[i].astype(BF16),
        ikg=jnp.concatenate([idx_k_g[i], idx_k_g[i]])[None, :],
        ikb=jnp.concatenate([idx_k_b[i], idx_k_b[i]])[None, :],
        gn_g=gn_g[i][None, :], gn_b=gn_b[i][None, :],
        w_ret_o=w_ret_o[i].astype(BF16), w_att_o=w_att_o[i].astype(BF16), w_out=w_out[i].astype(BF16),
        ln1_g=ln1_g[i][None, :], ln1_b=ln1_b[i][None, :],
        peer_wqt=peer_wq[i].T.astype(BF16), peer_subkeys=peer_subkeys[i].astype(BF16),
        peer_u=peer_u[i].astype(BF16), peer_vt=peer_v[i].T.astype(BF16),
        w_ple_gate=w_ple_gate[i].astype(BF16), w_ple=w_ple[i].astype(BF16),
        ln2_g=ln2_g[i][None, :], ln2_b=ln2_b[i][None, :],
    )

    def attn_p(pr):
        return _attention_prompt(pr["aq"], pr["iq"], pr["iw"], pr["akb"], pr["avb"], pr["ikb"], nbp, seq)

    yp, kp, vp, ikp, rp = _layer(
        x_prompt.reshape(nbp * seq, dm), p_prompt[i].reshape(nbp * seq, -1),
        jnp.arange(seq, dtype=jnp.int32), min(512, seq), weights, attn_p,
        jnp.zeros((nbp,) + state_ret.shape[2:], F32), RET_CHUNK, RET_CHUNK, alpha)

    def attn_s(pr):
        return _attention_sample(pr["aq"], pr["iq"], pr["iw"], pr["ak"], pr["av"], pr["ik"],
                                 cache_k[i], cache_v[i], cache_kidx[i], page_table, nbs, t_new)

    ns = nbs * t_new
    pos_s = jnp.tile(past_len + jnp.arange(t_new, dtype=jnp.int32), nbs)
    ys, ks, vs, iks, rs = _layer(
        x_sample.reshape(ns, dm), p_sample[i].reshape(ns, -1), pos_s, ns, weights, attn_s,
        state_ret[i].astype(F32), BF16_SUBLANES, t_new, alpha)

    return (yp.reshape(nbp, seq, dm), ys.reshape(nbs, t_new, dm),
            kp.reshape(1, nbp, seq, ATT_KV_HEADS, ATT_HD), vp.reshape(1, nbp, seq, ATT_KV_HEADS, ATT_HD),
            ikp.reshape(1, nbp, seq, IDX_HD), rp[None].astype(state_ret.dtype),
            ks.reshape(1, nbs, t_new, ATT_KV_HEADS, ATT_HD), vs.reshape(1, nbs, t_new, ATT_KV_HEADS, ATT_HD),
            iks.reshape(1, nbs, t_new, IDX_HD), rs[None].astype(state_ret.dtype))
```

```python
import functools

import jax
import jax.numpy as jnp
from jax import lax
from jax.experimental import pallas as pl
from jax.experimental.pallas import tpu as pltpu

F32 = jnp.float32
BF16 = jnp.bfloat16

RET_HEADS = 4
RET_DK = 256
RET_DV = 512
RET_CHUNK = 256
ATT_HEADS = 8
ATT_KV_HEADS = 2
ATT_HD = 128
IDX_HEADS = 8
IDX_HD = 64
TOPK_ATTN = 256
ROPE_THETA = 10000.0
PEER_HEADS = 8
PEER_NKEYS = 128
PEER_DKEY = 256
PEER_TOPK = 16
LN_EPS = 1e-5
NEG = -1e30
INF = float("inf")

LANES = 128
BF16_SUBLANES = 16
VMEM_LIMIT = 48 * 1024 * 1024
BISECT_ITERS = 32
SAMPLE_PAGES_PER_STEP = 16


def _dot(a, b):
    return jnp.dot(a, b, preferred_element_type=F32)


def _dot_nt(a, b):
    return lax.dot_general(a, b, (((1,), (1,)), ((), ())), preferred_element_type=F32)


def _dot_tn(a, b):
    return lax.dot_general(a, b, (((0,), (0,)), ((), ())), preferred_element_type=F32)


def _params(sem):
    return pltpu.CompilerParams(dimension_semantics=sem, vmem_limit_bytes=VMEM_LIMIT)


def _rope_tables(pos, d, width):
    half = d // 2
    inv = ROPE_THETA ** (-jnp.arange(half, dtype=F32) / half)
    ang = pos.astype(F32)[:, None] * inv[None, :]
    cos, sin = jnp.cos(ang), jnp.sin(ang)
    cosf = jnp.concatenate([cos, cos], axis=-1)
    sinf = jnp.concatenate([-sin, sin], axis=-1)
    reps = width // d
    return jnp.tile(cosf, (1, reps)), jnp.tile(sinf, (1, reps))


def _partner(y, d):
    if d == 2 * LANES:
        return jnp.concatenate([y[:, LANES:], y[:, :LANES]], axis=-1)
    if d == LANES:
        return pltpu.roll(y, LANES // 2, axis=1)
    lane = lax.broadcasted_iota(jnp.int32, y.shape, 1)
    first_half = (lane & (d - 1)) < (d // 2)
    return jnp.where(first_half, pltpu.roll(y, LANES - d // 2, axis=1), pltpu.roll(y, d // 2, axis=1))


def _rope(y, cosf, sinf, d):
    return y * cosf + _partner(y, d) * sinf


def _ret_qk_kernel(x_ref, w_ref, cos_ref, sin_ref, q_ref, k_ref):
    x = x_ref[...]
    cosf, sinf = cos_ref[...], sin_ref[...]
    for h in range(2 * RET_HEADS):
        y = _dot(x, w_ref[:, h * RET_DK:(h + 1) * RET_DK])
        r = _rope(y, cosf, sinf, RET_DK)
        if h < RET_HEADS:
            q_ref[:, h * RET_DK:(h + 1) * RET_DK] = r.astype(BF16)
        else:
            hh = h - RET_HEADS
            k_ref[:, hh * RET_DK:(hh + 1) * RET_DK] = (r * (RET_DK ** -0.5)).astype(BF16)


def _plain_kernel(x_ref, w_ref, o_ref, *, sigmoid):
    y = _dot(x_ref[...], w_ref[...])
    if sigmoid:
        y = jax.nn.sigmoid(y)
    o_ref[...] = y.astype(o_ref.dtype)


def _att_proj_kernel(x_ref, wq_ref, wk_ref, wv_ref, cos_ref, sin_ref,
                     q_ref, k_ref, v_ref, kb_ref, vb_ref):
    x = x_ref[...]
    cosf, sinf = cos_ref[...], sin_ref[...]
    for h in range(ATT_HEADS):
        sl = slice(h * ATT_HD, (h + 1) * ATT_HD)
        q_ref[:, sl] = _rope(_dot(x, wq_ref[:, sl]), cosf, sinf, ATT_HD).astype(BF16)
    tm = x.shape[0]
    v = _dot(x, wv_ref[...])
    vb_ref[...] = v.astype(BF16)
    for h in range(ATT_KV_HEADS):
        sl = slice(h * ATT_HD, (h + 1) * ATT_HD)
        k = _rope(_dot(x, wk_ref[:, sl]), cosf, sinf, ATT_HD)
        kb_ref[:, sl] = k.astype(BF16)
        head_rows = pl.ds(h, tm, stride=ATT_KV_HEADS)
        k_ref[head_rows, :] = k
        v_ref[head_rows, :] = v[:, sl]


def _idx_proj_kernel(x_ref, wq_ref, wk_ref, ww_ref, cos_ref, sin_ref, g_ref, b_ref,
                     q_ref, k_ref, kb_ref, w_ref):
    x = x_ref[...]
    cosf, sinf = cos_ref[...], sin_ref[...]
    for c in range(IDX_HEADS * IDX_HD // LANES):
        sl = slice(c * LANES, (c + 1) * LANES)
        q_ref[:, sl] = _rope(_dot(x, wq_ref[:, sl]), cosf, sinf, IDX_HD).astype(BF16)
    y = _dot(x, wk_ref[...])
    mu = jnp.mean(y, axis=-1, keepdims=True)
    var = jnp.mean(jnp.square(y - mu), axis=-1, keepdims=True)
    yn = (y - mu) * lax.rsqrt(var + LN_EPS) * g_ref[...] + b_ref[...]
    k = yn * cosf + pltpu.roll(yn, IDX_HD // 2, axis=1) * sinf
    k_ref[...] = k[:, :IDX_HD]
    kb_ref[...] = k[:, :IDX_HD].astype(BF16)
    w = _dot(x, ww_ref[...])
    w_ref[...] = w[:, :IDX_HEADS] * (IDX_HEADS ** -0.5)


def _project(xb, w, ikg, ikb, pos_rows, tm):
    n, dm = xb.shape
    nt = n // tm
    rows = pos_rows.shape[0]
    tab_blocks = rows // tm
    hq = RET_HEADS * RET_DK
    hv = RET_HEADS * RET_DV
    o = 0
    w_rqk = w[:, o:o + 2 * hq]; o += 2 * hq
    w_rv = w[:, o:o + hv]; o += hv
    w_rg = w[:, o:o + hv]; o += hv
    w_aq = w[:, o:o + ATT_HEADS * ATT_HD]; o += ATT_HEADS * ATT_HD
    w_ak = w[:, o:o + ATT_KV_HEADS * ATT_HD]; o += ATT_KV_HEADS * ATT_HD
    w_av = w[:, o:o + ATT_KV_HEADS * ATT_HD]; o += ATT_KV_HEADS * ATT_HD
    w_iq = w[:, o:o + IDX_HEADS * IDX_HD]; o += IDX_HEADS * IDX_HD
    w_ik = w[:, o:o + IDX_HD]; o += IDX_HD
    w_iw = w[:, o:o + IDX_HEADS]; o += IDX_HEADS
    w_gab = w[:, o:]

    x_spec = pl.BlockSpec((tm, dm), lambda i: (i, 0))

    def tab_spec(width):
        return pl.BlockSpec((tm, width), lambda i: (i % tab_blocks, 0))

    def full(shape):
        return pl.BlockSpec(shape, lambda i: (0,) * len(shape))

    def row_spec(width):
        return pl.BlockSpec((tm, width), lambda i: (i, 0))

    cos256, sin256 = _rope_tables(pos_rows, RET_DK, RET_DK)
    rq, rk = pl.pallas_call(
        _ret_qk_kernel,
        grid=(nt,),
        in_specs=[x_spec, full((dm, 2 * hq)), tab_spec(RET_DK), tab_spec(RET_DK)],
        out_specs=[row_spec(hq), row_spec(hq)],
        out_shape=[jax.ShapeDtypeStruct((n, hq), BF16)] * 2,
        compiler_params=_params(("parallel",)),
        name="proj_ret_qk",
    )(xb, w_rqk, cos256, sin256)

    def plain(wc, dtype, sigmoid, name):
        cols = wc.shape[1]
        tn = min(1024, cols)
        return pl.pallas_call(
            functools.partial(_plain_kernel, sigmoid=sigmoid),
            grid=(cols // tn, nt),
            in_specs=[pl.BlockSpec((tm, dm), lambda j, i: (i, 0)),
                      pl.BlockSpec((dm, tn), lambda j, i: (0, j))],
            out_specs=pl.BlockSpec((tm, tn), lambda j, i: (i, j)),
            out_shape=jax.ShapeDtypeStruct((n, cols), dtype),
            compiler_params=_params(("parallel", "parallel")),
            name=name,
        )(xb, wc)

    rv = plain(w_rv, BF16, False, "proj_ret_v")
    rg = plain(w_rg, F32, False, "proj_ret_gate")
    sgab = plain(w_gab, F32, True, "proj_branch_gates")

    cos128, sin128 = _rope_tables(pos_rows, ATT_HD, ATT_HD)
    kvw = ATT_KV_HEADS * ATT_HD
    aq, ak, av, akb, avb = pl.pallas_call(
        _att_proj_kernel,
        grid=(nt,),
        in_specs=[x_spec, full((dm, ATT_HEADS * ATT_HD)), full((dm, kvw)), full((dm, kvw)),
                  tab_spec(ATT_HD), tab_spec(ATT_HD)],
        out_specs=[row_spec(ATT_HEADS * ATT_HD),
                   pl.BlockSpec((tm * ATT_KV_HEADS, ATT_HD), lambda i: (i, 0)),
                   pl.BlockSpec((tm * ATT_KV_HEADS, ATT_HD), lambda i: (i, 0)),
                   row_spec(kvw), row_spec(kvw)],
        out_shape=[jax.ShapeDtypeStruct((n, ATT_HEADS * ATT_HD), BF16),
                   jax.ShapeDtypeStruct((n * ATT_KV_HEADS, ATT_HD), F32),
                   jax.ShapeDtypeStruct((n * ATT_KV_HEADS, ATT_HD), F32),
                   jax.ShapeDtypeStruct((n, kvw), BF16), jax.ShapeDtypeStruct((n, kvw), BF16)],
        compiler_params=_params(("parallel",)),
        name="proj_att",
    )(xb, w_aq, w_ak, w_av, cos128, sin128)

    cos64, sin64 = _rope_tables(pos_rows, IDX_HD, LANES)
    w_ik2 = jnp.concatenate([w_ik, w_ik], axis=1)
    w_iw_pad = jnp.pad(w_iw, ((0, 0), (0, LANES - IDX_HEADS)))
    iq, ik, ikb, iw = pl.pallas_call(
        _idx_proj_kernel,
        grid=(nt,),
        in_specs=[x_spec, full((dm, IDX_HEADS * IDX_HD)), full((dm, LANES)), full((dm, LANES)),
                  tab_spec(LANES), tab_spec(LANES), full((1, LANES)), full((1, LANES))],
        out_specs=[row_spec(IDX_HEADS * IDX_HD), row_spec(IDX_HD), row_spec(IDX_HD), row_spec(IDX_HEADS)],
        out_shape=[jax.ShapeDtypeStruct((n, IDX_HEADS * IDX_HD), BF16),
                   jax.ShapeDtypeStruct((n, IDX_HD), F32), jax.ShapeDtypeStruct((n, IDX_HD), BF16),
                   jax.ShapeDtypeStruct((n, IDX_HEADS), F32)],
        compiler_params=_params(("parallel",)),
        name="proj_idx",
    )(xb, w_iq, w_ik2, w_iw_pad, cos64, sin64, ikg, ikb)
    return dict(rq=rq, rk=rk, rv=rv, rg=rg, sgab=sgab, aq=aq, ak=ak, av=av, akb=akb, avb=avb,
                iq=iq, ik=ik, ikb=ikb, iw=iw)


def _retention_kernel(sdec_ref, q_ref, k_ref, v_ref, gate_ref, gg_ref, gb_ref,
                      d_ref, cdec_ref, kdec_ref, si_ref, o_ref, so_ref):
    h = pl.program_id(1)
    c = pl.program_id(2)

    @pl.when(c == 0)
    def _():
        so_ref[0, 0] = si_ref[0, 0]

    q = q_ref[...]
    k = k_ref[...]
    v = v_ref[...]
    state = so_ref[0, 0]
    s = _dot_nt(q, k)
    inner = _dot((s * d_ref[0]).astype(BF16), v)
    cross = _dot(q, state.astype(BF16)) * cdec_ref[0]
    o = inner + cross
    kd = (k.astype(F32) * kdec_ref[0]).astype(BF16)
    so_ref[0, 0] = state * sdec_ref[h] + _dot_tn(kd, v)
    mu = jnp.mean(o, axis=-1, keepdims=True)
    var = jnp.mean(jnp.square(o - mu), axis=-1, keepdims=True)
    on = (o - mu) * lax.rsqrt(var + LN_EPS) * gg_ref[...] + gb_ref[...]
    o_ref[...] = (jax.nn.silu(gate_ref[...]) * on).astype(o_ref.dtype)


def _retention(rq, rk, rv, rg, gn_g, gn_b, state_in, chunk, chunk_eff):
    n = rq.shape[0]
    nb = state_in.shape[0]
    nc = n // (nb * chunk)
    lg = jnp.log1p(-jnp.exp2(-5.0 - jnp.arange(RET_HEADS, dtype=F32)))[:, None]
    i = jnp.arange(chunk, dtype=F32)
    diff = i[:, None] - i[None, :]
    decay = jnp.where(diff >= 0, jnp.exp(lg[:, :, None] * jnp.maximum(diff, 0.0)), 0.0)
    cdec = jnp.exp(lg * (i + 1.0))[:, :, None]
    kdec = jnp.where(i < chunk_eff, jnp.exp(lg * jnp.maximum(chunk_eff - 1.0 - i, 0.0)), 0.0)[:, :, None]
    sdec = jnp.exp(lg[:, 0] * chunk_eff)

    o, s_out = pl.pallas_call(
        _retention_kernel,
        grid=(nb, RET_HEADS, nc),
        in_specs=[
            pl.BlockSpec(memory_space=pltpu.SMEM),
            pl.BlockSpec((chunk, RET_DK), lambda b, h, c: (b * nc + c, h)),
            pl.BlockSpec((chunk, RET_DK), lambda b, h, c: (b * nc + c, h)),
            pl.BlockSpec((chunk, RET_DV), lambda b, h, c: (b * nc + c, h)),
            pl.BlockSpec((chunk, RET_DV), lambda b, h, c: (b * nc + c, h)),
            pl.BlockSpec((1, RET_DV), lambda b, h, c: (0, h)),
            pl.BlockSpec((1, RET_DV), lambda b, h, c: (0, h)),
            pl.BlockSpec((1, chunk, chunk), lambda b, h, c: (h, 0, 0)),
            pl.BlockSpec((1, chunk, 1), lambda b, h, c: (h, 0, 0)),
            pl.BlockSpec((1, chunk, 1), lambda b, h, c: (h, 0, 0)),
            pl.BlockSpec((1, 1, RET_DK, RET_DV), lambda b, h, c: (b, h, 0, 0)),
        ],
        out_specs=[
            pl.BlockSpec((chunk, RET_DV), lambda b, h, c: (b * nc + c, h)),
            pl.BlockSpec((1, 1, RET_DK, RET_DV), lambda b, h, c: (b, h, 0, 0)),
        ],
        out_shape=[jax.ShapeDtypeStruct((n, RET_HEADS * RET_DV), BF16),
                   jax.ShapeDtypeStruct(state_in.shape, F32)],
        compiler_params=_params(("parallel", "parallel", "arbitrary")),
        name="retention",
    )(sdec, rq, rk, rv, rg, gn_g, gn_b, decay, cdec, kdec, state_in)
    return o, s_out


M_INIT = -1e29


def _lanes(col):
    return jnp.broadcast_to(col, (col.shape[0], LANES))


def _kth_largest(reduce_tiles, mn, mx, kq):
    kq_b = _lanes(kq)

    def bisect_body(_, carry):
        lo, hi = carry
        mid = 0.5 * (lo + hi)
        cnt = reduce_tiles(lambda a, t: a + jnp.where(t >= mid, 1.0, 0.0), 0.0)
        ge = _lanes(jnp.sum(cnt, axis=-1, keepdims=True)) >= kq_b
        return jnp.where(ge, mid, lo), jnp.where(ge, hi, mid)

    lo, _ = lax.fori_loop(0, BISECT_ITERS, bisect_body, (_lanes(mn), _lanes(mx + 1.0)))
    tau = jnp.min(reduce_tiles(lambda a, t: jnp.minimum(a, jnp.where(t >= lo, t, INF)), INF),
                  axis=-1, keepdims=True)
    tau_b = _lanes(tau)
    cgt = jnp.sum(reduce_tiles(lambda a, t: a + jnp.where(t > tau_b, 1.0, 0.0), 0.0),
                  axis=-1, keepdims=True)
    return tau, kq - cgt


def _tie_select(sct, tau, need, run, tri):
    gt = sct > tau
    eq = sct == tau
    eqf = jnp.where(eq, 1.0, 0.0)
    prefix = _dot(eqf.astype(BF16), tri)
    sel = gt | (eq & ((run + prefix) <= need))
    return sel, run + jnp.sum(eqf, axis=-1, keepdims=True)


def _softmax_step(s, m_ref, l_ref, acc_ref, idx, v):
    m_old = m_ref[idx]
    m_new = jnp.maximum(m_old, jnp.max(s, axis=-1, keepdims=True))
    p = jnp.exp(s - m_new)
    alpha = jnp.exp(m_old - m_new)
    l_ref[idx] = alpha * l_ref[idx] + jnp.sum(p, axis=-1, keepdims=True)
    acc_ref[idx] = alpha * acc_ref[idx] + _dot(p.astype(BF16), v)
    m_ref[idx] = m_new


def _fold_rows(x):
    return jnp.sum(x.reshape(x.shape[0] // 8, 8, x.shape[1]), axis=0)


def _attn_prompt_kernel(aqt_ref, iqt_ref, iwt_ref, k_ref, vt_ref, kidx_ref, tril_ref, o_ref,
                        sc_ref, m_ref, l_ref, acc_ref, *, tq, topk):
    i = pl.program_id(1)
    nkv = i + 1
    qpos = i * tq + lax.broadcasted_iota(jnp.int32, (1, tq), 1)
    kq = jnp.minimum(topk, qpos + 1).astype(F32)
    group = ATT_HEADS // ATT_KV_HEADS

    def score_body(j, carry):
        mn, mx = carry
        kx = kidx_ref[pl.ds(pl.multiple_of(j * tq, tq), tq), :]
        s = jnp.zeros((tq, tq), F32)
        for h in range(IDX_HEADS):
            d = _dot(kx, iqt_ref[h * IDX_HD:(h + 1) * IDX_HD, :]) * (IDX_HD ** -0.5)
            s = s + iwt_ref[h:h + 1, :] * jnp.maximum(d, 0.0)
        kpos = j * tq + lax.broadcasted_iota(jnp.int32, (tq, tq), 0)
        vis = kpos <= qpos
        sc_ref[j] = jnp.where(vis, s, -INF)
        mn = jnp.minimum(mn, jnp.min(jnp.where(vis, s, INF), axis=0, keepdims=True))
        mx = jnp.maximum(mx, jnp.max(jnp.where(vis, s, -INF), axis=0, keepdims=True))
        return mn, mx

    mn, mx = lax.fori_loop(0, nkv, score_body,
                           (jnp.full((1, tq), INF, F32), jnp.full((1, tq), -INF, F32)))

    def count(pred):
        def body(j, acc):
            return acc + _fold_rows(jnp.where(pred(sc_ref[j]), 1.0, 0.0))
        return jnp.sum(lax.fori_loop(0, nkv, body, jnp.zeros((8, tq), F32)), axis=0, keepdims=True)

    def bisect_body(_, carry):
        lo, hi = carry
        mid = 0.5 * (lo + hi)
        ge = count(lambda t: t >= mid) >= kq
        return jnp.where(ge, mid, lo), jnp.where(ge, hi, mid)

    lo, _ = lax.fori_loop(0, BISECT_ITERS, bisect_body, (mn, mx + 1.0))

    def snap_body(j, acc):
        t = sc_ref[j]
        return jnp.minimum(acc, jnp.min(jnp.where(t >= lo, t, INF), axis=0, keepdims=True))

    tau = lax.fori_loop(0, nkv, snap_body, jnp.full((1, tq), INF, F32))
    need = kq - count(lambda t: t > tau)

    tril = tril_ref[...]

    def mask_body(j, run):
        t = sc_ref[j]
        eqf = jnp.where(t == tau, 1.0, 0.0)
        prefix = _dot(tril, eqf.astype(BF16))
        sel = (t > tau) | ((t == tau) & ((run + prefix) <= need))
        sc_ref[j] = jnp.where(sel, 0.0, NEG)
        return run + jnp.sum(eqf, axis=0, keepdims=True)

    lax.fori_loop(0, nkv, mask_body, jnp.zeros((1, tq), F32))

    m_ref[...] = jnp.full(m_ref.shape, M_INIT, F32)
    l_ref[...] = jnp.zeros(l_ref.shape, F32)
    acc_ref[...] = jnp.zeros(acc_ref.shape, F32)

    def att_body(j, carry):
        rows = pl.ds(pl.multiple_of(j * tq, tq), tq)
        for h in range(ATT_HEADS):
            g = h // group
            kg = k_ref[rows, g * ATT_HD:(g + 1) * ATT_HD]
            vg = vt_ref[0, j, g * ATT_HD:(g + 1) * ATT_HD, :]
            for c in range(tq // LANES):
                ls = slice(c * LANES, (c + 1) * LANES)
                s = _dot(kg, aqt_ref[h * ATT_HD:(h + 1) * ATT_HD, ls]) * (ATT_HD ** -0.5) + sc_ref[j, :, ls]
                m_old = m_ref[h, :, ls]
                m_new = jnp.maximum(m_old, jnp.max(s, axis=0, keepdims=True))
                p = jnp.exp(s - m_new)
                alpha = jnp.exp(m_old - m_new)
                l_ref[h, :, ls] = alpha * l_ref[h, :, ls] + jnp.sum(p, axis=0, keepdims=True)
                acc_ref[h, :, ls] = alpha * acc_ref[h, :, ls] + _dot(vg, p.astype(BF16))
                m_ref[h, :, ls] = m_new
        return carry

    lax.fori_loop(0, nkv, att_body, 0)
    for h in range(ATT_HEADS):
        o_ref[h * ATT_HD:(h + 1) * ATT_HD, :] = (acc_ref[h] / l_ref[h]).astype(o_ref.dtype)


def _attention_prompt(aq, iq, iw, akb, avb, ikb, nb, seq):
    tq = min(256, seq)
    assert seq % tq == 0 and tq % LANES == 0
    nq = seq // tq
    topk = min(TOPK_ATTN, seq // 4)
    n = aq.shape[0]
    tril = jnp.tril(jnp.ones((tq, tq), F32)).astype(BF16)
    kvw = ATT_KV_HEADS * ATT_HD
    vt = avb.reshape(nb, nq, tq, kvw).transpose(0, 1, 3, 2)
    ot = pl.pallas_call(
        functools.partial(_attn_prompt_kernel, tq=tq, topk=topk),
        grid=(nb, nq),
        in_specs=[
            pl.BlockSpec((ATT_HEADS * ATT_HD, tq), lambda b, i: (0, b * nq + i)),
            pl.BlockSpec((IDX_HEADS * IDX_HD, tq), lambda b, i: (0, b * nq + i)),
            pl.BlockSpec((IDX_HEADS, tq), lambda b, i: (0, b * nq + i)),
            pl.BlockSpec((seq, kvw), lambda b, i: (b, 0)),
            pl.BlockSpec((1, nq, kvw, tq), lambda b, i: (b, 0, 0, 0)),
            pl.BlockSpec((seq, IDX_HD), lambda b, i: (b, 0)),
            pl.BlockSpec((tq, tq), lambda b, i: (0, 0)),
        ],
        out_specs=pl.BlockSpec((ATT_HEADS * ATT_HD, tq), lambda b, i: (0, b * nq + i)),
        out_shape=jax.ShapeDtypeStruct((ATT_HEADS * ATT_HD, n), BF16),
        scratch_shapes=[
            pltpu.VMEM((nq, tq, tq), F32),
            pltpu.VMEM((ATT_HEADS, 1, tq), F32),
            pltpu.VMEM((ATT_HEADS, 1, tq), F32),
            pltpu.VMEM((ATT_HEADS, ATT_HD, tq), F32),
        ],
        compiler_params=_params(("parallel", "arbitrary")),
        name="attention_prompt",
    )(aq.T, iq.T, iw.T, akb, vt, ikb, tril)
    return ot.T


def _page_scores(iq_ref, iw_ref, kx):
    s = jnp.zeros((iq_ref.shape[1], kx.shape[0]), F32)
    for h in range(IDX_HEADS):
        d = _dot_nt(iq_ref[0, :, h * IDX_HD:(h + 1) * IDX_HD], kx) * (IDX_HD ** -0.5)
        s = s + iw_ref[0, :, h:h + 1] * jnp.maximum(d, 0.0)
    return s


def _sample_scores_kernel(pt_ref, iq_ref, iw_ref, *refs, npp, n_new, topk):
    page_refs = refs[:npp]
    knew_ref = refs[npp]
    sc_ref, tau_ref, need_ref = refs[npp + 1:]
    s = pl.program_id(1)
    npages = sc_ref.shape[1] - 1
    tp, page = sc_ref.shape[2], sc_ref.shape[3]

    kx = jnp.concatenate([r[0] for r in page_refs], axis=0).astype(BF16)
    sc = _page_scores(iq_ref, iw_ref, kx)
    for p in range(npp):
        sc_ref[0, s * npp + p] = sc[:, p * page:(p + 1) * page]

    @pl.when(s == pl.num_programs(1) - 1)
    def _():
        snew = _page_scores(iq_ref, iw_ref, knew_ref[0].astype(BF16))
        t = jnp.minimum(lax.broadcasted_iota(jnp.int32, (tp, page), 0), n_new - 1)
        j = lax.broadcasted_iota(jnp.int32, (tp, page), 1)
        sc_ref[0, npages] = jnp.where(j <= t, snew, -INF)

        def reduce_tiles(fn, init):
            def body(c, acc):
                for p in range(npp):
                    acc = fn(acc, sc_ref[0, c * npp + p])
                return acc
            acc = lax.fori_loop(0, npages // npp, body, jnp.full((tp, page), init, F32))
            return fn(acc, sc_ref[0, npages])

        mn = jnp.min(reduce_tiles(lambda a, x: jnp.minimum(a, jnp.where(x == -INF, INF, x)), INF),
                     axis=-1, keepdims=True)
        mx = jnp.max(reduce_tiles(jnp.maximum, -INF), axis=-1, keepdims=True)
        tau, need = _kth_largest(reduce_tiles, mn, mx, jnp.full((tp, 1), float(topk), F32))
        tau_ref[0] = _lanes(tau)
        need_ref[0] = _lanes(need)


def _sample_attend_kernel(pt_ref, q_ref, sc_ref, tau_ref, need_ref, tri_ref, *refs, npp):
    k_refs = refs[:npp]
    v_refs = refs[npp:2 * npp]
    knew_ref, vnew_ref, o_ref, m_ref, l_ref, acc_ref, run_ref = refs[2 * npp:]
    s = pl.program_id(1)
    npages = sc_ref.shape[1] - 1
    page = sc_ref.shape[3]
    group = ATT_HEADS // ATT_KV_HEADS
    grows = q_ref.shape[1] // ATT_KV_HEADS

    @pl.when(s == 0)
    def _():
        m_ref[...] = jnp.full(m_ref.shape, M_INIT, F32)
        l_ref[...] = jnp.zeros(l_ref.shape, F32)
        acc_ref[...] = jnp.zeros(acc_ref.shape, F32)
        run_ref[...] = jnp.zeros(run_ref.shape, F32)

    tau = tau_ref[0][:, :1]
    need = need_ref[0][:, :1]
    tri = tri_ref[...]

    def attend(pages, krefs, vrefs):
        run = run_ref[...]
        madds = []
        for pg in pages:
            sel, run = _tie_select(sc_ref[0, pg], tau, need, run, tri)
            madds.append(jnp.where(sel, 0.0, NEG))
        run_ref[...] = run
        madd = jnp.concatenate(madds, axis=1)
        maddg = jnp.concatenate([madd] * group, axis=0)
        for g in range(ATT_KV_HEADS):
            head_rows = pl.ds(g, page, stride=ATT_KV_HEADS)
            kg = jnp.concatenate([r[0, head_rows, :] for r in krefs], axis=0).astype(BF16)
            vg = jnp.concatenate([r[0, head_rows, :] for r in vrefs], axis=0).astype(BF16)
            sg = _dot_nt(q_ref[0, g * grows:(g + 1) * grows, :], kg) * (ATT_HD ** -0.5) + maddg
            _softmax_step(sg, m_ref, l_ref, acc_ref, g, vg)

    attend([s * npp + p for p in range(npp)], k_refs, v_refs)

    @pl.when(s == pl.num_programs(1) - 1)
    def _():
        attend([npages], [knew_ref], [vnew_ref])
        for g in range(ATT_KV_HEADS):
            o_ref[0, g * grows:(g + 1) * grows, :] = acc_ref[g] / l_ref[g]


def _attention_sample(aq, iq, iw, ak, av, ik, cache_k, cache_v, cache_kidx, page_table, nb, t_new):
    n_phys, page = cache_kidx.shape[0], cache_kidx.shape[1]
    npages = page_table.shape[1]
    npp = SAMPLE_PAGES_PER_STEP if npages % SAMPLE_PAGES_PER_STEP == 0 else 1
    nsteps = npages // npp
    tp = BF16_SUBLANES
    topk = min(TOPK_ATTN, (npages * page + t_new) // 4)
    prow = page * ATT_KV_HEADS

    def pad_rows(a, per_seq, rows):
        a = a.reshape(nb, per_seq, a.shape[-1])
        return jnp.pad(a, ((0, 0), (0, rows - per_seq), (0, 0)))

    iq_p, iw_p = pad_rows(iq, t_new, tp), pad_rows(iw, t_new, tp)
    q_p = pad_rows(aq, t_new, tp).reshape(nb, tp, ATT_HEADS, ATT_HD).transpose(0, 2, 1, 3)
    q_p = q_p.reshape(nb, ATT_HEADS * tp, ATT_HD)
    knew = pad_rows(ak, t_new * ATT_KV_HEADS, prow)
    vnew = pad_rows(av, t_new * ATT_KV_HEADS, prow)
    kinew = pad_rows(ik, t_new, page)
    ck = cache_k.reshape(n_phys, prow, ATT_HD)
    cv = cache_v.reshape(n_phys, prow, ATT_HD)
    tri = jnp.triu(jnp.ones((page, page), F32)).astype(BF16)

    def page_spec(rows, width, p):
        return pl.BlockSpec((1, rows, width), lambda b, s, pt, p=p: (pt[b, s * npp + p], 0, 0))

    def seq_spec(shape):
        return pl.BlockSpec((1,) + shape, lambda b, s, pt: (b,) + (0,) * len(shape))

    sc, tau, need = pl.pallas_call(
        functools.partial(_sample_scores_kernel, npp=npp, n_new=t_new, topk=topk),
        grid_spec=pltpu.PrefetchScalarGridSpec(
            num_scalar_prefetch=1,
            grid=(nb, nsteps),
            in_specs=[seq_spec((tp, IDX_HEADS * IDX_HD)), seq_spec((tp, IDX_HEADS))]
                     + [page_spec(page, IDX_HD, p) for p in range(npp)] + [seq_spec((page, IDX_HD))],
            out_specs=[seq_spec((npages + 1, tp, page)), seq_spec((tp, page)), seq_spec((tp, page))],
        ),
        out_shape=[jax.ShapeDtypeStruct((nb, npages + 1, tp, page), F32),
                   jax.ShapeDtypeStruct((nb, tp, page), F32),
                   jax.ShapeDtypeStruct((nb, tp, page), F32)],
        compiler_params=_params(("parallel", "arbitrary")),
        name="sample_scores",
    )(page_table, iq_p, iw_p, *([cache_kidx] * npp), kinew)

    grows = (ATT_HEADS // ATT_KV_HEADS) * tp
    o = pl.pallas_call(
        functools.partial(_sample_attend_kernel, npp=npp),
        grid_spec=pltpu.PrefetchScalarGridSpec(
            num_scalar_prefetch=1,
            grid=(nb, nsteps),
            in_specs=[seq_spec((ATT_HEADS * tp, ATT_HD)), seq_spec((npages + 1, tp, page)),
                      seq_spec((tp, page)), seq_spec((tp, page)),
                      pl.BlockSpec((page, page), lambda b, s, pt: (0, 0))]
                     + [page_spec(prow, ATT_HD, p) for p in range(npp)]
                     + [page_spec(prow, ATT_HD, p) for p in range(npp)]
                     + [seq_spec((prow, ATT_HD)), seq_spec((prow, ATT_HD))],
            out_specs=seq_spec((ATT_HEADS * tp, ATT_HD)),
            scratch_shapes=[
                pltpu.VMEM((ATT_KV_HEADS, grows, 1), F32),
                pltpu.VMEM((ATT_KV_HEADS, grows, 1), F32),
                pltpu.VMEM((ATT_KV_HEADS, grows, ATT_HD), F32),
                pltpu.VMEM((tp, 1), F32),
            ],
        ),
        out_shape=jax.ShapeDtypeStruct((nb, ATT_HEADS * tp, ATT_HD), F32),
        compiler_params=_params(("parallel", "arbitrary")),
        name="sample_attend",
    )(page_table, q_p, sc, tau, need, tri, *([ck] * npp), *([cv] * npp), knew, vnew)
    o = o.reshape(nb, ATT_HEADS, tp, ATT_HD)[:, :, :t_new].transpose(0, 2, 1, 3)
    return o.reshape(nb * t_new, ATT_HEADS * ATT_HD).astype(BF16)


def _layer_norm(y, g, b):
    mu = jnp.mean(y, axis=-1, keepdims=True)
    var = jnp.mean(jnp.square(y - mu), axis=-1, keepdims=True)
    return (y - mu) * lax.rsqrt(var + LN_EPS) * g + b


def _mix_kernel(x_ref, ret_ref, att_ref, sga_ref, sgb_ref, wr_ref, wa_ref, wo_ref, g_ref, b_ref,
                o_ref, ot_ref, *, alpha):
    branch = sga_ref[...] * _dot(ret_ref[...], wr_ref[...]) + sgb_ref[...] * _dot(att_ref[...], wa_ref[...])
    y = alpha * x_ref[...] + _dot(branch.astype(BF16), wo_ref[...])
    x1 = _layer_norm(y, g_ref[...], b_ref[...])
    o_ref[...] = x1
    ot_ref[...] = x1.T.astype(BF16)


def _mix(x, ret_o, att_o, sgab, w_ret_o, w_att_o, w_out, g, b, tm, alpha):
    n, dm = x.shape
    hv = RET_HEADS * RET_DV
    ha = ATT_HEADS * ATT_HD

    def full(shape):
        return pl.BlockSpec(shape, lambda i: (0,) * len(shape))

    return pl.pallas_call(
        functools.partial(_mix_kernel, alpha=alpha),
        grid=(n // tm,),
        in_specs=[pl.BlockSpec((tm, dm), lambda i: (i, 0)),
                  pl.BlockSpec((tm, hv), lambda i: (i, 0)),
                  pl.BlockSpec((tm, ha), lambda i: (i, 0)),
                  pl.BlockSpec((tm, dm), lambda i: (i, 0)),
                  pl.BlockSpec((tm, dm), lambda i: (i, 1)),
                  full((hv, dm)), full((ha, dm)), full((dm, dm)), full((1, dm)), full((1, dm))],
        out_specs=[pl.BlockSpec((tm, dm), lambda i: (i, 0)),
                   pl.BlockSpec((dm, tm), lambda i: (0, i))],
        out_shape=[jax.ShapeDtypeStruct((n, dm), F32), jax.ShapeDtypeStruct((dm, n), BF16)],
        compiler_params=_params(("parallel",)),
        name="branch_mix_ln1",
    )(x, ret_o, att_o, sgab, sgab, w_ret_o, w_att_o, w_out, g, b)


PEER_EXPERT_ROWS = 8
PEER_ROW_TILE = 32


def _top_values(cur, k):
    rows = []
    for _ in range(k):
        m = jnp.max(cur, axis=0, keepdims=True)
        rows.append(m)
        cur = jnp.where(cur == m, -INF, cur)
    return rows


def _peer_route_kernel(xt_ref, wqt_ref, sk_ref, c_ref, a_ref, s2_ref, e2_ref):
    xt = xt_ref[...]
    half = PEER_DKEY // 2
    k1 = PEER_TOPK + 1
    pairs = [(a, b) for a in range(k1) for b in range(k1 // (a + 1))]
    for h in range(PEER_HEADS):
        qt = _dot(wqt_ref[h * PEER_DKEY:(h + 1) * PEER_DKEY, :], xt).astype(BF16)
        s1_ref = c_ref.at[h]
        s1_ref[...] = _dot(sk_ref[h, 0], qt[:half])
        s2_ref[h] = _dot(sk_ref[h, 1], qt[half:])
        for lc in range(xt.shape[1] // LANES):
            ls = slice(lc * LANES, (lc + 1) * LANES)
            s1 = s1_ref[:, ls]
            s2 = s2_ref[h, :, ls]
            v1 = _top_values(s1, k1)
            v2 = _top_values(s2, k1)
            rows = [v1[a] + v2[b] for a, b in pairs]
            rows += [jnp.full_like(rows[0], -INF)] * ((-len(rows)) % 8)
            cand = jnp.concatenate(rows, axis=0)
            best = _top_values(cand, k1)
            theta = 0.5 * (best[PEER_TOPK - 1] + best[PEER_TOPK])
            z = jnp.sum(jnp.where(cand > theta, jnp.exp(cand - best[0]), 0.0), axis=0, keepdims=True)
            c_ref[h, :, ls] = theta - s1
            a_ref[h, :, ls] = jnp.exp(s1 - v1[0]) / z
            e2_ref[h, :, ls] = jnp.exp(s2 - v2[0])


def _peer_dense_kernel(xt_ref, c_ref, a_ref, s2_ref, e2_ref, u_ref, vt_ref, o_ref, act_ref, gm_ref):
    e = pl.program_id(1)
    nk = PEER_NKEYS
    nsub = PEER_EXPERT_ROWS
    tt = xt_ref.shape[1]

    @pl.when(e == 0)
    def _():
        o_ref[...] = jnp.zeros(o_ref.shape, F32)

    act = _dot(u_ref[...], xt_ref[...])
    act_ref[...] = 0.5 * act * (1.0 + lax.erf(act * (2.0 ** -0.5)))

    for lc in range(tt // LANES):
        ls = slice(lc * LANES, (lc + 1) * LANES)
        for rt in range(nk // PEER_ROW_TILE):
            rs = slice(rt * PEER_ROW_TILE, (rt + 1) * PEER_ROW_TILE)
            g = [jnp.zeros((PEER_ROW_TILE, LANES), F32)] * nsub
            for h in range(PEER_HEADS):
                s2 = s2_ref[h, rs, ls]
                e2 = e2_ref[h, rs, ls]
                for c in range(nsub):
                    g[c] = g[c] + jnp.where(s2 > c_ref[h, c:c + 1, ls], e2 * a_ref[h, c:c + 1, ls], 0.0)
            for c in range(nsub):
                er = slice(c * nk + rt * PEER_ROW_TILE, c * nk + (rt + 1) * PEER_ROW_TILE)
                gm_ref[er, ls] = (g[c] * act_ref[er, ls]).astype(BF16)

    o_ref[...] += _dot(vt_ref[...], gm_ref[...])


def _peer(x1t, wqt, subkeys, u, vt, tt):
    dm, n = x1t.shape
    assert n % tt == 0 and tt % LANES == 0
    nt = n // tt
    nk = PEER_NKEYS
    nsub = PEER_EXPERT_ROWS
    hshape = (PEER_HEADS, nk, n)
    hspec = pl.BlockSpec((PEER_HEADS, nk, tt), lambda i: (0, 0, i))
    c, a, s2, e2 = pl.pallas_call(
        _peer_route_kernel,
        grid=(nt,),
        in_specs=[pl.BlockSpec((dm, tt), lambda i: (0, i)),
                  pl.BlockSpec(wqt.shape, lambda i: (0, 0)),
                  pl.BlockSpec(subkeys.shape, lambda i: (0, 0, 0, 0))],
        out_specs=[hspec] * 4,
        out_shape=[jax.ShapeDtypeStruct(hshape, F32)] * 4,
        compiler_params=_params(("parallel",)),
        name="peer_route",
    )(x1t, wqt, subkeys)

    full_spec = pl.BlockSpec((PEER_HEADS, nk, tt), lambda i, e: (0, 0, i))
    row_spec = pl.BlockSpec((PEER_HEADS, nsub, tt), lambda i, e: (0, e, i))
    return pl.pallas_call(
        _peer_dense_kernel,
        grid=(nt, nk // nsub),
        in_specs=[pl.BlockSpec((dm, tt), lambda i, e: (0, i)),
                  row_spec, row_spec, full_spec, full_spec,
                  pl.BlockSpec((nsub * nk, dm), lambda i, e: (e, 0)),
                  pl.BlockSpec((dm, nsub * nk), lambda i, e: (0, e))],
        out_specs=pl.BlockSpec((dm, tt), lambda i, e: (0, i)),
        out_shape=jax.ShapeDtypeStruct((dm, n), F32),
        scratch_shapes=[pltpu.VMEM((nsub * nk, tt), F32), pltpu.VMEM((nsub * nk, tt), BF16)],
        compiler_params=_params(("parallel", "arbitrary")),
        name="peer_dense",
    )(x1t, c, a, s2, e2, u, vt)


def _final_kernel(x1_ref, pt_ref, p_ref, wg_ref, wp_ref, g_ref, b_ref, o_ref, *, alpha):
    x1 = x1_ref[...]
    ple = jax.nn.sigmoid(_dot(x1.astype(BF16), wg_ref[...])) * _dot(p_ref[...], wp_ref[...])
    y = alpha * x1 + pt_ref[...].T + ple
    o_ref[...] = _layer_norm(y, g_ref[...], b_ref[...])


def _final(x1, peer_t, p_emb, w_gate, w_ple, g, b, tm, alpha):
    n, dm = x1.shape
    pd = p_emb.shape[1]

    def full(shape):
        return pl.BlockSpec(shape, lambda i: (0,) * len(shape))

    return pl.pallas_call(
        functools.partial(_final_kernel, alpha=alpha),
        grid=(n // tm,),
        in_specs=[pl.BlockSpec((tm, dm), lambda i: (i, 0)),
                  pl.BlockSpec((dm, tm), lambda i: (0, i)),
                  pl.BlockSpec((tm, pd), lambda i: (i, 0)),
                  full((dm, dm)), full((pd, dm)), full((1, dm)), full((1, dm))],
        out_specs=pl.BlockSpec((tm, dm), lambda i: (i, 0)),
        out_shape=jax.ShapeDtypeStruct((n, dm), F32),
        compiler_params=_params(("parallel",)),
        name="ple_ln2",
    )(x1, peer_t, p_emb, w_gate, w_ple, g, b)


def _layer(x, p_emb, pos_rows, tm, weights, attention_fn, state_in, chunk, chunk_eff, alpha):
    n = x.shape[0]
    nb = state_in.shape[0]
    t_seq = n // nb
    pr = _project(x.astype(BF16), weights["w_in"], weights["ikg"], weights["ikb"], pos_rows, tm)

    def pad_chunks(a):
        if t_seq % chunk == 0:
            return a
        a = a.reshape(nb, t_seq, a.shape[-1])
        return jnp.pad(a, ((0, 0), (0, chunk - t_seq), (0, 0))).reshape(nb * chunk, a.shape[-1])

    ret_o, s_out = _retention(pad_chunks(pr["rq"]), pad_chunks(pr["rk"]), pad_chunks(pr["rv"]),
                              pad_chunks(pr["rg"]), weights["gn_g"], weights["gn_b"],
                              state_in, chunk, chunk_eff)
    if t_seq % chunk != 0:
        ret_o = ret_o.reshape(nb, chunk, -1)[:, :t_seq].reshape(n, -1)
    att_o = attention_fn(pr)
    x1, x1t = _mix(x, ret_o, att_o, pr["sgab"], weights["w_ret_o"], weights["w_att_o"], weights["w_out"],
                   weights["ln1_g"], weights["ln1_b"], tm, alpha)
    peer_t = _peer(x1t, weights["peer_wqt"], weights["peer_subkeys"], weights["peer_u"], weights["peer_vt"],
                   min(512, n))
    y = _final(x1, peer_t, p_emb.astype(BF16), weights["w_ple_gate"], weights["w_ple"],
               weights["ln2_g"], weights["ln2_b"], tm, alpha)
    return y, pr["ak"], pr["av"], pr["ik"], s_out


def kernel(x_prompt, x_sample, cache_k, cache_v, cache_kidx, state_ret, page_table, p_prompt, p_sample,
           w_in, idx_k_g, idx_k_b, gn_g, gn_b, w_ret_o, w_att_o, w_out, ln1_g, ln1_b, peer_wq,
           peer_subkeys, peer_u, peer_v, w_ple_gate, w_ple, ln2_g, ln2_b):
    depth = w_in.shape[0]
    assert depth == 1, "single-layer trunk"
    nbp, seq, dm = x_prompt.shape
    nbs, t_new, _ = x_sample.shape
    past_len = page_table.shape[1] * cache_k.shape[2]
    alpha = (2.0 * depth) ** 0.25
    i = 0
    weights = dict(
        w_in=w_in[i].astype(BF16),
        ikg=jnp.concatenate([idx_k_g[i], idx_k_g[i]])[None, :],
        ikb=jnp.concatenate([idx_k_b[i], idx_k_b[i]])[None, :],
        gn_g=gn_g[i][None, :], gn_b=gn_b[i][None, :],
        w_ret_o=w_ret_o[i].astype(BF16), w_att_o=w_att_o[i].astype(BF16), w_out=w_out[i].astype(BF16),
        ln1_g=ln1_g[i][None, :], ln1_b=ln1_b[i][None, :],
        peer_wqt=peer_wq[i].T.astype(BF16), peer_subkeys=peer_subkeys[i].astype(BF16),
        peer_u=peer_u[i].astype(BF16), peer_vt=peer_v[i].T.astype(BF16),
        w_ple_gate=w_ple_gate[i].astype(BF16), w_ple=w_ple[i].astype(BF16),
        ln2_g=ln2_g[i][None, :], ln2_b=ln2_b[i][None, :],
    )

    def attn_p(pr):
        return _attention_prompt(pr["aq"], pr["iq"], pr["iw"], pr["akb"], pr["avb"], pr["ikb"], nbp, seq)

    yp, kp, vp, ikp, rp = _layer(
        x_prompt.reshape(nbp * seq, dm), p_prompt[i].reshape(nbp * seq, -1),
        jnp.arange(seq, dtype=jnp.int32), min(512, seq), weights, attn_p,
        jnp.zeros((nbp,) + state_ret.shape[2:], F32), RET_CHUNK, RET_CHUNK, alpha)

    def attn_s(pr):
        return _attention_sample(pr["aq"], pr["iq"], pr["iw"], pr["ak"], pr["av"], pr["ik"],
                                 cache_k[i], cache_v[i], cache_kidx[i], page_table, nbs, t_new)

    ns = nbs * t_new
    pos_s = jnp.tile(past_len + jnp.arange(t_new, dtype=jnp.int32), nbs)
    ys, ks, vs, iks, rs = _layer(
        x_sample.reshape(ns, dm), p_sample[i].reshape(ns, -1), pos_s, ns, weights, attn_s,
        state_ret[i].astype(F32), BF16_SUBLANES, t_new, alpha)

    return (yp.reshape(nbp, seq, dm), ys.reshape(nbs, t_new, dm),
            kp.reshape(1, nbp, seq, ATT_KV_HEADS, ATT_HD), vp.reshape(1, nbp, seq, ATT_KV_HEADS, ATT_HD),
            ikp.reshape(1, nbp, seq, IDX_HD), rp[None].astype(state_ret.dtype),
            ks.reshape(1, nbs, t_new, ATT_KV_HEADS, ATT_HD), vs.reshape(1, nbs, t_new, ATT_KV_HEADS, ATT_HD),
            iks.reshape(1, nbs, t_new, IDX_HD), rs[None].astype(state_ret.dtype))
```

```python
import functools

import jax
import jax.numpy as jnp
from jax import lax
from jax.experimental import pallas as pl
from jax.experimental.pallas import tpu as pltpu

F32 = jnp.float32
BF16 = jnp.bfloat16

RET_HEADS = 4
RET_DK = 256
RET_DV = 512
RET_CHUNK = 256
ATT_HEADS = 8
ATT_KV_HEADS = 2
ATT_HD = 128
IDX_HEADS = 8
IDX_HD = 64
TOPK_ATTN = 256
ROPE_THETA = 10000.0
PEER_HEADS = 8
PEER_NKEYS = 128
PEER_DKEY = 256
PEER_TOPK = 16
LN_EPS = 1e-5
NEG = -1e30
INF = float("inf")

LANES = 128
BF16_SUBLANES = 16
VMEM_LIMIT = 48 * 1024 * 1024
BISECT_ITERS = 32
SAMPLE_PAGES_PER_STEP = 16


def _dot(a, b):
    return jnp.dot(a, b, preferred_element_type=F32)


def _dot_nt(a, b):
    return lax.dot_general(a, b, (((1,), (1,)), ((), ())), preferred_element_type=F32)


def _dot_tn(a, b):
    return lax.dot_general(a, b, (((0,), (0,)), ((), ())), preferred_element_type=F32)


def _params(sem):
    return pltpu.CompilerParams(dimension_semantics=sem, vmem_limit_bytes=VMEM_LIMIT)


def _rope_tables(pos, d, width):
    half = d // 2
    inv = ROPE_THETA ** (-jnp.arange(half, dtype=F32) / half)
    ang = pos.astype(F32)[:, None] * inv[None, :]
    cos, sin = jnp.cos(ang), jnp.sin(ang)
    cosf = jnp.concatenate([cos, cos], axis=-1)
    sinf = jnp.concatenate([-sin, sin], axis=-1)
    reps = width // d
    return jnp.tile(cosf, (1, reps)), jnp.tile(sinf, (1, reps))


def _partner(y, d):
    if d == 2 * LANES:
        return jnp.concatenate([y[:, LANES:], y[:, :LANES]], axis=-1)
    if d == LANES:
        return pltpu.roll(y, LANES // 2, axis=1)
    lane = lax.broadcasted_iota(jnp.int32, y.shape, 1)
    first_half = (lane & (d - 1)) < (d // 2)
    return jnp.where(first_half, pltpu.roll(y, LANES - d // 2, axis=1), pltpu.roll(y, d // 2, axis=1))


def _rope(y, cosf, sinf, d):
    return y * cosf + _partner(y, d) * sinf


def _ret_qk_kernel(x_ref, w_ref, cos_ref, sin_ref, q_ref, k_ref):
    x = x_ref[...]
    cosf, sinf = cos_ref[...], sin_ref[...]
    for h in range(2 * RET_HEADS):
        y = _dot(x, w_ref[:, h * RET_DK:(h + 1) * RET_DK])
        r = _rope(y, cosf, sinf, RET_DK)
        if h < RET_HEADS:
            q_ref[:, h * RET_DK:(h + 1) * RET_DK] = r.astype(BF16)
        else:
            hh = h - RET_HEADS
            k_ref[:, hh * RET_DK:(hh + 1) * RET_DK] = (r * (RET_DK ** -0.5)).astype(BF16)


def _plain_kernel(x_ref, w_ref, o_ref, *, sigmoid):
    y = _dot(x_ref[...], w_ref[...])
    if sigmoid:
        y = jax.nn.sigmoid(y)
    o_ref[...] = y.astype(o_ref.dtype)


def _att_proj_kernel(x_ref, wq_ref, wk_ref, wv_ref, cos_ref, sin_ref,
                     q_ref, k_ref, v_ref, kb_ref, vb_ref):
    x = x_ref[...]
    cosf, sinf = cos_ref[...], sin_ref[...]
    for h in range(ATT_HEADS):
        sl = slice(h * ATT_HD, (h + 1) * ATT_HD)
        q_ref[:, sl] = _rope(_dot(x, wq_ref[:, sl]), cosf, sinf, ATT_HD).astype(BF16)
    tm = x.shape[0]
    v = _dot(x, wv_ref[...])
    vb_ref[...] = v.astype(BF16)
    for h in range(ATT_KV_HEADS):
        sl = slice(h * ATT_HD, (h + 1) * ATT_HD)
        k = _rope(_dot(x, wk_ref[:, sl]), cosf, sinf, ATT_HD)
        kb_ref[:, sl] = k.astype(BF16)
        head_rows = pl.ds(h, tm, stride=ATT_KV_HEADS)
        k_ref[head_rows, :] = k
        v_ref[head_rows, :] = v[:, sl]


def _idx_proj_kernel(x_ref, wq_ref, wk_ref, ww_ref, cos_ref, sin_ref, g_ref, b_ref,
                     q_ref, k_ref, kb_ref, w_ref):
    x = x_ref[...]
    cosf, sinf = cos_ref[...], sin_ref[...]
    for c in range(IDX_HEADS * IDX_HD // LANES):
        sl = slice(c * LANES, (c + 1) * LANES)
        q_ref[:, sl] = _rope(_dot(x, wq_ref[:, sl]), cosf, sinf, IDX_HD).astype(BF16)
    y = _dot(x, wk_ref[...])
    mu = jnp.mean(y, axis=-1, keepdims=True)
    var = jnp.mean(jnp.square(y - mu), axis=-1, keepdims=True)
    yn = (y - mu) * lax.rsqrt(var + LN_EPS) * g_ref[...] + b_ref[...]
    k = yn * cosf + pltpu.roll(yn, IDX_HD // 2, axis=1) * sinf
    k_ref[...] = k[:, :IDX_HD]
    kb_ref[...] = k[:, :IDX_HD].astype(BF16)
    w = _dot(x, ww_ref[...])
    w_ref[...] = w[:, :IDX_HEADS] * (IDX_HEADS ** -0.5)


def _project(xb, w, ikg, ikb, pos_rows, tm):
    n, dm = xb.shape
    nt = n // tm
    rows = pos_rows.shape[0]
    tab_blocks = rows // tm
    hq = RET_HEADS * RET_DK
    hv = RET_HEADS * RET_DV
    o = 0
    w_rqk = w[:, o:o + 2 * hq]; o += 2 * hq
    w_rv = w[:, o:o + hv]; o += hv
    w_rg = w[:, o:o + hv]; o += hv
    w_aq = w[:, o:o + ATT_HEADS * ATT_HD]; o += ATT_HEADS * ATT_HD
    w_ak = w[:, o:o + ATT_KV_HEADS * ATT_HD]; o += ATT_KV_HEADS * ATT_HD
    w_av = w[:, o:o + ATT_KV_HEADS * ATT_HD]; o += ATT_KV_HEADS * ATT_HD
    w_iq = w[:, o:o + IDX_HEADS * IDX_HD]; o += IDX_HEADS * IDX_HD
    w_ik = w[:, o:o + IDX_HD]; o += IDX_HD
    w_iw = w[:, o:o + IDX_HEADS]; o += IDX_HEADS
    w_gab = w[:, o:]

    x_spec = pl.BlockSpec((tm, dm), lambda i: (i, 0))

    def tab_spec(width):
        return pl.BlockSpec((tm, width), lambda i: (i % tab_blocks, 0))

    def full(shape):
        return pl.BlockSpec(shape, lambda i: (0,) * len(shape))

    def row_spec(width):
        return pl.BlockSpec((tm, width), lambda i: (i, 0))

    cos256, sin256 = _rope_tables(pos_rows, RET_DK, RET_DK)
    rq, rk = pl.pallas_call(
        _ret_qk_kernel,
        grid=(nt,),
        in_specs=[x_spec, full((dm, 2 * hq)), tab_spec(RET_DK), tab_spec(RET_DK)],
        out_specs=[row_spec(hq), row_spec(hq)],
        out_shape=[jax.ShapeDtypeStruct((n, hq), BF16)] * 2,
        compiler_params=_params(("parallel",)),
        name="proj_ret_qk",
    )(xb, w_rqk, cos256, sin256)

    def plain(wc, dtype, sigmoid, name):
        cols = wc.shape[1]
        tn = min(1024, cols)
        return pl.pallas_call(
            functools.partial(_plain_kernel, sigmoid=sigmoid),
            grid=(cols // tn, nt),
            in_specs=[pl.BlockSpec((tm, dm), lambda j, i: (i, 0)),
                      pl.BlockSpec((dm, tn), lambda j, i: (0, j))],
            out_specs=pl.BlockSpec((tm, tn), lambda j, i: (i, j)),
            out_shape=jax.ShapeDtypeStruct((n, cols), dtype),
            compiler_params=_params(("parallel", "parallel")),
            name=name,
        )(xb, wc)

    rv = plain(w_rv, BF16, False, "proj_ret_v")
    rg = plain(w_rg, F32, False, "proj_ret_gate")
    sgab = plain(w_gab, F32, True, "proj_branch_gates")

    cos128, sin128 = _rope_tables(pos_rows, ATT_HD, ATT_HD)
    kvw = ATT_KV_HEADS * ATT_HD
    aq, ak, av, akb, avb = pl.pallas_call(
        _att_proj_kernel,
        grid=(nt,),
        in_specs=[x_spec, full((dm, ATT_HEADS * ATT_HD)), full((dm, kvw)), full((dm, kvw)),
                  tab_spec(ATT_HD), tab_spec(ATT_HD)],
        out_specs=[row_spec(ATT_HEADS * ATT_HD),
                   pl.BlockSpec((tm * ATT_KV_HEADS, ATT_HD), lambda i: (i, 0)),
                   pl.BlockSpec((tm * ATT_KV_HEADS, ATT_HD), lambda i: (i, 0)),
                   row_spec(kvw), row_spec(kvw)],
        out_shape=[jax.ShapeDtypeStruct((n, ATT_HEADS * ATT_HD), BF16),
                   jax.ShapeDtypeStruct((n * ATT_KV_HEADS, ATT_HD), F32),
                   jax.ShapeDtypeStruct((n * ATT_KV_HEADS, ATT_HD), F32),
                   jax.ShapeDtypeStruct((n, kvw), BF16), jax.ShapeDtypeStruct((n, kvw), BF16)],
        compiler_params=_params(("parallel",)),
        name="proj_att",
    )(xb, w_aq, w_ak, w_av, cos128, sin128)

    cos64, sin64 = _rope_tables(pos_rows, IDX_HD, LANES)
    w_ik2 = jnp.concatenate([w_ik, w_ik], axis=1)
    w_iw_pad = jnp.pad(w_iw, ((0, 0), (0, LANES - IDX_HEADS)))
    iq, ik, ikb, iw = pl.pallas_call(
        _idx_proj_kernel,
        grid=(nt,),
        in_specs=[x_spec, full((dm, IDX_HEADS * IDX_HD)), full((dm, LANES)), full((dm, LANES)),
                  tab_spec(LANES), tab_spec(LANES), full((1, LANES)), full((1, LANES))],
        out_specs=[row_spec(IDX_HEADS * IDX_HD), row_spec(IDX_HD), row_spec(IDX_HD), row_spec(IDX_HEADS)],
        out_shape=[jax.ShapeDtypeStruct((n, IDX_HEADS * IDX_HD), BF16),
                   jax.ShapeDtypeStruct((n, IDX_HD), F32), jax.ShapeDtypeStruct((n, IDX_HD), BF16),
                   jax.ShapeDtypeStruct((n, IDX_HEADS), F32)],
        compiler_params=_params(("parallel",)),
        name="proj_idx",
    )(xb, w_iq, w_ik2, w_iw_pad, cos64, sin64, ikg, ikb)
    return dict(rq=rq, rk=rk, rv=rv, rg=rg, sgab=sgab, aq=aq, ak=ak, av=av, akb=akb, avb=avb,
                iq=iq, ik=ik, ikb=ikb, iw=iw)


def _retention_kernel(sdec_ref, q_ref, k_ref, v_ref, gate_ref, gg_ref, gb_ref,
                      d_ref, cdec_ref, kdec_ref, si_ref, o_ref, so_ref):
    h = pl.program_id(1)
    c = pl.program_id(2)

    @pl.when(c == 0)
    def _():
        so_ref[0, 0] = si_ref[0, 0]

    q = q_ref[...]
    k = k_ref[...]
    v = v_ref[...]
    state = so_ref[0, 0]
    s = _dot_nt(q, k)
    inner = _dot((s * d_ref[0]).astype(BF16), v)
    cross = _dot(q, state.astype(BF16)) * cdec_ref[0]
    o = inner + cross
    kd = (k.astype(F32) * kdec_ref[0]).astype(BF16)
    so_ref[0, 0] = state * sdec_ref[h] + _dot_tn(kd, v)
    mu = jnp.mean(o, axis=-1, keepdims=True)
    var = jnp.mean(jnp.square(o - mu), axis=-1, keepdims=True)
    on = (o - mu) * lax.rsqrt(var + LN_EPS) * gg_ref[...] + gb_ref[...]
    o_ref[...] = (jax.nn.silu(gate_ref[...]) * on).astype(o_ref.dtype)


def _retention(rq, rk, rv, rg, gn_g, gn_b, state_in, chunk, chunk_eff):
    n = rq.shape[0]
    nb = state_in.shape[0]
    nc = n // (nb * chunk)
    lg = jnp.log1p(-jnp.exp2(-5.0 - jnp.arange(RET_HEADS, dtype=F32)))[:, None]
    i = jnp.arange(chunk, dtype=F32)
    diff = i[:, None] - i[None, :]
    decay = jnp.where(diff >= 0, jnp.exp(lg[:, :, None] * jnp.maximum(diff, 0.0)), 0.0)
    cdec = jnp.exp(lg * (i + 1.0))[:, :, None]
    kdec = jnp.where(i < chunk_eff, jnp.exp(lg * jnp.maximum(chunk_eff - 1.0 - i, 0.0)), 0.0)[:, :, None]
    sdec = jnp.exp(lg[:, 0] * chunk_eff)

    o, s_out = pl.pallas_call(
        _retention_kernel,
        grid=(nb, RET_HEADS, nc),
        in_specs=[
            pl.BlockSpec(memory_space=pltpu.SMEM),
            pl.BlockSpec((chunk, RET_DK), lambda b, h, c: (b * nc + c, h)),
            pl.BlockSpec((chunk, RET_DK), lambda b, h, c: (b * nc + c, h)),
            pl.BlockSpec((chunk, RET_DV), lambda b, h, c: (b * nc + c, h)),
            pl.BlockSpec((chunk, RET_DV), lambda b, h, c: (b * nc + c, h)),
            pl.BlockSpec((1, RET_DV), lambda b, h, c: (0, h)),
            pl.BlockSpec((1, RET_DV), lambda b, h, c: (0, h)),
            pl.BlockSpec((1, chunk, chunk), lambda b, h, c: (h, 0, 0)),
            pl.BlockSpec((1, chunk, 1), lambda b, h, c: (h, 0, 0)),
            pl.BlockSpec((1, chunk, 1), lambda b, h, c: (h, 0, 0)),
            pl.BlockSpec((1, 1, RET_DK, RET_DV), lambda b, h, c: (b, h, 0, 0)),
        ],
        out_specs=[
            pl.BlockSpec((chunk, RET_DV), lambda b, h, c: (b * nc + c, h)),
            pl.BlockSpec((1, 1, RET_DK, RET_DV), lambda b, h, c: (b, h, 0, 0)),
        ],
        out_shape=[jax.ShapeDtypeStruct((n, RET_HEADS * RET_DV), BF16),
                   jax.ShapeDtypeStruct(state_in.shape, F32)],
        compiler_params=_params(("parallel", "parallel", "arbitrary")),
        name="retention",
    )(sdec, rq, rk, rv, rg, gn_g, gn_b, decay, cdec, kdec, state_in)
    return o, s_out


M_INIT = -1e29


def _lanes(col):
    return jnp.broadcast_to(col, (col.shape[0], LANES))


def _kth_largest(reduce_tiles, mn, mx, kq):
    kq_b = _lanes(kq)

    def bisect_body(_, carry):
        lo, hi = carry
        mid = 0.5 * (lo + hi)
        cnt = reduce_tiles(lambda a, t: a + jnp.where(t >= mid, 1.0, 0.0), 0.0)
        ge = _lanes(jnp.sum(cnt, axis=-1, keepdims=True)) >= kq_b
        return jnp.where(ge, mid, lo), jnp.where(ge, hi, mid)

    lo, _ = lax.fori_loop(0, BISECT_ITERS, bisect_body, (_lanes(mn), _lanes(mx + 1.0)))
    tau = jnp.min(reduce_tiles(lambda a, t: jnp.minimum(a, jnp.where(t >= lo, t, INF)), INF),
                  axis=-1, keepdims=True)
    tau_b = _lanes(tau)
    cgt = jnp.sum(reduce_tiles(lambda a, t: a + jnp.where(t > tau_b, 1.0, 0.0), 0.0),
                  axis=-1, keepdims=True)
    return tau, kq - cgt


def _softmax_step(s, m_ref, l_ref, acc_ref, idx, v):
    m_old = m_ref[idx]
    m_new = jnp.maximum(m_old, jnp.max(s, axis=-1, keepdims=True))
    p = jnp.exp(s - m_new)
    alpha = jnp.exp(m_old - m_new)
    l_ref[idx] = alpha * l_ref[idx] + jnp.sum(p, axis=-1, keepdims=True)
    acc_ref[idx] = alpha * acc_ref[idx] + _dot(p.astype(BF16), v)
    m_ref[idx] = m_new


def _fold_rows(x):
    return jnp.sum(x.reshape(x.shape[0] // 8, 8, x.shape[1]), axis=0)


def _attn_prompt_kernel(aqt_ref, iqt_ref, iwt_ref, k_ref, vt_ref, kidx_ref, tril_ref, o_ref,
                        sc_ref, m_ref, l_ref, acc_ref, *, tq, topk):
    i = pl.program_id(1)
    nkv = i + 1
    qpos = i * tq + lax.broadcasted_iota(jnp.int32, (1, tq), 1)
    kq = jnp.minimum(topk, qpos + 1).astype(F32)
    group = ATT_HEADS // ATT_KV_HEADS

    def score_body(j, carry):
        mn, mx = carry
        kx = kidx_ref[pl.ds(pl.multiple_of(j * tq, tq), tq), :]
        s = jnp.zeros((tq, tq), F32)
        for h in range(IDX_HEADS):
            d = _dot(kx, iqt_ref[h * IDX_HD:(h + 1) * IDX_HD, :]) * (IDX_HD ** -0.5)
            s = s + iwt_ref[h:h + 1, :] * jnp.maximum(d, 0.0)
        kpos = j * tq + lax.broadcasted_iota(jnp.int32, (tq, tq), 0)
        vis = kpos <= qpos
        sc_ref[j] = jnp.where(vis, s, -INF)
        mn = jnp.minimum(mn, jnp.min(jnp.where(vis, s, INF), axis=0, keepdims=True))
        mx = jnp.maximum(mx, jnp.max(jnp.where(vis, s, -INF), axis=0, keepdims=True))
        return mn, mx

    mn, mx = lax.fori_loop(0, nkv, score_body,
                           (jnp.full((1, tq), INF, F32), jnp.full((1, tq), -INF, F32)))

    def count(pred):
        def body(j, acc):
            return acc + _fold_rows(jnp.where(pred(sc_ref[j]), 1.0, 0.0))
        return jnp.sum(lax.fori_loop(0, nkv, body, jnp.zeros((8, tq), F32)), axis=0, keepdims=True)

    def bisect_body(_, carry):
        lo, hi = carry
        mid = 0.5 * (lo + hi)
        ge = count(lambda t: t >= mid) >= kq
        return jnp.where(ge, mid, lo), jnp.where(ge, hi, mid)

    lo, _ = lax.fori_loop(0, BISECT_ITERS, bisect_body, (mn, mx + 1.0))

    def snap_body(j, acc):
        t = sc_ref[j]
        return jnp.minimum(acc, jnp.min(jnp.where(t >= lo, t, INF), axis=0, keepdims=True))

    tau = lax.fori_loop(0, nkv, snap_body, jnp.full((1, tq), INF, F32))
    need = kq - count(lambda t: t > tau)

    tril = tril_ref[...]

    def mask_body(j, run):
        t = sc_ref[j]
        eqf = jnp.where(t == tau, 1.0, 0.0)
        prefix = _dot(tril, eqf.astype(BF16))
        sel = (t > tau) | ((t == tau) & ((run + prefix) <= need))
        sc_ref[j] = jnp.where(sel, 0.0, NEG)
        return run + jnp.sum(eqf, axis=0, keepdims=True)

    lax.fori_loop(0, nkv, mask_body, jnp.zeros((1, tq), F32))

    m_ref[...] = jnp.full(m_ref.shape, M_INIT, F32)
    l_ref[...] = jnp.zeros(l_ref.shape, F32)
    acc_ref[...] = jnp.zeros(acc_ref.shape, F32)

    def att_body(j, carry):
        rows = pl.ds(pl.multiple_of(j * tq, tq), tq)
        for h in range(ATT_HEADS):
            g = h // group
            kg = k_ref[rows, g * ATT_HD:(g + 1) * ATT_HD]
            vg = vt_ref[0, j, g * ATT_HD:(g + 1) * ATT_HD, :]
            for c in range(tq // LANES):
                ls = slice(c * LANES, (c + 1) * LANES)
                s = _dot(kg, aqt_ref[h * ATT_HD:(h + 1) * ATT_HD, ls]) * (ATT_HD ** -0.5) + sc_ref[j, :, ls]
                m_old = m_ref[h, :, ls]
                m_new = jnp.maximum(m_old, jnp.max(s, axis=0, keepdims=True))
                p = jnp.exp(s - m_new)
                alpha = jnp.exp(m_old - m_new)
                l_ref[h, :, ls] = alpha * l_ref[h, :, ls] + jnp.sum(p, axis=0, keepdims=True)
                acc_ref[h, :, ls] = alpha * acc_ref[h, :, ls] + _dot(vg, p.astype(BF16))
                m_ref[h, :, ls] = m_new
        return carry

    lax.fori_loop(0, nkv, att_body, 0)
    for h in range(ATT_HEADS):
        o_ref[h * ATT_HD:(h + 1) * ATT_HD, :] = (acc_ref[h] / l_ref[h]).astype(o_ref.dtype)


def _attention_prompt(aq, iq, iw, akb, avb, ikb, nb, seq):
    tq = min(256, seq)
    assert seq % tq == 0 and tq % LANES == 0
    nq = seq // tq
    topk = min(TOPK_ATTN, seq // 4)
    n = aq.shape[0]
    tril = jnp.tril(jnp.ones((tq, tq), F32)).astype(BF16)
    kvw = ATT_KV_HEADS * ATT_HD
    vt = avb.reshape(nb, nq, tq, kvw).transpose(0, 1, 3, 2)
    ot = pl.pallas_call(
        functools.partial(_attn_prompt_kernel, tq=tq, topk=topk),
        grid=(nb, nq),
        in_specs=[
            pl.BlockSpec((ATT_HEADS * ATT_HD, tq), lambda b, i: (0, b * nq + i)),
            pl.BlockSpec((IDX_HEADS * IDX_HD, tq), lambda b, i: (0, b * nq + i)),
            pl.BlockSpec((IDX_HEADS, tq), lambda b, i: (0, b * nq + i)),
            pl.BlockSpec((seq, kvw), lambda b, i: (b, 0)),
            pl.BlockSpec((1, nq, kvw, tq), lambda b, i: (b, 0, 0, 0)),
            pl.BlockSpec((seq, IDX_HD), lambda b, i: (b, 0)),
            pl.BlockSpec((tq, tq), lambda b, i: (0, 0)),
        ],
        out_specs=pl.BlockSpec((ATT_HEADS * ATT_HD, tq), lambda b, i: (0, b * nq + i)),
        out_shape=jax.ShapeDtypeStruct((ATT_HEADS * ATT_HD, n), BF16),
        scratch_shapes=[
            pltpu.VMEM((nq, tq, tq), F32),
            pltpu.VMEM((ATT_HEADS, 1, tq), F32),
            pltpu.VMEM((ATT_HEADS, 1, tq), F32),
            pltpu.VMEM((ATT_HEADS, ATT_HD, tq), F32),
        ],
        compiler_params=_params(("parallel", "arbitrary")),
        name="attention_prompt",
    )(aq.T, iq.T, iw.T, akb, vt, ikb, tril)
    return ot.T


def _page_scores(iq_ref, iw_ref, kx):
    tp = iq_ref.shape[1] // IDX_HEADS
    d = _dot_nt(iq_ref[0], kx) * (IDX_HD ** -0.5)
    d = iw_ref[0] * jnp.maximum(d, 0.0)
    s = d[:tp]
    for h in range(1, IDX_HEADS):
        s = s + d[h * tp:(h + 1) * tp]
    return s


def _sample_scores_kernel(pt_ref, iq_ref, iw_ref, *refs, npp, n_new, topk):
    page_refs = refs[:npp]
    knew_ref = refs[npp]
    sc_ref, tau_ref, need_ref = refs[npp + 1:]
    s = pl.program_id(1)
    npages = sc_ref.shape[1] - 1
    tp, page = sc_ref.shape[2], sc_ref.shape[3]

    kx = jnp.concatenate([r[0] for r in page_refs], axis=0).astype(BF16)
    sc = _page_scores(iq_ref, iw_ref, kx)
    for p in range(npp):
        sc_ref[0, s * npp + p] = sc[:, p * page:(p + 1) * page]

    @pl.when(s == pl.num_programs(1) - 1)
    def _():
        snew = _page_scores(iq_ref, iw_ref, knew_ref[0].astype(BF16))
        t = jnp.minimum(lax.broadcasted_iota(jnp.int32, (tp, page), 0), n_new - 1)
        j = lax.broadcasted_iota(jnp.int32, (tp, page), 1)
        sc_ref[0, npages] = jnp.where(j <= t, snew, -INF)

        def reduce_tiles(fn, init):
            def body(c, acc):
                for p in range(npp):
                    acc = fn(acc, sc_ref[0, c * npp + p])
                return acc
            acc = lax.fori_loop(0, npages // npp, body, jnp.full((tp, page), init, F32))
            return fn(acc, sc_ref[0, npages])

        mn = jnp.min(reduce_tiles(lambda a, x: jnp.minimum(a, jnp.where(x == -INF, INF, x)), INF),
                     axis=-1, keepdims=True)
        mx = jnp.max(reduce_tiles(jnp.maximum, -INF), axis=-1, keepdims=True)
        tau, need = _kth_largest(reduce_tiles, mn, mx, jnp.full((tp, 1), float(topk), F32))
        tau_ref[0] = _lanes(tau)
        need_ref[0] = _lanes(need)


def _sample_attend_kernel(pt_ref, q_ref, sc_ref, tau_ref, need_ref, tri_ref, *refs, npp):
    k_refs = refs[:npp]
    v_refs = refs[npp:2 * npp]
    knew_ref, vnew_ref, o_ref, m_ref, l_ref, acc_ref, run_ref = refs[2 * npp:]
    s = pl.program_id(1)
    npages = sc_ref.shape[1] - 1
    page = sc_ref.shape[3]
    group = ATT_HEADS // ATT_KV_HEADS
    grows = q_ref.shape[1] // ATT_KV_HEADS

    @pl.when(s == 0)
    def _():
        m_ref[...] = jnp.full(m_ref.shape, M_INIT, F32)
        l_ref[...] = jnp.zeros(l_ref.shape, F32)
        acc_ref[...] = jnp.zeros(acc_ref.shape, F32)
        run_ref[...] = jnp.zeros(run_ref.shape, F32)

    tau = tau_ref[0][:, :1]
    need = need_ref[0][:, :1]
    tri = tri_ref[...]

    def attend(pages, krefs, vrefs):
        tp = sc_ref.shape[2]
        scs = [sc_ref[0, pg] for pg in pages]
        eqs = [jnp.where(t == tau, 1.0, 0.0) for t in scs]
        prefix = _dot(jnp.concatenate(eqs, axis=0).astype(BF16), tri)
        run = run_ref[...]
        madds = []
        for k, t in enumerate(scs):
            sel = (t > tau) | ((t == tau) & ((run + prefix[k * tp:(k + 1) * tp]) <= need))
            madds.append(jnp.where(sel, 0.0, NEG))
            run = run + jnp.sum(eqs[k], axis=-1, keepdims=True)
        run_ref[...] = run
        madd = jnp.concatenate(madds, axis=1)
        maddg = jnp.concatenate([madd] * group, axis=0)
        for g in range(ATT_KV_HEADS):
            head_rows = pl.ds(g, page, stride=ATT_KV_HEADS)
            kg = jnp.concatenate([r[0, head_rows, :] for r in krefs], axis=0).astype(BF16)
            vg = jnp.concatenate([r[0, head_rows, :] for r in vrefs], axis=0).astype(BF16)
            sg = _dot_nt(q_ref[0, g * grows:(g + 1) * grows, :], kg) * (ATT_HD ** -0.5) + maddg
            _softmax_step(sg, m_ref, l_ref, acc_ref, g, vg)

    attend([s * npp + p for p in range(npp)], k_refs, v_refs)

    @pl.when(s == pl.num_programs(1) - 1)
    def _():
        attend([npages], [knew_ref], [vnew_ref])
        for g in range(ATT_KV_HEADS):
            o_ref[0, g * grows:(g + 1) * grows, :] = acc_ref[g] / l_ref[g]


def _attention_sample(aq, iq, iw, ak, av, ik, cache_k, cache_v, cache_kidx, page_table, nb, t_new):
    n_phys, page = cache_kidx.shape[0], cache_kidx.shape[1]
    npages = page_table.shape[1]
    npp = SAMPLE_PAGES_PER_STEP if npages % SAMPLE_PAGES_PER_STEP == 0 else 1
    nsteps = npages // npp
    tp = BF16_SUBLANES
    topk = min(TOPK_ATTN, (npages * page + t_new) // 4)
    prow = page * ATT_KV_HEADS

    def pad_rows(a, per_seq, rows):
        a = a.reshape(nb, per_seq, a.shape[-1])
        return jnp.pad(a, ((0, 0), (0, rows - per_seq), (0, 0)))

    iq_p = pad_rows(iq, t_new, tp).reshape(nb, tp, IDX_HEADS, IDX_HD).transpose(0, 2, 1, 3)
    iq_p = iq_p.reshape(nb, IDX_HEADS * tp, IDX_HD)
    iw_p = pad_rows(iw, t_new, tp).transpose(0, 2, 1).reshape(nb, IDX_HEADS * tp, 1)
    q_p = pad_rows(aq, t_new, tp).reshape(nb, tp, ATT_HEADS, ATT_HD).transpose(0, 2, 1, 3)
    q_p = q_p.reshape(nb, ATT_HEADS * tp, ATT_HD)
    knew = pad_rows(ak, t_new * ATT_KV_HEADS, prow)
    vnew = pad_rows(av, t_new * ATT_KV_HEADS, prow)
    kinew = pad_rows(ik, t_new, page)
    ck = cache_k.reshape(n_phys, prow, ATT_HD)
    cv = cache_v.reshape(n_phys, prow, ATT_HD)
    tri = jnp.triu(jnp.ones((page, page), F32)).astype(BF16)

    def page_spec(rows, width, p):
        return pl.BlockSpec((1, rows, width), lambda b, s, pt, p=p: (pt[b, s * npp + p], 0, 0))

    def seq_spec(shape):
        return pl.BlockSpec((1,) + shape, lambda b, s, pt: (b,) + (0,) * len(shape))

    sc, tau, need = pl.pallas_call(
        functools.partial(_sample_scores_kernel, npp=npp, n_new=t_new, topk=topk),
        grid_spec=pltpu.PrefetchScalarGridSpec(
            num_scalar_prefetch=1,
            grid=(nb, nsteps),
            in_specs=[seq_spec((IDX_HEADS * tp, IDX_HD)), seq_spec((IDX_HEADS * tp, 1))]
                     + [page_spec(page, IDX_HD, p) for p in range(npp)] + [seq_spec((page, IDX_HD))],
            out_specs=[seq_spec((npages + 1, tp, page)), seq_spec((tp, page)), seq_spec((tp, page))],
        ),
        out_shape=[jax.ShapeDtypeStruct((nb, npages + 1, tp, page), F32),
                   jax.ShapeDtypeStruct((nb, tp, page), F32),
                   jax.ShapeDtypeStruct((nb, tp, page), F32)],
        compiler_params=_params(("parallel", "arbitrary")),
        name="sample_scores",
    )(page_table, iq_p, iw_p, *([cache_kidx] * npp), kinew)

    grows = (ATT_HEADS // ATT_KV_HEADS) * tp
    o = pl.pallas_call(
        functools.partial(_sample_attend_kernel, npp=npp),
        grid_spec=pltpu.PrefetchScalarGridSpec(
            num_scalar_prefetch=1,
            grid=(nb, nsteps),
            in_specs=[seq_spec((ATT_HEADS * tp, ATT_HD)), seq_spec((npages + 1, tp, page)),
                      seq_spec((tp, page)), seq_spec((tp, page)),
                      pl.BlockSpec((page, page), lambda b, s, pt: (0, 0))]
                     + [page_spec(prow, ATT_HD, p) for p in range(npp)]
                     + [page_spec(prow, ATT_HD, p) for p in range(npp)]
                     + [seq_spec((prow, ATT_HD)), seq_spec((prow, ATT_HD))],
            out_specs=seq_spec((ATT_HEADS * tp, ATT_HD)),
            scratch_shapes=[
                pltpu.VMEM((ATT_KV_HEADS, grows, 1), F32),
                pltpu.VMEM((ATT_KV_HEADS, grows, 1), F32),
                pltpu.VMEM((ATT_KV_HEADS, grows, ATT_HD), F32),
                pltpu.VMEM((tp, 1), F32),
            ],
        ),
        out_shape=jax.ShapeDtypeStruct((nb, ATT_HEADS * tp, ATT_HD), F32),
        compiler_params=_params(("parallel", "arbitrary")),
        name="sample_attend",
    )(page_table, q_p, sc, tau, need, tri, *([ck] * npp), *([cv] * npp), knew, vnew)
    o = o.reshape(nb, ATT_HEADS, tp, ATT_HD)[:, :, :t_new].transpose(0, 2, 1, 3)
    return o.reshape(nb * t_new, ATT_HEADS * ATT_HD).astype(BF16)


def _layer_norm(y, g, b):
    mu = jnp.mean(y, axis=-1, keepdims=True)
    var = jnp.mean(jnp.square(y - mu), axis=-1, keepdims=True)
    return (y - mu) * lax.rsqrt(var + LN_EPS) * g + b


def _mix_kernel(x_ref, ret_ref, att_ref, sga_ref, sgb_ref, wr_ref, wa_ref, wo_ref, g_ref, b_ref,
                o_ref, ot_ref, *, alpha):
    branch = sga_ref[...] * _dot(ret_ref[...], wr_ref[...]) + sgb_ref[...] * _dot(att_ref[...], wa_ref[...])
    y = alpha * x_ref[...] + _dot(branch.astype(BF16), wo_ref[...])
    x1 = _layer_norm(y, g_ref[...], b_ref[...])
    o_ref[...] = x1
    ot_ref[...] = x1.T.astype(BF16)


def _mix(x, ret_o, att_o, sgab, w_ret_o, w_att_o, w_out, g, b, tm, alpha):
    n, dm = x.shape
    hv = RET_HEADS * RET_DV
    ha = ATT_HEADS * ATT_HD

    def full(shape):
        return pl.BlockSpec(shape, lambda i: (0,) * len(shape))

    return pl.pallas_call(
        functools.partial(_mix_kernel, alpha=alpha),
        grid=(n // tm,),
        in_specs=[pl.BlockSpec((tm, dm), lambda i: (i, 0)),
                  pl.BlockSpec((tm, hv), lambda i: (i, 0)),
                  pl.BlockSpec((tm, ha), lambda i: (i, 0)),
                  pl.BlockSpec((tm, dm), lambda i: (i, 0)),
                  pl.BlockSpec((tm, dm), lambda i: (i, 1)),
                  full((hv, dm)), full((ha, dm)), full((dm, dm)), full((1, dm)), full((1, dm))],
        out_specs=[pl.BlockSpec((tm, dm), lambda i: (i, 0)),
                   pl.BlockSpec((dm, tm), lambda i: (0, i))],
        out_shape=[jax.ShapeDtypeStruct((n, dm), F32), jax.ShapeDtypeStruct((dm, n), BF16)],
        compiler_params=_params(("parallel",)),
        name="branch_mix_ln1",
    )(x, ret_o, att_o, sgab, sgab, w_ret_o, w_att_o, w_out, g, b)


PEER_EXPERT_ROWS = 8
PEER_ROW_TILE = 32


def _top_values(cur, k):
    rows = []
    for _ in range(k):
        m = jnp.max(cur, axis=0, keepdims=True)
        rows.append(m)
        cur = jnp.where(cur == m, -INF, cur)
    return rows


def _peer_route_kernel(xt_ref, wqt_ref, sk_ref, c_ref, a_ref, s2_ref, e2_ref):
    xt = xt_ref[...]
    half = PEER_DKEY // 2
    k1 = PEER_TOPK + 1
    pairs = [(a, b) for a in range(k1) for b in range(k1 // (a + 1))]
    for h in range(PEER_HEADS):
        qt = _dot(wqt_ref[h * PEER_DKEY:(h + 1) * PEER_DKEY, :], xt).astype(BF16)
        s1_ref = c_ref.at[h]
        s1_ref[...] = _dot(sk_ref[h, 0], qt[:half])
        s2_ref[h] = _dot(sk_ref[h, 1], qt[half:])
        for lc in range(xt.shape[1] // LANES):
            ls = slice(lc * LANES, (lc + 1) * LANES)
            s1 = s1_ref[:, ls]
            s2 = s2_ref[h, :, ls]
            v1 = _top_values(s1, k1)
            v2 = _top_values(s2, k1)
            rows = [v1[a] + v2[b] for a, b in pairs]
            rows += [jnp.full_like(rows[0], -INF)] * ((-len(rows)) % 8)
            cand = jnp.concatenate(rows, axis=0)
            best = _top_values(cand, k1)
            theta = 0.5 * (best[PEER_TOPK - 1] + best[PEER_TOPK])
            z = jnp.sum(jnp.where(cand > theta, jnp.exp(cand - best[0]), 0.0), axis=0, keepdims=True)
            c_ref[h, :, ls] = theta - s1
            a_ref[h, :, ls] = jnp.exp(s1 - v1[0]) / z
            e2_ref[h, :, ls] = jnp.exp(s2 - v2[0])


def _peer_dense_kernel(xt_ref, c_ref, a_ref, s2_ref, e2_ref, u_ref, vt_ref, o_ref, act_ref, gm_ref):
    e = pl.program_id(1)
    nk = PEER_NKEYS
    nsub = PEER_EXPERT_ROWS
    tt = xt_ref.shape[1]

    @pl.when(e == 0)
    def _():
        o_ref[...] = jnp.zeros(o_ref.shape, F32)

    def activations(rows, ls):
        act = _dot(u_ref[rows, :], xt_ref[:, ls])
        act_ref[rows, ls] = 0.5 * act * (1.0 + lax.erf(act * (2.0 ** -0.5)))

    def weights_tile(lc, rt):
        ls = slice(lc * LANES, (lc + 1) * LANES)
        rs = slice(rt * PEER_ROW_TILE, (rt + 1) * PEER_ROW_TILE)
        g = [jnp.zeros((PEER_ROW_TILE, LANES), F32)] * nsub
        for h in range(PEER_HEADS):
            s2 = s2_ref[h, rs, ls]
            e2 = e2_ref[h, rs, ls]
            for c in range(nsub):
                g[c] = g[c] + jnp.where(s2 > c_ref[h, c:c + 1, ls], e2 * a_ref[h, c:c + 1, ls], 0.0)
        for c in range(nsub):
            er = slice(c * nk + rt * PEER_ROW_TILE, c * nk + (rt + 1) * PEER_ROW_TILE)
            gm_ref[er, ls] = (g[c] * act_ref[er, ls]).astype(BF16)

    def outputs(rows, ls):
        o_ref[rows, ls] += _dot(vt_ref[rows, :], gm_ref[:, ls])

    nrt = nk // PEER_ROW_TILE
    nlc = tt // LANES
    if nlc % 2:
        activations(slice(None), slice(None))
        for lc in range(nlc):
            for rt in range(nrt):
                weights_tile(lc, rt)
        outputs(slice(None), slice(None))
        return

    halves = [slice(0, tt // 2), slice(tt // 2, tt)]
    npiece = 4
    arows = [slice(q * (nsub * nk // npiece), (q + 1) * (nsub * nk // npiece)) for q in range(npiece)]
    orows = [slice(q * (o_ref.shape[0] // npiece), (q + 1) * (o_ref.shape[0] // npiece)) for q in range(npiece)]
    tiles = [[(lc, rt) for lc in range(hf * nlc // 2, (hf + 1) * nlc // 2) for rt in range(nrt)] for hf in range(2)]
    per = len(tiles[0]) // npiece
    for q in range(npiece):
        activations(arows[q], halves[0])
    for q in range(npiece):
        activations(arows[q], halves[1])
        for lc, rt in tiles[0][q * per:(q + 1) * per]:
            weights_tile(lc, rt)
    for q in range(npiece):
        outputs(orows[q], halves[0])
        for lc, rt in tiles[1][q * per:(q + 1) * per]:
            weights_tile(lc, rt)
    for q in range(npiece):
        outputs(orows[q], halves[1])


def _peer(x1t, wqt, subkeys, u, vt, tt):
    dm, n = x1t.shape
    assert n % tt == 0 and tt % LANES == 0
    nt = n // tt
    nk = PEER_NKEYS
    nsub = PEER_EXPERT_ROWS
    hshape = (PEER_HEADS, nk, n)
    hspec = pl.BlockSpec((PEER_HEADS, nk, tt), lambda i: (0, 0, i))
    c, a, s2, e2 = pl.pallas_call(
        _peer_route_kernel,
        grid=(nt,),
        in_specs=[pl.BlockSpec((dm, tt), lambda i: (0, i)),
                  pl.BlockSpec(wqt.shape, lambda i: (0, 0)),
                  pl.BlockSpec(subkeys.shape, lambda i: (0, 0, 0, 0))],
        out_specs=[hspec] * 4,
        out_shape=[jax.ShapeDtypeStruct(hshape, F32)] * 4,
        compiler_params=_params(("parallel",)),
        name="peer_route",
    )(x1t, wqt, subkeys)

    full_spec = pl.BlockSpec((PEER_HEADS, nk, tt), lambda i, e: (0, 0, i))
    row_spec = pl.BlockSpec((PEER_HEADS, nsub, tt), lambda i, e: (0, e, i))
    return pl.pallas_call(
        _peer_dense_kernel,
        grid=(nt, nk // nsub),
        in_specs=[pl.BlockSpec((dm, tt), lambda i, e: (0, i)),
                  row_spec, row_spec, full_spec, full_spec,
                  pl.BlockSpec((nsub * nk, dm), lambda i, e: (e, 0)),
                  pl.BlockSpec((dm, nsub * nk), lambda i, e: (0, e))],
        out_specs=pl.BlockSpec((dm, tt), lambda i, e: (0, i)),
        out_shape=jax.ShapeDtypeStruct((dm, n), F32),
        scratch_shapes=[pltpu.VMEM((nsub * nk, tt), F32), pltpu.VMEM((nsub * nk, tt), BF16)],
        compiler_params=_params(("parallel", "arbitrary")),
        name="peer_dense",
    )(x1t, c, a, s2, e2, u, vt)


def _final_kernel(x1_ref, pt_ref, p_ref, wg_ref, wp_ref, g_ref, b_ref, o_ref, *, alpha):
    x1 = x1_ref[...]
    ple = jax.nn.sigmoid(_dot(x1.astype(BF16), wg_ref[...])) * _dot(p_ref[...], wp_ref[...])
    y = alpha * x1 + pt_ref[...].T + ple
    o_ref[...] = _layer_norm(y, g_ref[...], b_ref[...])


def _final(x1, peer_t, p_emb, w_gate, w_ple, g, b, tm, alpha):
    n, dm = x1.shape
    pd = p_emb.shape[1]

    def full(shape):
        return pl.BlockSpec(shape, lambda i: (0,) * len(shape))

    return pl.pallas_call(
        functools.partial(_final_kernel, alpha=alpha),
        grid=(n // tm,),
        in_specs=[pl.BlockSpec((tm, dm), lambda i: (i, 0)),
                  pl.BlockSpec((dm, tm), lambda i: (0, i)),
                  pl.BlockSpec((tm, pd), lambda i: (i, 0)),
                  full((dm, dm)), full((pd, dm)), full((1, dm)), full((1, dm))],
        out_specs=pl.BlockSpec((tm, dm), lambda i: (i, 0)),
        out_shape=jax.ShapeDtypeStruct((n, dm), F32),
        compiler_params=_params(("parallel",)),
        name="ple_ln2",
    )(x1, peer_t, p_emb, w_gate, w_ple, g, b)


def _layer(x, p_emb, pos_rows, tm, weights, attention_fn, state_in, chunk, chunk_eff, alpha):
    n = x.shape[0]
    nb = state_in.shape[0]
    t_seq = n // nb
    pr = _project(x.astype(BF16), weights["w_in"], weights["ikg"], weights["ikb"], pos_rows, tm)

    def pad_chunks(a):
        if t_seq % chunk == 0:
            return a
        a = a.reshape(nb, t_seq, a.shape[-1])
        return jnp.pad(a, ((0, 0), (0, chunk - t_seq), (0, 0))).reshape(nb * chunk, a.shape[-1])

    ret_o, s_out = _retention(pad_chunks(pr["rq"]), pad_chunks(pr["rk"]), pad_chunks(pr["rv"]),
                              pad_chunks(pr["rg"]), weights["gn_g"], weights["gn_b"],
                              state_in, chunk, chunk_eff)
    if t_seq % chunk != 0:
        ret_o = ret_o.reshape(nb, chunk, -1)[:, :t_seq].reshape(n, -1)
    att_o = attention_fn(pr)
    x1, x1t = _mix(x, ret_o, att_o, pr["sgab"], weights["w_ret_o"], weights["w_att_o"], weights["w_out"],
                   weights["ln1_g"], weights["ln1_b"], tm, alpha)
    peer_t = _peer(x1t, weights["peer_wqt"], weights["peer_subkeys"], weights["peer_u"], weights["peer_vt"],
                   min(512, n))
    y = _final(x1, peer_t, p_emb.astype(BF16), weights["w_ple_gate"], weights["w_ple"],
               weights["ln2_g"], weights["ln2_b"], tm, alpha)
    return y, pr["ak"], pr["av"], pr["ik"], s_out


def kernel(x_prompt, x_sample, cache_k, cache_v, cache_kidx, state_ret, page_table, p_prompt, p_sample,
           w_in, idx_k_g, idx_k_b, gn_g, gn_b, w_ret_o, w_att_o, w_out, ln1_g, ln1_b, peer_wq,
           peer_subkeys, peer_u, peer_v, w_ple_gate, w_ple, ln2_g, ln2_b):
    depth = w_in.shape[0]
    assert depth == 1, "single-layer trunk"
    nbp, seq, dm = x_prompt.shape
    nbs, t_new, _ = x_sample.shape
    past_len = page_table.shape[1] * cache_k.shape[2]
    alpha = (2.0 * depth) ** 0.25
    i = 0
    weights = dict(
        w_in=w_in[i].astype(BF16),
        ikg=jnp.concatenate([idx_k_g[i], idx_k_g[i]])[None, :],
        ikb=jnp.concatenate([idx_k_b[i], idx_k_b[i]])[None, :],
        gn_g=gn_g[i][None, :], gn_b=gn_b[i][None, :],
        w_ret_o=w_ret_o[i].astype(BF16), w_att_o=w_att_o[i].astype(BF16), w_out=w_out[i].astype(BF16),
        ln1_g=ln1_g[i][None, :], ln1_b=ln1_b[i][None, :],
        peer_wqt=peer_wq[i].T.astype(BF16), peer_subkeys=peer_subkeys[i].astype(BF16),
        peer_u=peer_u[i].astype(BF16), peer_vt=peer_v[i].T.astype(BF16),
        w_ple_gate=w_ple_gate[i].astype(BF16), w_ple=w_ple[i].astype(BF16),
        ln2_g=ln2_g[i][None, :], ln2_b=ln2_b[i][None, :],
    )

    def attn_p(pr):
        return _attention_prompt(pr["aq"], pr["iq"], pr["iw"], pr["akb"], pr["avb"], pr["ikb"], nbp, seq)

    yp, kp, vp, ikp, rp = _layer(
        x_prompt.reshape(nbp * seq, dm), p_prompt[i].reshape(nbp * seq, -1),
        jnp.arange(seq, dtype=jnp.int32), min(512, seq), weights, attn_p,
        jnp.zeros((nbp,) + state_ret.shape[2:], F32), RET_CHUNK, RET_CHUNK, alpha)

    def attn_s(pr):
        return _attention_sample(pr["aq"], pr["iq"], pr["iw"], pr["ak"], pr["av"], pr["ik"],
                                 cache_k[i], cache_v[i], cache_kidx[i], page_table, nbs, t_new)

    ns = nbs * t_new
    pos_s = jnp.tile(past_len + jnp.arange(t_new, dtype=jnp.int32), nbs)
    ys, ks, vs, iks, rs = _layer(
        x_sample.reshape(ns, dm), p_sample[i].reshape(ns, -1), pos_s, ns, weights, attn_s,
        state_ret[i].astype(F32), BF16_SUBLANES, t_new, alpha)

    return (yp.reshape(nbp, seq, dm), ys.reshape(nbs, t_new, dm),
            kp.reshape(1, nbp, seq, ATT_KV_HEADS, ATT_HD), vp.reshape(1, nbp, seq, ATT_KV_HEADS, ATT_HD),
            ikp.reshape(1, nbp, seq, IDX_HD), rp[None].astype(state_ret.dtype),
            ks.reshape(1, nbs, t_new, ATT_KV_HEADS, ATT_HD), vs.reshape(1, nbs, t_new, ATT_KV_HEADS, ATT_HD),
            iks.reshape(1, nbs, t_new, IDX_HD), rs[None].astype(state_ret.dtype))
```

```python
import functools

import jax
import jax.numpy as jnp
from jax import lax
from jax.experimental import pallas as pl
from jax.experimental.pallas import tpu as pltpu

F32 = jnp.float32
BF16 = jnp.bfloat16

RET_HEADS = 4
RET_DK = 256
RET_DV = 512
RET_CHUNK = 256
ATT_HEADS = 8
ATT_KV_HEADS = 2
ATT_HD = 128
IDX_HEADS = 8
IDX_HD = 64
TOPK_ATTN = 256
ROPE_THETA = 10000.0
PEER_HEADS = 8
PEER_NKEYS = 128
PEER_DKEY = 256
PEER_TOPK = 16
LN_EPS = 1e-5
NEG = -1e30
INF = float("inf")

LANES = 128
BF16_SUBLANES = 16
VMEM_LIMIT = 48 * 1024 * 1024
BISECT_ITERS = 32
SAMPLE_PAGES_PER_STEP = 16


def _dot(a, b):
    return jnp.dot(a, b, preferred_element_type=F32)


def _dot_nt(a, b):
    return lax.dot_general(a, b, (((1,), (1,)), ((), ())), preferred_element_type=F32)


def _dot_tn(a, b):
    return lax.dot_general(a, b, (((0,), (0,)), ((), ())), preferred_element_type=F32)


def _params(sem):
    return pltpu.CompilerParams(dimension_semantics=sem, vmem_limit_bytes=VMEM_LIMIT)


TRANSPOSE_BLOCK = 512


def _transpose_kernel(x_ref, o_ref):
    o_ref[...] = x_ref[...].astype(F32).T.astype(o_ref.dtype)


def _transpose(x, dtype):
    r, c = x.shape
    br, bc = min(TRANSPOSE_BLOCK, r), min(TRANSPOSE_BLOCK, c)
    assert r % br == 0 and c % bc == 0
    return pl.pallas_call(
        _transpose_kernel,
        grid=(r // br, c // bc),
        in_specs=[pl.BlockSpec((br, bc), lambda i, j: (i, j))],
        out_specs=pl.BlockSpec((bc, br), lambda i, j: (j, i)),
        out_shape=jax.ShapeDtypeStruct((c, r), dtype),
        compiler_params=_params(("parallel", "parallel")),
        name="transpose",
    )(x)


def _rope_tables(pos, d, width):
    half = d // 2
    inv = ROPE_THETA ** (-jnp.arange(half, dtype=F32) / half)
    ang = pos.astype(F32)[:, None] * inv[None, :]
    cos, sin = jnp.cos(ang), jnp.sin(ang)
    cosf = jnp.concatenate([cos, cos], axis=-1)
    sinf = jnp.concatenate([-sin, sin], axis=-1)
    reps = width // d
    return jnp.tile(cosf, (1, reps)), jnp.tile(sinf, (1, reps))


def _partner(y, d):
    if d == 2 * LANES:
        return jnp.concatenate([y[:, LANES:], y[:, :LANES]], axis=-1)
    if d == LANES:
        return pltpu.roll(y, LANES // 2, axis=1)
    lane = lax.broadcasted_iota(jnp.int32, y.shape, 1)
    first_half = (lane & (d - 1)) < (d // 2)
    return jnp.where(first_half, pltpu.roll(y, LANES - d // 2, axis=1), pltpu.roll(y, d // 2, axis=1))


def _rope(y, cosf, sinf, d):
    return y * cosf + _partner(y, d) * sinf


def _ret_qk_kernel(x_ref, w_ref, cos_ref, sin_ref, q_ref, k_ref):
    x = x_ref[...]
    cosf, sinf = cos_ref[...], sin_ref[...]
    for h in range(2 * RET_HEADS):
        y = _dot(x, w_ref[:, h * RET_DK:(h + 1) * RET_DK])
        r = _rope(y, cosf, sinf, RET_DK)
        if h < RET_HEADS:
            q_ref[:, h * RET_DK:(h + 1) * RET_DK] = r.astype(BF16)
        else:
            hh = h - RET_HEADS
            k_ref[:, hh * RET_DK:(hh + 1) * RET_DK] = (r * (RET_DK ** -0.5)).astype(BF16)


def _plain_kernel(x_ref, w_ref, o_ref, *, sigmoid):
    y = _dot(x_ref[...], w_ref[...])
    if sigmoid:
        y = jax.nn.sigmoid(y)
    o_ref[...] = y.astype(o_ref.dtype)


def _att_proj_kernel(x_ref, wq_ref, wk_ref, wv_ref, cos_ref, sin_ref,
                     q_ref, k_ref, v_ref, kb_ref, vb_ref):
    x = x_ref[...]
    cosf, sinf = cos_ref[...], sin_ref[...]
    for h in range(ATT_HEADS):
        sl = slice(h * ATT_HD, (h + 1) * ATT_HD)
        q_ref[:, sl] = _rope(_dot(x, wq_ref[:, sl]), cosf, sinf, ATT_HD).astype(BF16)
    tm = x.shape[0]
    v = _dot(x, wv_ref[...])
    vb_ref[...] = v.astype(BF16)
    for h in range(ATT_KV_HEADS):
        sl = slice(h * ATT_HD, (h + 1) * ATT_HD)
        k = _rope(_dot(x, wk_ref[:, sl]), cosf, sinf, ATT_HD)
        kb_ref[:, sl] = k.astype(BF16)
        head_rows = pl.ds(h, tm, stride=ATT_KV_HEADS)
        k_ref[head_rows, :] = k
        v_ref[head_rows, :] = v[:, sl]


def _idx_proj_kernel(x_ref, wq_ref, wk_ref, ww_ref, cos_ref, sin_ref, g_ref, b_ref,
                     q_ref, k_ref, kb_ref, w_ref):
    x = x_ref[...]
    cosf, sinf = cos_ref[...], sin_ref[...]
    for c in range(IDX_HEADS * IDX_HD // LANES):
        sl = slice(c * LANES, (c + 1) * LANES)
        q_ref[:, sl] = _rope(_dot(x, wq_ref[:, sl]), cosf, sinf, IDX_HD).astype(BF16)
    y = _dot(x, wk_ref[...])
    mu = jnp.mean(y, axis=-1, keepdims=True)
    var = jnp.mean(jnp.square(y - mu), axis=-1, keepdims=True)
    yn = (y - mu) * lax.rsqrt(var + LN_EPS) * g_ref[...] + b_ref[...]
    k = yn * cosf + pltpu.roll(yn, IDX_HD // 2, axis=1) * sinf
    k_ref[...] = k[:, :IDX_HD]
    kb_ref[...] = k[:, :IDX_HD].astype(BF16)
    w = _dot(x, ww_ref[...])
    w_ref[...] = w[:, :IDX_HEADS] * (IDX_HEADS ** -0.5)


def _project(xb, w, ikg, ikb, pos_rows, tm):
    n, dm = xb.shape
    nt = n // tm
    rows = pos_rows.shape[0]
    tab_blocks = rows // tm
    hq = RET_HEADS * RET_DK
    hv = RET_HEADS * RET_DV
    o = 0
    w_rqk = w[:, o:o + 2 * hq]; o += 2 * hq
    w_rv = w[:, o:o + hv]; o += hv
    w_rg = w[:, o:o + hv]; o += hv
    w_aq = w[:, o:o + ATT_HEADS * ATT_HD]; o += ATT_HEADS * ATT_HD
    w_ak = w[:, o:o + ATT_KV_HEADS * ATT_HD]; o += ATT_KV_HEADS * ATT_HD
    w_av = w[:, o:o + ATT_KV_HEADS * ATT_HD]; o += ATT_KV_HEADS * ATT_HD
    w_iq = w[:, o:o + IDX_HEADS * IDX_HD]; o += IDX_HEADS * IDX_HD
    w_ik = w[:, o:o + IDX_HD]; o += IDX_HD
    w_iw = w[:, o:o + IDX_HEADS]; o += IDX_HEADS
    w_gab = w[:, o:]

    x_spec = pl.BlockSpec((tm, dm), lambda i: (i, 0))

    def tab_spec(width):
        return pl.BlockSpec((tm, width), lambda i: (i % tab_blocks, 0))

    def full(shape):
        return pl.BlockSpec(shape, lambda i: (0,) * len(shape))

    def row_spec(width):
        return pl.BlockSpec((tm, width), lambda i: (i, 0))

    cos256, sin256 = _rope_tables(pos_rows, RET_DK, RET_DK)
    rq, rk = pl.pallas_call(
        _ret_qk_kernel,
        grid=(nt,),
        in_specs=[x_spec, full((dm, 2 * hq)), tab_spec(RET_DK), tab_spec(RET_DK)],
        out_specs=[row_spec(hq), row_spec(hq)],
        out_shape=[jax.ShapeDtypeStruct((n, hq), BF16)] * 2,
        compiler_params=_params(("parallel",)),
        name="proj_ret_qk",
    )(xb, w_rqk, cos256, sin256)

    def plain(wc, dtype, sigmoid, name):
        cols = wc.shape[1]
        tn = min(1024, cols)
        return pl.pallas_call(
            functools.partial(_plain_kernel, sigmoid=sigmoid),
            grid=(cols // tn, nt),
            in_specs=[pl.BlockSpec((tm, dm), lambda j, i: (i, 0)),
                      pl.BlockSpec((dm, tn), lambda j, i: (0, j))],
            out_specs=pl.BlockSpec((tm, tn), lambda j, i: (i, j)),
            out_shape=jax.ShapeDtypeStruct((n, cols), dtype),
            compiler_params=_params(("parallel", "parallel")),
            name=name,
        )(xb, wc)

    rv = plain(w_rv, BF16, False, "proj_ret_v")
    rg = plain(w_rg, F32, False, "proj_ret_gate")
    sgab = plain(w_gab, F32, True, "proj_branch_gates")

    cos128, sin128 = _rope_tables(pos_rows, ATT_HD, ATT_HD)
    kvw = ATT_KV_HEADS * ATT_HD
    aq, ak, av, akb, avb = pl.pallas_call(
        _att_proj_kernel,
        grid=(nt,),
        in_specs=[x_spec, full((dm, ATT_HEADS * ATT_HD)), full((dm, kvw)), full((dm, kvw)),
                  tab_spec(ATT_HD), tab_spec(ATT_HD)],
        out_specs=[row_spec(ATT_HEADS * ATT_HD),
                   pl.BlockSpec((tm * ATT_KV_HEADS, ATT_HD), lambda i: (i, 0)),
                   pl.BlockSpec((tm * ATT_KV_HEADS, ATT_HD), lambda i: (i, 0)),
                   row_spec(kvw), row_spec(kvw)],
        out_shape=[jax.ShapeDtypeStruct((n, ATT_HEADS * ATT_HD), BF16),
                   jax.ShapeDtypeStruct((n * ATT_KV_HEADS, ATT_HD), F32),
                   jax.ShapeDtypeStruct((n * ATT_KV_HEADS, ATT_HD), F32),
                   jax.ShapeDtypeStruct((n, kvw), BF16), jax.ShapeDtypeStruct((n, kvw), BF16)],
        compiler_params=_params(("parallel",)),
        name="proj_att",
    )(xb, w_aq, w_ak, w_av, cos128, sin128)

    cos64, sin64 = _rope_tables(pos_rows, IDX_HD, LANES)
    w_ik2 = jnp.concatenate([w_ik, w_ik], axis=1)
    w_iw_pad = jnp.pad(w_iw, ((0, 0), (0, LANES - IDX_HEADS)))
    iq, ik, ikb, iw = pl.pallas_call(
        _idx_proj_kernel,
        grid=(nt,),
        in_specs=[x_spec, full((dm, IDX_HEADS * IDX_HD)), full((dm, LANES)), full((dm, LANES)),
                  tab_spec(LANES), tab_spec(LANES), full((1, LANES)), full((1, LANES))],
        out_specs=[row_spec(IDX_HEADS * IDX_HD), row_spec(IDX_HD), row_spec(IDX_HD), row_spec(IDX_HEADS)],
        out_shape=[jax.ShapeDtypeStruct((n, IDX_HEADS * IDX_HD), BF16),
                   jax.ShapeDtypeStruct((n, IDX_HD), F32), jax.ShapeDtypeStruct((n, IDX_HD), BF16),
                   jax.ShapeDtypeStruct((n, IDX_HEADS), F32)],
        compiler_params=_params(("parallel",)),
        name="proj_idx",
    )(xb, w_iq, w_ik2, w_iw_pad, cos64, sin64, ikg, ikb)
    return dict(rq=rq, rk=rk, rv=rv, rg=rg, sgab=sgab, aq=aq, ak=ak, av=av, akb=akb, avb=avb,
                iq=iq, ik=ik, ikb=ikb, iw=iw)


def _retention_kernel(sdec_ref, q_ref, k_ref, v_ref, gate_ref, gg_ref, gb_ref,
                      d_ref, cdec_ref, kdec_ref, si_ref, o_ref, so_ref):
    h = pl.program_id(1)
    c = pl.program_id(2)

    @pl.when(c == 0)
    def _():
        so_ref[0, 0] = si_ref[0, 0]

    q = q_ref[...]
    k = k_ref[...]
    v = v_ref[...]
    state = so_ref[0, 0]
    s = _dot_nt(q, k)
    inner = _dot((s * d_ref[0]).astype(BF16), v)
    cross = _dot(q, state.astype(BF16)) * cdec_ref[0]
    o = inner + cross
    kd = (k.astype(F32) * kdec_ref[0]).astype(BF16)
    so_ref[0, 0] = state * sdec_ref[h] + _dot_tn(kd, v)
    mu = jnp.mean(o, axis=-1, keepdims=True)
    var = jnp.mean(jnp.square(o - mu), axis=-1, keepdims=True)
    on = (o - mu) * lax.rsqrt(var + LN_EPS) * gg_ref[...] + gb_ref[...]
    o_ref[...] = (jax.nn.silu(gate_ref[...]) * on).astype(o_ref.dtype)


def _retention(rq, rk, rv, rg, gn_g, gn_b, state_in, chunk, chunk_eff):
    n = rq.shape[0]
    nb = state_in.shape[0]
    nc = n // (nb * chunk)
    lg = jnp.log1p(-jnp.exp2(-5.0 - jnp.arange(RET_HEADS, dtype=F32)))[:, None]
    i = jnp.arange(chunk, dtype=F32)
    diff = i[:, None] - i[None, :]
    decay = jnp.where(diff >= 0, jnp.exp(lg[:, :, None] * jnp.maximum(diff, 0.0)), 0.0)
    cdec = jnp.exp(lg * (i + 1.0))[:, :, None]
    kdec = jnp.where(i < chunk_eff, jnp.exp(lg * jnp.maximum(chunk_eff - 1.0 - i, 0.0)), 0.0)[:, :, None]
    sdec = jnp.exp(lg[:, 0] * chunk_eff)

    o, s_out = pl.pallas_call(
        _retention_kernel,
        grid=(nb, RET_HEADS, nc),
        in_specs=[
            pl.BlockSpec(memory_space=pltpu.SMEM),
            pl.BlockSpec((chunk, RET_DK), lambda b, h, c: (b * nc + c, h)),
            pl.BlockSpec((chunk, RET_DK), lambda b, h, c: (b * nc + c, h)),
            pl.BlockSpec((chunk, RET_DV), lambda b, h, c: (b * nc + c, h)),
            pl.BlockSpec((chunk, RET_DV), lambda b, h, c: (b * nc + c, h)),
            pl.BlockSpec((1, RET_DV), lambda b, h, c: (0, h)),
            pl.BlockSpec((1, RET_DV), lambda b, h, c: (0, h)),
            pl.BlockSpec((1, chunk, chunk), lambda b, h, c: (h, 0, 0)),
            pl.BlockSpec((1, chunk, 1), lambda b, h, c: (h, 0, 0)),
            pl.BlockSpec((1, chunk, 1), lambda b, h, c: (h, 0, 0)),
            pl.BlockSpec((1, 1, RET_DK, RET_DV), lambda b, h, c: (b, h, 0, 0)),
        ],
        out_specs=[
            pl.BlockSpec((chunk, RET_DV), lambda b, h, c: (b * nc + c, h)),
            pl.BlockSpec((1, 1, RET_DK, RET_DV), lambda b, h, c: (b, h, 0, 0)),
        ],
        out_shape=[jax.ShapeDtypeStruct((n, RET_HEADS * RET_DV), BF16),
                   jax.ShapeDtypeStruct(state_in.shape, F32)],
        compiler_params=_params(("parallel", "parallel", "arbitrary")),
        name="retention",
    )(sdec, rq, rk, rv, rg, gn_g, gn_b, decay, cdec, kdec, state_in)
    return o, s_out


M_INIT = -1e29


def _lanes(col):
    return jnp.broadcast_to(col, (col.shape[0], LANES))


def _kth_largest(reduce_tiles, mn, mx, kq):
    kq_b = _lanes(kq)

    def bisect_body(_, carry):
        lo, hi = carry
        mid = 0.5 * (lo + hi)
        cnt = reduce_tiles(lambda a, t: a + jnp.where(t >= mid, 1.0, 0.0), 0.0)
        ge = _lanes(jnp.sum(cnt, axis=-1, keepdims=True)) >= kq_b
        return jnp.where(ge, mid, lo), jnp.where(ge, hi, mid)

    lo, _ = lax.fori_loop(0, BISECT_ITERS, bisect_body, (_lanes(mn), _lanes(mx + 1.0)))
    tau = jnp.min(reduce_tiles(lambda a, t: jnp.minimum(a, jnp.where(t >= lo, t, INF)), INF),
                  axis=-1, keepdims=True)
    tau_b = _lanes(tau)
    cgt = jnp.sum(reduce_tiles(lambda a, t: a + jnp.where(t > tau_b, 1.0, 0.0), 0.0),
                  axis=-1, keepdims=True)
    return tau, kq - cgt


def _softmax_step(s, m_ref, l_ref, acc_ref, idx, v):
    m_old = m_ref[idx]
    m_new = jnp.maximum(m_old, jnp.max(s, axis=-1, keepdims=True))
    p = jnp.exp(s - m_new)
    alpha = jnp.exp(m_old - m_new)
    l_ref[idx] = alpha * l_ref[idx] + jnp.sum(p, axis=-1, keepdims=True)
    acc_ref[idx] = alpha * acc_ref[idx] + _dot(p.astype(BF16), v)
    m_ref[idx] = m_new


def _fold_rows(x):
    return jnp.sum(x.reshape(x.shape[0] // 8, 8, x.shape[1]), axis=0)


def _attn_prompt_kernel(aqt_ref, iqt_ref, iwt_ref, k_ref, vt_ref, kidx_ref, tril_ref, o_ref,
                        sc_ref, m_ref, l_ref, acc_ref, *, tq, topk):
    i = pl.program_id(1)
    nkv = i + 1
    qpos = i * tq + lax.broadcasted_iota(jnp.int32, (1, tq), 1)
    kq = jnp.minimum(topk, qpos + 1).astype(F32)
    group = ATT_HEADS // ATT_KV_HEADS

    def score_body(j, carry):
        mn, mx = carry
        kx = kidx_ref[pl.ds(pl.multiple_of(j * tq, tq), tq), :]
        s = jnp.zeros((tq, tq), F32)
        for h in range(IDX_HEADS):
            d = _dot(kx, iqt_ref[h * IDX_HD:(h + 1) * IDX_HD, :]) * (IDX_HD ** -0.5)
            s = s + iwt_ref[h:h + 1, :] * jnp.maximum(d, 0.0)
        kpos = j * tq + lax.broadcasted_iota(jnp.int32, (tq, tq), 0)
        vis = kpos <= qpos
        sc_ref[j] = jnp.where(vis, s, -INF)
        mn = jnp.minimum(mn, jnp.min(jnp.where(vis, s, INF), axis=0, keepdims=True))
        mx = jnp.maximum(mx, jnp.max(jnp.where(vis, s, -INF), axis=0, keepdims=True))
        return mn, mx

    mn, mx = lax.fori_loop(0, nkv, score_body,
                           (jnp.full((1, tq), INF, F32), jnp.full((1, tq), -INF, F32)))

    def count(pred):
        def body(j, acc):
            return acc + _fold_rows(jnp.where(pred(sc_ref[j]), 1.0, 0.0))
        return jnp.sum(lax.fori_loop(0, nkv, body, jnp.zeros((8, tq), F32)), axis=0, keepdims=True)

    def bisect_body(_, carry):
        lo, hi = carry
        mid = 0.5 * (lo + hi)
        ge = count(lambda t: t >= mid) >= kq
        return jnp.where(ge, mid, lo), jnp.where(ge, hi, mid)

    lo, _ = lax.fori_loop(0, BISECT_ITERS, bisect_body, (mn, mx + 1.0))

    def snap_body(j, acc):
        t = sc_ref[j]
        return jnp.minimum(acc, jnp.min(jnp.where(t >= lo, t, INF), axis=0, keepdims=True))

    tau = lax.fori_loop(0, nkv, snap_body, jnp.full((1, tq), INF, F32))
    need = kq - count(lambda t: t > tau)

    tril = tril_ref[...]

    def mask_body(j, run):
        t = sc_ref[j]
        eqf = jnp.where(t == tau, 1.0, 0.0)
        prefix = _dot(tril, eqf.astype(BF16))
        sel = (t > tau) | ((t == tau) & ((run + prefix) <= need))
        sc_ref[j] = jnp.where(sel, 0.0, NEG)
        return run + jnp.sum(eqf, axis=0, keepdims=True)

    lax.fori_loop(0, nkv, mask_body, jnp.zeros((1, tq), F32))

    m_ref[...] = jnp.full(m_ref.shape, M_INIT, F32)
    l_ref[...] = jnp.zeros(l_ref.shape, F32)
    acc_ref[...] = jnp.zeros(acc_ref.shape, F32)

    def att_body(j, carry):
        rows = pl.ds(pl.multiple_of(j * tq, tq), tq)
        for h in range(ATT_HEADS):
            g = h // group
            kg = k_ref[rows, g * ATT_HD:(g + 1) * ATT_HD]
            vg = vt_ref[0, j, g * ATT_HD:(g + 1) * ATT_HD, :]
            for c in range(tq // LANES):
                ls = slice(c * LANES, (c + 1) * LANES)
                s = _dot(kg, aqt_ref[h * ATT_HD:(h + 1) * ATT_HD, ls]) * (ATT_HD ** -0.5) + sc_ref[j, :, ls]
                m_old = m_ref[h, :, ls]
                m_new = jnp.maximum(m_old, jnp.max(s, axis=0, keepdims=True))
                p = jnp.exp(s - m_new)
                alpha = jnp.exp(m_old - m_new)
                l_ref[h, :, ls] = alpha * l_ref[h, :, ls] + jnp.sum(p, axis=0, keepdims=True)
                acc_ref[h, :, ls] = alpha * acc_ref[h, :, ls] + _dot(vg, p.astype(BF16))
                m_ref[h, :, ls] = m_new
        return carry

    lax.fori_loop(0, nkv, att_body, 0)
    for h in range(ATT_HEADS):
        o_ref[h * ATT_HD:(h + 1) * ATT_HD, :] = (acc_ref[h] / l_ref[h]).astype(o_ref.dtype)


def _attention_prompt(aq, iq, iw, akb, avb, ikb, nb, seq):
    tq = min(256, seq)
    assert seq % tq == 0 and tq % LANES == 0
    nq = seq // tq
    topk = min(TOPK_ATTN, seq // 4)
    n = aq.shape[0]
    tril = jnp.tril(jnp.ones((tq, tq), F32)).astype(BF16)
    kvw = ATT_KV_HEADS * ATT_HD
    vt = avb.reshape(nb, nq, tq, kvw).transpose(0, 1, 3, 2)
    ot = pl.pallas_call(
        functools.partial(_attn_prompt_kernel, tq=tq, topk=topk),
        grid=(nb, nq),
        in_specs=[
            pl.BlockSpec((ATT_HEADS * ATT_HD, tq), lambda b, i: (0, b * nq + i)),
            pl.BlockSpec((IDX_HEADS * IDX_HD, tq), lambda b, i: (0, b * nq + i)),
            pl.BlockSpec((IDX_HEADS, tq), lambda b, i: (0, b * nq + i)),
            pl.BlockSpec((seq, kvw), lambda b, i: (b, 0)),
            pl.BlockSpec((1, nq, kvw, tq), lambda b, i: (b, 0, 0, 0)),
            pl.BlockSpec((seq, IDX_HD), lambda b, i: (b, 0)),
            pl.BlockSpec((tq, tq), lambda b, i: (0, 0)),
        ],
        out_specs=pl.BlockSpec((ATT_HEADS * ATT_HD, tq), lambda b, i: (0, b * nq + i)),
        out_shape=jax.ShapeDtypeStruct((ATT_HEADS * ATT_HD, n), BF16),
        scratch_shapes=[
            pltpu.VMEM((nq, tq, tq), F32),
            pltpu.VMEM((ATT_HEADS, 1, tq), F32),
            pltpu.VMEM((ATT_HEADS, 1, tq), F32),
            pltpu.VMEM((ATT_HEADS, ATT_HD, tq), F32),
        ],
        compiler_params=_params(("parallel", "arbitrary")),
        name="attention_prompt",
    )(_transpose(aq, BF16), _transpose(iq, BF16), iw.T, akb, vt, ikb, tril)
    return _transpose(ot, BF16)


def _page_scores(iq_ref, iw_ref, kx):
    tp = iq_ref.shape[1] // IDX_HEADS
    d = _dot_nt(iq_ref[0], kx) * (IDX_HD ** -0.5)
    d = iw_ref[0] * jnp.maximum(d, 0.0)
    s = d[:tp]
    for h in range(1, IDX_HEADS):
        s = s + d[h * tp:(h + 1) * tp]
    return s


def _sample_scores_kernel(pt_ref, iq_ref, iw_ref, *refs, npp, n_new, topk):
    page_refs = refs[:npp]
    knew_ref = refs[npp]
    sc_ref, tau_ref, need_ref = refs[npp + 1:]
    s = pl.program_id(1)
    npages = sc_ref.shape[1] - 1
    tp, page = sc_ref.shape[2], sc_ref.shape[3]

    kx = jnp.concatenate([r[0] for r in page_refs], axis=0).astype(BF16)
    sc = _page_scores(iq_ref, iw_ref, kx)
    for p in range(npp):
        sc_ref[0, s * npp + p] = sc[:, p * page:(p + 1) * page]

    @pl.when(s == pl.num_programs(1) - 1)
    def _():
        snew = _page_scores(iq_ref, iw_ref, knew_ref[0].astype(BF16))
        t = jnp.minimum(lax.broadcasted_iota(jnp.int32, (tp, page), 0), n_new - 1)
        j = lax.broadcasted_iota(jnp.int32, (tp, page), 1)
        sc_ref[0, npages] = jnp.where(j <= t, snew, -INF)

        def reduce_tiles(fn, init):
            def body(c, acc):
                for p in range(npp):
                    acc = fn(acc, sc_ref[0, c * npp + p])
                return acc
            acc = lax.fori_loop(0, npages // npp, body, jnp.full((tp, page), init, F32))
            return fn(acc, sc_ref[0, npages])

        mn = jnp.min(reduce_tiles(lambda a, x: jnp.minimum(a, jnp.where(x == -INF, INF, x)), INF),
                     axis=-1, keepdims=True)
        mx = jnp.max(reduce_tiles(jnp.maximum, -INF), axis=-1, keepdims=True)
        tau, need = _kth_largest(reduce_tiles, mn, mx, jnp.full((tp, 1), float(topk), F32))
        tau_ref[0] = _lanes(tau)
        need_ref[0] = _lanes(need)


def _sample_attend_kernel(pt_ref, q_ref, sc_ref, tau_ref, need_ref, tri_ref, *refs, npp):
    k_refs = refs[:npp]
    v_refs = refs[npp:2 * npp]
    knew_ref, vnew_ref, o_ref, m_ref, l_ref, acc_ref, run_ref = refs[2 * npp:]
    s = pl.program_id(1)
    npages = sc_ref.shape[1] - 1
    page = sc_ref.shape[3]
    group = ATT_HEADS // ATT_KV_HEADS
    grows = q_ref.shape[1] // ATT_KV_HEADS

    @pl.when(s == 0)
    def _():
        m_ref[...] = jnp.full(m_ref.shape, M_INIT, F32)
        l_ref[...] = jnp.zeros(l_ref.shape, F32)
        acc_ref[...] = jnp.zeros(acc_ref.shape, F32)
        run_ref[...] = jnp.zeros(run_ref.shape, F32)

    tau = tau_ref[0][:, :1]
    need = need_ref[0][:, :1]
    tri = tri_ref[...]

    def attend(pages, krefs, vrefs):
        tp = sc_ref.shape[2]
        scs = [sc_ref[0, pg] for pg in pages]
        eqs = [jnp.where(t == tau, 1.0, 0.0) for t in scs]
        prefix = _dot(jnp.concatenate(eqs, axis=0).astype(BF16), tri)
        run = run_ref[...]
        madds = []
        for k, t in enumerate(scs):
            sel = (t > tau) | ((t == tau) & ((run + prefix[k * tp:(k + 1) * tp]) <= need))
            madds.append(jnp.where(sel, 0.0, NEG))
            run = run + jnp.sum(eqs[k], axis=-1, keepdims=True)
        run_ref[...] = run
        madd = jnp.concatenate(madds, axis=1)
        maddg = jnp.concatenate([madd] * group, axis=0)
        for g in range(ATT_KV_HEADS):
            head_rows = pl.ds(g, page, stride=ATT_KV_HEADS)
            kg = jnp.concatenate([r[0, head_rows, :] for r in krefs], axis=0).astype(BF16)
            vg = jnp.concatenate([r[0, head_rows, :] for r in vrefs], axis=0).astype(BF16)
            sg = _dot_nt(q_ref[0, g * grows:(g + 1) * grows, :], kg) * (ATT_HD ** -0.5) + maddg
            _softmax_step(sg, m_ref, l_ref, acc_ref, g, vg)

    attend([s * npp + p for p in range(npp)], k_refs, v_refs)

    @pl.when(s == pl.num_programs(1) - 1)
    def _():
        attend([npages], [knew_ref], [vnew_ref])
        for g in range(ATT_KV_HEADS):
            o_ref[0, g * grows:(g + 1) * grows, :] = acc_ref[g] / l_ref[g]


def _attention_sample(aq, iq, iw, ak, av, ik, cache_k, cache_v, cache_kidx, page_table, nb, t_new):
    n_phys, page = cache_kidx.shape[0], cache_kidx.shape[1]
    npages = page_table.shape[1]
    npp = SAMPLE_PAGES_PER_STEP if npages % SAMPLE_PAGES_PER_STEP == 0 else 1
    nsteps = npages // npp
    tp = BF16_SUBLANES
    topk = min(TOPK_ATTN, (npages * page + t_new) // 4)
    prow = page * ATT_KV_HEADS

    def pad_rows(a, per_seq, rows):
        a = a.reshape(nb, per_seq, a.shape[-1])
        return jnp.pad(a, ((0, 0), (0, rows - per_seq), (0, 0)))

    iq_p = pad_rows(iq, t_new, tp).reshape(nb, tp, IDX_HEADS, IDX_HD).transpose(0, 2, 1, 3)
    iq_p = iq_p.reshape(nb, IDX_HEADS * tp, IDX_HD)
    iw_p = pad_rows(iw, t_new, tp).transpose(0, 2, 1).reshape(nb, IDX_HEADS * tp, 1)
    q_p = pad_rows(aq, t_new, tp).reshape(nb, tp, ATT_HEADS, ATT_HD).transpose(0, 2, 1, 3)
    q_p = q_p.reshape(nb, ATT_HEADS * tp, ATT_HD)
    knew = pad_rows(ak, t_new * ATT_KV_HEADS, prow)
    vnew = pad_rows(av, t_new * ATT_KV_HEADS, prow)
    kinew = pad_rows(ik, t_new, page)
    ck = cache_k.reshape(n_phys, prow, ATT_HD)
    cv = cache_v.reshape(n_phys, prow, ATT_HD)
    tri = jnp.triu(jnp.ones((page, page), F32)).astype(BF16)

    def page_spec(rows, width, p):
        return pl.BlockSpec((1, rows, width), lambda b, s, pt, p=p: (pt[b, s * npp + p], 0, 0))

    def seq_spec(shape):
        return pl.BlockSpec((1,) + shape, lambda b, s, pt: (b,) + (0,) * len(shape))

    sc, tau, need = pl.pallas_call(
        functools.partial(_sample_scores_kernel, npp=npp, n_new=t_new, topk=topk),
        grid_spec=pltpu.PrefetchScalarGridSpec(
            num_scalar_prefetch=1,
            grid=(nb, nsteps),
            in_specs=[seq_spec((IDX_HEADS * tp, IDX_HD)), seq_spec((IDX_HEADS * tp, 1))]
                     + [page_spec(page, IDX_HD, p) for p in range(npp)] + [seq_spec((page, IDX_HD))],
            out_specs=[seq_spec((npages + 1, tp, page)), seq_spec((tp, page)), seq_spec((tp, page))],
        ),
        out_shape=[jax.ShapeDtypeStruct((nb, npages + 1, tp, page), F32),
                   jax.ShapeDtypeStruct((nb, tp, page), F32),
                   jax.ShapeDtypeStruct((nb, tp, page), F32)],
        compiler_params=_params(("parallel", "arbitrary")),
        name="sample_scores",
    )(page_table, iq_p, iw_p, *([cache_kidx] * npp), kinew)

    grows = (ATT_HEADS // ATT_KV_HEADS) * tp
    o = pl.pallas_call(
        functools.partial(_sample_attend_kernel, npp=npp),
        grid_spec=pltpu.PrefetchScalarGridSpec(
            num_scalar_prefetch=1,
            grid=(nb, nsteps),
            in_specs=[seq_spec((ATT_HEADS * tp, ATT_HD)), seq_spec((npages + 1, tp, page)),
                      seq_spec((tp, page)), seq_spec((tp, page)),
                      pl.BlockSpec((page, page), lambda b, s, pt: (0, 0))]
                     + [page_spec(prow, ATT_HD, p) for p in range(npp)]
                     + [page_spec(prow, ATT_HD, p) for p in range(npp)]
                     + [seq_spec((prow, ATT_HD)), seq_spec((prow, ATT_HD))],
            out_specs=seq_spec((ATT_HEADS * tp, ATT_HD)),
            scratch_shapes=[
                pltpu.VMEM((ATT_KV_HEADS, grows, 1), F32),
                pltpu.VMEM((ATT_KV_HEADS, grows, 1), F32),
                pltpu.VMEM((ATT_KV_HEADS, grows, ATT_HD), F32),
                pltpu.VMEM((tp, 1), F32),
            ],
        ),
        out_shape=jax.ShapeDtypeStruct((nb, ATT_HEADS * tp, ATT_HD), F32),
        compiler_params=_params(("parallel", "arbitrary")),
        name="sample_attend",
    )(page_table, q_p, sc, tau, need, tri, *([ck] * npp), *([cv] * npp), knew, vnew)
    o = o.reshape(nb, ATT_HEADS, tp, ATT_HD)[:, :, :t_new].transpose(0, 2, 1, 3)
    return o.reshape(nb * t_new, ATT_HEADS * ATT_HD).astype(BF16)


def _layer_norm(y, g, b):
    mu = jnp.mean(y, axis=-1, keepdims=True)
    var = jnp.mean(jnp.square(y - mu), axis=-1, keepdims=True)
    return (y - mu) * lax.rsqrt(var + LN_EPS) * g + b


def _mix_kernel(x_ref, ret_ref, att_ref, sga_ref, sgb_ref, wr_ref, wa_ref, wo_ref, g_ref, b_ref,
                o_ref, ot_ref, *, alpha):
    branch = sga_ref[...] * _dot(ret_ref[...], wr_ref[...]) + sgb_ref[...] * _dot(att_ref[...], wa_ref[...])
    y = alpha * x_ref[...] + _dot(branch.astype(BF16), wo_ref[...])
    x1 = _layer_norm(y, g_ref[...], b_ref[...])
    o_ref[...] = x1
    ot_ref[...] = x1.T.astype(BF16)


def _mix(x, ret_o, att_o, sgab, w_ret_o, w_att_o, w_out, g, b, tm, alpha):
    n, dm = x.shape
    hv = RET_HEADS * RET_DV
    ha = ATT_HEADS * ATT_HD

    def full(shape):
        return pl.BlockSpec(shape, lambda i: (0,) * len(shape))

    return pl.pallas_call(
        functools.partial(_mix_kernel, alpha=alpha),
        grid=(n // tm,),
        in_specs=[pl.BlockSpec((tm, dm), lambda i: (i, 0)),
                  pl.BlockSpec((tm, hv), lambda i: (i, 0)),
                  pl.BlockSpec((tm, ha), lambda i: (i, 0)),
                  pl.BlockSpec((tm, dm), lambda i: (i, 0)),
                  pl.BlockSpec((tm, dm), lambda i: (i, 1)),
                  full((hv, dm)), full((ha, dm)), full((dm, dm)), full((1, dm)), full((1, dm))],
        out_specs=[pl.BlockSpec((tm, dm), lambda i: (i, 0)),
                   pl.BlockSpec((dm, tm), lambda i: (0, i))],
        out_shape=[jax.ShapeDtypeStruct((n, dm), F32), jax.ShapeDtypeStruct((dm, n), BF16)],
        compiler_params=_params(("parallel",)),
        name="branch_mix_ln1",
    )(x, ret_o, att_o, sgab, sgab, w_ret_o, w_att_o, w_out, g, b)


PEER_EXPERT_ROWS = 8
PEER_ROW_TILE = 32


def _top_values(cur, k):
    rows = []
    for _ in range(k):
        m = jnp.max(cur, axis=0, keepdims=True)
        rows.append(m)
        cur = jnp.where(cur == m, -INF, cur)
    return rows


def _peer_route_kernel(xt_ref, wqt_ref, sk_ref, c_ref, a_ref, s2_ref, e2_ref):
    xt = xt_ref[...]
    half = PEER_DKEY // 2
    k1 = PEER_TOPK + 1
    pairs = [(a, b) for a in range(k1) for b in range(k1 // (a + 1))]
    for h in range(PEER_HEADS):
        qt = _dot(wqt_ref[h * PEER_DKEY:(h + 1) * PEER_DKEY, :], xt).astype(BF16)
        s1_ref = c_ref.at[h]
        s1_ref[...] = _dot(sk_ref[h, 0], qt[:half])
        s2_ref[h] = _dot(sk_ref[h, 1], qt[half:])
        for lc in range(xt.shape[1] // LANES):
            ls = slice(lc * LANES, (lc + 1) * LANES)
            s1 = s1_ref[:, ls]
            s2 = s2_ref[h, :, ls]
            v1 = _top_values(s1, k1)
            v2 = _top_values(s2, k1)
            rows = [v1[a] + v2[b] for a, b in pairs]
            rows += [jnp.full_like(rows[0], -INF)] * ((-len(rows)) % 8)
            cand = jnp.concatenate(rows, axis=0)
            best = _top_values(cand, k1)
            theta = 0.5 * (best[PEER_TOPK - 1] + best[PEER_TOPK])
            z = jnp.sum(jnp.where(cand > theta, jnp.exp(cand - best[0]), 0.0), axis=0, keepdims=True)
            c_ref[h, :, ls] = theta - s1
            a_ref[h, :, ls] = jnp.exp(s1 - v1[0]) / z
            e2_ref[h, :, ls] = jnp.exp(s2 - v2[0])


def _peer_dense_kernel(xt_ref, c_ref, a_ref, s2_ref, e2_ref, anchor_ref, u_ref, vt_ref, o_ref,
                       act_ref, g_ref, gm_ref):
    e = pl.program_id(1)
    nk = PEER_NKEYS
    nsub = PEER_EXPERT_ROWS
    tt = xt_ref.shape[1]

    @pl.when(e == 0)
    def _():
        o_ref[...] = jnp.zeros(o_ref.shape, F32)

    def activations(rows, ls):
        act = _dot(u_ref[rows, :], xt_ref[:, ls])
        act_ref[rows, ls] = 0.5 * act * (1.0 + lax.erf(act * (2.0 ** -0.5)))

    def weights_tile(lc, rt):
        ls = slice(lc * LANES, (lc + 1) * LANES)
        rs = slice(rt * PEER_ROW_TILE, (rt + 1) * PEER_ROW_TILE)
        ers = [slice(c * nk + rt * PEER_ROW_TILE, c * nk + (rt + 1) * PEER_ROW_TILE) for c in range(nsub)]
        for h in range(PEER_HEADS):
            s2 = s2_ref[h, rs, ls] + anchor_ref[0:1, ls]
            e2 = e2_ref[h, rs, ls] * anchor_ref[1:2, ls]
            for c in range(nsub):
                chosen = s2 > c_ref[h, c:c + 1, ls]
                w = e2 * a_ref[h, c:c + 1, ls]
                if h == 0:
                    g_ref[ers[c], ls] = jnp.where(chosen, w, 0.0)
                else:
                    pltpu.store(g_ref.at[ers[c], ls], g_ref[ers[c], ls] + w, mask=chosen)
        for c in range(nsub):
            gm_ref[ers[c], ls] = (g_ref[ers[c], ls] * act_ref[ers[c], ls]).astype(BF16)

    def outputs(rows, ls):
        o_ref[rows, ls] += _dot(vt_ref[rows, :], gm_ref[:, ls])

    nrt = nk // PEER_ROW_TILE
    nlc = tt // LANES
    if nlc % 2:
        activations(slice(None), slice(None))
        for lc in range(nlc):
            for rt in range(nrt):
                weights_tile(lc, rt)
        outputs(slice(None), slice(None))
        return

    halves = [slice(0, tt // 2), slice(tt // 2, tt)]
    npiece = 4
    arows = [slice(q * (nsub * nk // npiece), (q + 1) * (nsub * nk // npiece)) for q in range(npiece)]
    orows = [slice(q * (o_ref.shape[0] // npiece), (q + 1) * (o_ref.shape[0] // npiece)) for q in range(npiece)]
    tiles = [[(lc, rt) for lc in range(hf * nlc // 2, (hf + 1) * nlc // 2) for rt in range(nrt)] for hf in range(2)]
    per = len(tiles[0]) // npiece
    for q in range(npiece):
        activations(arows[q], halves[0])
    for q in range(npiece):
        activations(arows[q], halves[1])
        for lc, rt in tiles[0][q * per:(q + 1) * per]:
            weights_tile(lc, rt)
    for q in range(npiece):
        outputs(orows[q], halves[0])
        for lc, rt in tiles[1][q * per:(q + 1) * per]:
            weights_tile(lc, rt)
    for q in range(npiece):
        outputs(orows[q], halves[1])


def _peer(x1t, wqt, subkeys, u, vt, tt):
    dm, n = x1t.shape
    assert n % tt == 0 and tt % LANES == 0
    nt = n // tt
    nk = PEER_NKEYS
    nsub = PEER_EXPERT_ROWS
    hshape = (PEER_HEADS, nk, n)
    hspec = pl.BlockSpec((PEER_HEADS, nk, tt), lambda i: (0, 0, i))
    c, a, s2, e2 = pl.pallas_call(
        _peer_route_kernel,
        grid=(nt,),
        in_specs=[pl.BlockSpec((dm, tt), lambda i: (0, i)),
                  pl.BlockSpec(wqt.shape, lambda i: (0, 0)),
                  pl.BlockSpec(subkeys.shape, lambda i: (0, 0, 0, 0))],
        out_specs=[hspec] * 4,
        out_shape=[jax.ShapeDtypeStruct(hshape, F32)] * 4,
        compiler_params=_params(("parallel",)),
        name="peer_route",
    )(x1t, wqt, subkeys)

    full_spec = pl.BlockSpec((PEER_HEADS, nk, tt), lambda i, e: (0, 0, i))
    row_spec = pl.BlockSpec((PEER_HEADS, nsub, tt), lambda i, e: (0, e, i))
    anchor = jnp.zeros((8, tt), F32).at[1].set(1.0)
    return pl.pallas_call(
        _peer_dense_kernel,
        grid=(nt, nk // nsub),
        in_specs=[pl.BlockSpec((dm, tt), lambda i, e: (0, i)),
                  row_spec, row_spec, full_spec, full_spec,
                  pl.BlockSpec((8, tt), lambda i, e: (0, 0)),
                  pl.BlockSpec((nsub * nk, dm), lambda i, e: (e, 0)),
                  pl.BlockSpec((dm, nsub * nk), lambda i, e: (0, e))],
        out_specs=pl.BlockSpec((dm, tt), lambda i, e: (0, i)),
        out_shape=jax.ShapeDtypeStruct((dm, n), F32),
        scratch_shapes=[pltpu.VMEM((nsub * nk, tt), F32), pltpu.VMEM((nsub * nk, tt), F32),
                        pltpu.VMEM((nsub * nk, tt), BF16)],
        compiler_params=_params(("parallel", "arbitrary")),
        name="peer_dense",
    )(x1t, c, a, s2, e2, anchor, u, vt)


def _final_kernel(x1_ref, pt_ref, p_ref, wg_ref, wp_ref, g_ref, b_ref, o_ref, *, alpha):
    x1 = x1_ref[...]
    ple = jax.nn.sigmoid(_dot(x1.astype(BF16), wg_ref[...])) * _dot(p_ref[...], wp_ref[...])
    y = alpha * x1 + pt_ref[...].T + ple
    o_ref[...] = _layer_norm(y, g_ref[...], b_ref[...])


def _final(x1, peer_t, p_emb, w_gate, w_ple, g, b, tm, alpha):
    n, dm = x1.shape
    pd = p_emb.shape[1]

    def full(shape):
        return pl.BlockSpec(shape, lambda i: (0,) * len(shape))

    return pl.pallas_call(
        functools.partial(_final_kernel, alpha=alpha),
        grid=(n // tm,),
        in_specs=[pl.BlockSpec((tm, dm), lambda i: (i, 0)),
                  pl.BlockSpec((dm, tm), lambda i: (0, i)),
                  pl.BlockSpec((tm, pd), lambda i: (i, 0)),
                  full((dm, dm)), full((pd, dm)), full((1, dm)), full((1, dm))],
        out_specs=pl.BlockSpec((tm, dm), lambda i: (i, 0)),
        out_shape=jax.ShapeDtypeStruct((n, dm), F32),
        compiler_params=_params(("parallel",)),
        name="ple_ln2",
    )(x1, peer_t, p_emb, w_gate, w_ple, g, b)


def _layer(x, p_emb, pos_rows, tm, weights, attention_fn, state_in, chunk, chunk_eff, alpha):
    n = x.shape[0]
    nb = state_in.shape[0]
    t_seq = n // nb
    pr = _project(x.astype(BF16), weights["w_in"], weights["ikg"], weights["ikb"], pos_rows, tm)

    def pad_chunks(a):
        if t_seq % chunk == 0:
            return a
        a = a.reshape(nb, t_seq, a.shape[-1])
        return jnp.pad(a, ((0, 0), (0, chunk - t_seq), (0, 0))).reshape(nb * chunk, a.shape[-1])

    ret_o, s_out = _retention(pad_chunks(pr["rq"]), pad_chunks(pr["rk"]), pad_chunks(pr["rv"]),
                              pad_chunks(pr["rg"]), weights["gn_g"], weights["gn_b"],
                              state_in, chunk, chunk_eff)
    if t_seq % chunk != 0:
        ret_o = ret_o.reshape(nb, chunk, -1)[:, :t_seq].reshape(n, -1)
    att_o = attention_fn(pr)
    x1, x1t = _mix(x, ret_o, att_o, pr["sgab"], weights["w_ret_o"], weights["w_att_o"], weights["w_out"],
                   weights["ln1_g"], weights["ln1_b"], tm, alpha)
    peer_t = _peer(x1t, weights["peer_wqt"], weights["peer_subkeys"], weights["peer_u"], weights["peer_vt"],
                   min(512, n))
    y = _final(x1, peer_t, p_emb.astype(BF16), weights["w_ple_gate"], weights["w_ple"],
               weights["ln2_g"], weights["ln2_b"], tm, alpha)
    return y, pr["ak"], pr["av"], pr["ik"], s_out


def kernel(x_prompt, x_sample, cache_k, cache_v, cache_kidx, state_ret, page_table, p_prompt, p_sample,
           w_in, idx_k_g, idx_k_b, gn_g, gn_b, w_ret_o, w_att_o, w_out, ln1_g, ln1_b, peer_wq,
           peer_subkeys, peer_u, peer_v, w_ple_gate, w_ple, ln2_g, ln2_b):
    depth = w_in.shape[0]
    assert depth == 1, "single-layer trunk"
    nbp, seq, dm = x_prompt.shape
    nbs, t_new, _ = x_sample.shape
    past_len = page_table.shape[1] * cache_k.shape[2]
    alpha = (2.0 * depth) ** 0.25
    i = 0
    weights = dict(
        w_in=w_in[i].astype(BF16),
        ikg=jnp.concatenate([idx_k_g[i], idx_k_g[i]])[None, :],
        ikb=jnp.concatenate([idx_k_b[i], idx_k_b[i]])[None, :],
        gn_g=gn_g[i][None, :], gn_b=gn_b[i][None, :],
        w_ret_o=w_ret_o[i].astype(BF16), w_att_o=w_att_o[i].astype(BF16), w_out=w_out[i].astype(BF16),
        ln1_g=ln1_g[i][None, :], ln1_b=ln1_b[i][None, :],
        peer_wqt=_transpose(peer_wq[i], BF16), peer_subkeys=peer_subkeys[i].astype(BF16),
        peer_u=peer_u[i].astype(BF16), peer_vt=_transpose(peer_v[i], BF16),
        w_ple_gate=w_ple_gate[i].astype(BF16), w_ple=w_ple[i].astype(BF16),
        ln2_g=ln2_g[i][None, :], ln2_b=ln2_b[i][None, :],
    )

    def attn_p(pr):
        return _attention_prompt(pr["aq"], pr["iq"], pr["iw"], pr["akb"], pr["avb"], pr["ikb"], nbp, seq)

    yp, kp, vp, ikp, rp = _layer(
        x_prompt.reshape(nbp * seq, dm), p_prompt[i].reshape(nbp * seq, -1),
        jnp.arange(seq, dtype=jnp.int32), min(512, seq), weights, attn_p,
        jnp.zeros((nbp,) + state_ret.shape[2:], F32), RET_CHUNK, RET_CHUNK, alpha)

    def attn_s(pr):
        return _attention_sample(pr["aq"], pr["iq"], pr["iw"], pr["ak"], pr["av"], pr["ik"],
                                 cache_k[i], cache_v[i], cache_kidx[i], page_table, nbs, t_new)

    ns = nbs * t_new
    pos_s = jnp.tile(past_len + jnp.arange(t_new, dtype=jnp.int32), nbs)
    ys, ks, vs, iks, rs = _layer(
        x_sample.reshape(ns, dm), p_sample[i].reshape(ns, -1), pos_s, ns, weights, attn_s,
        state_ret[i].astype(F32), BF16_SUBLANES, t_new, alpha)

    return (yp.reshape(nbp, seq, dm), ys.reshape(nbs, t_new, dm),
            kp.reshape(1, nbp, seq, ATT_KV_HEADS, ATT_HD), vp.reshape(1, nbp, seq, ATT_KV_HEADS, ATT_HD),
            ikp.reshape(1, nbp, seq, IDX_HD), rp[None].astype(state_ret.dtype),
            ks.reshape(1, nbs, t_new, ATT_KV_HEADS, ATT_HD), vs.reshape(1, nbs, t_new, ATT_KV_HEADS, ATT_HD),
            iks.reshape(1, nbs, t_new, IDX_HD), rs[None].astype(state_ret.dtype))
```

```python
import functools

import jax
import jax.numpy as jnp
from jax import lax
from jax.experimental import pallas as pl
from jax.experimental.pallas import tpu as pltpu

F32 = jnp.float32
BF16 = jnp.bfloat16

RET_HEADS = 4
RET_DK = 256
RET_DV = 512
RET_CHUNK = 256
ATT_HEADS = 8
ATT_KV_HEADS = 2
ATT_HD = 128
IDX_HEADS = 8
IDX_HD = 64
TOPK_ATTN = 256
ROPE_THETA = 10000.0
PEER_HEADS = 8
PEER_NKEYS = 128
PEER_DKEY = 256
PEER_TOPK = 16
LN_EPS = 1e-5
NEG = -1e30
INF = float("inf")

LANES = 128
BF16_SUBLANES = 16
VMEM_LIMIT = 48 * 1024 * 1024
BISECT_ITERS = 32
SAMPLE_PAGES_PER_STEP = 16


def _dot(a, b):
    return jnp.dot(a, b, preferred_element_type=F32)


def _dot_nt(a, b):
    return lax.dot_general(a, b, (((1,), (1,)), ((), ())), preferred_element_type=F32)


def _dot_tn(a, b):
    return lax.dot_general(a, b, (((0,), (0,)), ((), ())), preferred_element_type=F32)


def _params(sem):
    return pltpu.CompilerParams(dimension_semantics=sem, vmem_limit_bytes=VMEM_LIMIT)


def _rope_tables(pos, d, width):
    half = d // 2
    inv = ROPE_THETA ** (-jnp.arange(half, dtype=F32) / half)
    ang = pos.astype(F32)[:, None] * inv[None, :]
    cos, sin = jnp.cos(ang), jnp.sin(ang)
    cosf = jnp.concatenate([cos, cos], axis=-1)
    sinf = jnp.concatenate([-sin, sin], axis=-1)
    reps = width // d
    return jnp.tile(cosf, (1, reps)), jnp.tile(sinf, (1, reps))


def _partner(y, d):
    if d == 2 * LANES:
        return jnp.concatenate([y[:, LANES:], y[:, :LANES]], axis=-1)
    if d == LANES:
        return pltpu.roll(y, LANES // 2, axis=1)
    lane = lax.broadcasted_iota(jnp.int32, y.shape, 1)
    first_half = (lane & (d - 1)) < (d // 2)
    return jnp.where(first_half, pltpu.roll(y, LANES - d // 2, axis=1), pltpu.roll(y, d // 2, axis=1))


def _rope(y, cosf, sinf, d):
    return y * cosf + _partner(y, d) * sinf


def _ret_qk_kernel(x_ref, w_ref, cos_ref, sin_ref, q_ref, k_ref):
    x = x_ref[...]
    cosf, sinf = cos_ref[...], sin_ref[...]
    for h in range(2 * RET_HEADS):
        y = _dot(x, w_ref[:, h * RET_DK:(h + 1) * RET_DK])
        r = _rope(y, cosf, sinf, RET_DK)
        if h < RET_HEADS:
            q_ref[:, h * RET_DK:(h + 1) * RET_DK] = r.astype(BF16)
        else:
            hh = h - RET_HEADS
            k_ref[:, hh * RET_DK:(hh + 1) * RET_DK] = (r * (RET_DK ** -0.5)).astype(BF16)


def _plain_kernel(x_ref, w_ref, o_ref, *, sigmoid):
    y = _dot(x_ref[...], w_ref[...])
    if sigmoid:
        y = jax.nn.sigmoid(y)
    o_ref[...] = y.astype(o_ref.dtype)


def _att_proj_kernel(x_ref, wq_ref, wk_ref, wv_ref, cos_ref, sin_ref,
                     q_ref, k_ref, v_ref, kb_ref, vb_ref):
    x = x_ref[...]
    cosf, sinf = cos_ref[...], sin_ref[...]
    for h in range(ATT_HEADS):
        sl = slice(h * ATT_HD, (h + 1) * ATT_HD)
        q_ref[:, sl] = _rope(_dot(x, wq_ref[:, sl]), cosf, sinf, ATT_HD).astype(BF16)
    tm = x.shape[0]
    v = _dot(x, wv_ref[...])
    vb_ref[...] = v.astype(BF16)
    for h in range(ATT_KV_HEADS):
        sl = slice(h * ATT_HD, (h + 1) * ATT_HD)
        k = _rope(_dot(x, wk_ref[:, sl]), cosf, sinf, ATT_HD)
        kb_ref[:, sl] = k.astype(BF16)
        head_rows = pl.ds(h, tm, stride=ATT_KV_HEADS)
        k_ref[head_rows, :] = k
        v_ref[head_rows, :] = v[:, sl]


def _idx_proj_kernel(x_ref, wq_ref, wk_ref, ww_ref, cos_ref, sin_ref, g_ref, b_ref,
                     q_ref, k_ref, kb_ref, w_ref):
    x = x_ref[...]
    cosf, sinf = cos_ref[...], sin_ref[...]
    for c in range(IDX_HEADS * IDX_HD // LANES):
        sl = slice(c * LANES, (c + 1) * LANES)
        q_ref[:, sl] = _rope(_dot(x, wq_ref[:, sl]), cosf, sinf, IDX_HD).astype(BF16)
    y = _dot(x, wk_ref[...])
    mu = jnp.mean(y, axis=-1, keepdims=True)
    var = jnp.mean(jnp.square(y - mu), axis=-1, keepdims=True)
    yn = (y - mu) * lax.rsqrt(var + LN_EPS) * g_ref[...] + b_ref[...]
    k = yn * cosf + pltpu.roll(yn, IDX_HD // 2, axis=1) * sinf
    k_ref[...] = k[:, :IDX_HD]
    kb_ref[...] = k[:, :IDX_HD].astype(BF16)
    w = _dot(x, ww_ref[...])
    w_ref[...] = w[:, :IDX_HEADS] * (IDX_HEADS ** -0.5)


def _project(xb, w, ikg, ikb, pos_rows, tm):
    n, dm = xb.shape
    nt = n // tm
    rows = pos_rows.shape[0]
    tab_blocks = rows // tm
    hq = RET_HEADS * RET_DK
    hv = RET_HEADS * RET_DV
    o = 0
    w_rqk = w[:, o:o + 2 * hq]; o += 2 * hq
    w_rv = w[:, o:o + hv]; o += hv
    w_rg = w[:, o:o + hv]; o += hv
    w_aq = w[:, o:o + ATT_HEADS * ATT_HD]; o += ATT_HEADS * ATT_HD
    w_ak = w[:, o:o + ATT_KV_HEADS * ATT_HD]; o += ATT_KV_HEADS * ATT_HD
    w_av = w[:, o:o + ATT_KV_HEADS * ATT_HD]; o += ATT_KV_HEADS * ATT_HD
    w_iq = w[:, o:o + IDX_HEADS * IDX_HD]; o += IDX_HEADS * IDX_HD
    w_ik = w[:, o:o + IDX_HD]; o += IDX_HD
    w_iw = w[:, o:o + IDX_HEADS]; o += IDX_HEADS
    w_gab = w[:, o:]

    x_spec = pl.BlockSpec((tm, dm), lambda i: (i, 0))

    def tab_spec(width):
        return pl.BlockSpec((tm, width), lambda i: (i % tab_blocks, 0))

    def full(shape):
        return pl.BlockSpec(shape, lambda i: (0,) * len(shape))

    def row_spec(width):
        return pl.BlockSpec((tm, width), lambda i: (i, 0))

    cos256, sin256 = _rope_tables(pos_rows, RET_DK, RET_DK)
    rq, rk = pl.pallas_call(
        _ret_qk_kernel,
        grid=(nt,),
        in_specs=[x_spec, full((dm, 2 * hq)), tab_spec(RET_DK), tab_spec(RET_DK)],
        out_specs=[row_spec(hq), row_spec(hq)],
        out_shape=[jax.ShapeDtypeStruct((n, hq), BF16)] * 2,
        compiler_params=_params(("parallel",)),
        name="proj_ret_qk",
    )(xb, w_rqk, cos256, sin256)

    def plain(wc, dtype, sigmoid, name):
        cols = wc.shape[1]
        tn = min(1024, cols)
        return pl.pallas_call(
            functools.partial(_plain_kernel, sigmoid=sigmoid),
            grid=(cols // tn, nt),
            in_specs=[pl.BlockSpec((tm, dm), lambda j, i: (i, 0)),
                      pl.BlockSpec((dm, tn), lambda j, i: (0, j))],
            out_specs=pl.BlockSpec((tm, tn), lambda j, i: (i, j)),
            out_shape=jax.ShapeDtypeStruct((n, cols), dtype),
            compiler_params=_params(("parallel", "parallel")),
            name=name,
        )(xb, wc)

    rv = plain(w_rv, BF16, False, "proj_ret_v")
    rg = plain(w_rg, F32, False, "proj_ret_gate")
    sgab = plain(w_gab, F32, True, "proj_branch_gates")

    cos128, sin128 = _rope_tables(pos_rows, ATT_HD, ATT_HD)
    kvw = ATT_KV_HEADS * ATT_HD
    aq, ak, av, akb, avb = pl.pallas_call(
        _att_proj_kernel,
        grid=(nt,),
        in_specs=[x_spec, full((dm, ATT_HEADS * ATT_HD)), full((dm, kvw)), full((dm, kvw)),
                  tab_spec(ATT_HD), tab_spec(ATT_HD)],
        out_specs=[row_spec(ATT_HEADS * ATT_HD),
                   pl.BlockSpec((tm * ATT_KV_HEADS, ATT_HD), lambda i: (i, 0)),
                   pl.BlockSpec((tm * ATT_KV_HEADS, ATT_HD), lambda i: (i, 0)),
                   row_spec(kvw), row_spec(kvw)],
        out_shape=[jax.ShapeDtypeStruct((n, ATT_HEADS * ATT_HD), BF16),
                   jax.ShapeDtypeStruct((n * ATT_KV_HEADS, ATT_HD), F32),
                   jax.ShapeDtypeStruct((n * ATT_KV_HEADS, ATT_HD), F32),
                   jax.ShapeDtypeStruct((n, kvw), BF16), jax.ShapeDtypeStruct((n, kvw), BF16)],
        compiler_params=_params(("parallel",)),
        name="proj_att",
    )(xb, w_aq, w_ak, w_av, cos128, sin128)

    cos64, sin64 = _rope_tables(pos_rows, IDX_HD, LANES)
    w_ik2 = jnp.concatenate([w_ik, w_ik], axis=1)
    w_iw_pad = jnp.pad(w_iw, ((0, 0), (0, LANES - IDX_HEADS)))
    iq, ik, ikb, iw = pl.pallas_call(
        _idx_proj_kernel,
        grid=(nt,),
        in_specs=[x_spec, full((dm, IDX_HEADS * IDX_HD)), full((dm, LANES)), full((dm, LANES)),
                  tab_spec(LANES), tab_spec(LANES), full((1, LANES)), full((1, LANES))],
        out_specs=[row_spec(IDX_HEADS * IDX_HD), row_spec(IDX_HD), row_spec(IDX_HD), row_spec(IDX_HEADS)],
        out_shape=[jax.ShapeDtypeStruct((n, IDX_HEADS * IDX_HD), BF16),
                   jax.ShapeDtypeStruct((n, IDX_HD), F32), jax.ShapeDtypeStruct((n, IDX_HD), BF16),
                   jax.ShapeDtypeStruct((n, IDX_HEADS), F32)],
        compiler_params=_params(("parallel",)),
        name="proj_idx",
    )(xb, w_iq, w_ik2, w_iw_pad, cos64, sin64, ikg, ikb)
    return dict(rq=rq, rk=rk, rv=rv, rg=rg, sgab=sgab, aq=aq, ak=ak, av=av, akb=akb, avb=avb,
                iq=iq, ik=ik, ikb=ikb, iw=iw)


def _retention_kernel(sdec_ref, q_ref, k_ref, v_ref, gate_ref, gg_ref, gb_ref,
                      d_ref, cdec_ref, kdec_ref, si_ref, o_ref, so_ref):
    h = pl.program_id(1)
    c = pl.program_id(2)

    @pl.when(c == 0)
    def _():
        so_ref[0, 0] = si_ref[0, 0]

    q = q_ref[...]
    k = k_ref[...]
    v = v_ref[...]
    state = so_ref[0, 0]
    s = _dot_nt(q, k)
    inner = _dot((s * d_ref[0]).astype(BF16), v)
    cross = _dot(q, state.astype(BF16)) * cdec_ref[0]
    o = inner + cross
    kd = (k.astype(F32) * kdec_ref[0]).astype(BF16)
    so_ref[0, 0] = state * sdec_ref[h] + _dot_tn(kd, v)
    mu = jnp.mean(o, axis=-1, keepdims=True)
    var = jnp.mean(jnp.square(o - mu), axis=-1, keepdims=True)
    on = (o - mu) * lax.rsqrt(var + LN_EPS) * gg_ref[...] + gb_ref[...]
    o_ref[...] = (jax.nn.silu(gate_ref[...]) * on).astype(o_ref.dtype)


def _retention(rq, rk, rv, rg, gn_g, gn_b, state_in, chunk, chunk_eff):
    n = rq.shape[0]
    nb = state_in.shape[0]
    nc = n // (nb * chunk)
    lg = jnp.log1p(-jnp.exp2(-5.0 - jnp.arange(RET_HEADS, dtype=F32)))[:, None]
    i = jnp.arange(chunk, dtype=F32)
    diff = i[:, None] - i[None, :]
    decay = jnp.where(diff >= 0, jnp.exp(lg[:, :, None] * jnp.maximum(diff, 0.0)), 0.0)
    cdec = jnp.exp(lg * (i + 1.0))[:, :, None]
    kdec = jnp.where(i < chunk_eff, jnp.exp(lg * jnp.maximum(chunk_eff - 1.0 - i, 0.0)), 0.0)[:, :, None]
    sdec = jnp.exp(lg[:, 0] * chunk_eff)

    o, s_out = pl.pallas_call(
        _retention_kernel,
        grid=(nb, RET_HEADS, nc),
        in_specs=[
            pl.BlockSpec(memory_space=pltpu.SMEM),
            pl.BlockSpec((chunk, RET_DK), lambda b, h, c: (b * nc + c, h)),
            pl.BlockSpec((chunk, RET_DK), lambda b, h, c: (b * nc + c, h)),
            pl.BlockSpec((chunk, RET_DV), lambda b, h, c: (b * nc + c, h)),
            pl.BlockSpec((chunk, RET_DV), lambda b, h, c: (b * nc + c, h)),
            pl.BlockSpec((1, RET_DV), lambda b, h, c: (0, h)),
            pl.BlockSpec((1, RET_DV), lambda b, h, c: (0, h)),
            pl.BlockSpec((1, chunk, chunk), lambda b, h, c: (h, 0, 0)),
            pl.BlockSpec((1, chunk, 1), lambda b, h, c: (h, 0, 0)),
            pl.BlockSpec((1, chunk, 1), lambda b, h, c: (h, 0, 0)),
            pl.BlockSpec((1, 1, RET_DK, RET_DV), lambda b, h, c: (b, h, 0, 0)),
        ],
        out_specs=[
            pl.BlockSpec((chunk, RET_DV), lambda b, h, c: (b * nc + c, h)),
            pl.BlockSpec((1, 1, RET_DK, RET_DV), lambda b, h, c: (b, h, 0, 0)),
        ],
        out_shape=[jax.ShapeDtypeStruct((n, RET_HEADS * RET_DV), BF16),
                   jax.ShapeDtypeStruct(state_in.shape, F32)],
        compiler_params=_params(("parallel", "parallel", "arbitrary")),
        name="retention",
    )(sdec, rq, rk, rv, rg, gn_g, gn_b, decay, cdec, kdec, state_in)
    return o, s_out


M_INIT = -1e29


def _lanes(col):
    return jnp.broadcast_to(col, (col.shape[0], LANES))


def _kth_largest(reduce_tiles, mn, mx, kq):
    kq_b = _lanes(kq)

    def bisect_body(_, carry):
        lo, hi = carry
        mid = 0.5 * (lo + hi)
        cnt = reduce_tiles(lambda a, t: a + jnp.where(t >= mid, 1.0, 0.0), 0.0)
        ge = _lanes(jnp.sum(cnt, axis=-1, keepdims=True)) >= kq_b
        return jnp.where(ge, mid, lo), jnp.where(ge, hi, mid)

    lo, _ = lax.fori_loop(0, BISECT_ITERS, bisect_body, (_lanes(mn), _lanes(mx + 1.0)))
    tau = jnp.min(reduce_tiles(lambda a, t: jnp.minimum(a, jnp.where(t >= lo, t, INF)), INF),
                  axis=-1, keepdims=True)
    tau_b = _lanes(tau)
    cgt = jnp.sum(reduce_tiles(lambda a, t: a + jnp.where(t > tau_b, 1.0, 0.0), 0.0),
                  axis=-1, keepdims=True)
    return tau, kq - cgt


def _softmax_step(s, m_ref, l_ref, acc_ref, idx, v):
    m_old = m_ref[idx]
    m_new = jnp.maximum(m_old, jnp.max(s, axis=-1, keepdims=True))
    p = jnp.exp(s - m_new)
    alpha = jnp.exp(m_old - m_new)
    l_ref[idx] = alpha * l_ref[idx] + jnp.sum(p, axis=-1, keepdims=True)
    acc_ref[idx] = alpha * acc_ref[idx] + _dot(p.astype(BF16), v)
    m_ref[idx] = m_new


def _fold_rows(x):
    return jnp.sum(x.reshape(x.shape[0] // 8, 8, x.shape[1]), axis=0)


def _attn_prompt_kernel(aqt_ref, iqt_ref, iwt_ref, k_ref, vt_ref, kidx_ref, tril_ref, o_ref,
                        sc_ref, m_ref, l_ref, acc_ref, *, tq, topk):
    i = pl.program_id(1)
    nkv = i + 1
    qpos = i * tq + lax.broadcasted_iota(jnp.int32, (1, tq), 1)
    kq = jnp.minimum(topk, qpos + 1).astype(F32)
    group = ATT_HEADS // ATT_KV_HEADS

    def score_body(j, carry):
        mn, mx = carry
        kx = kidx_ref[pl.ds(pl.multiple_of(j * tq, tq), tq), :]
        s = jnp.zeros((tq, tq), F32)
        for h in range(IDX_HEADS):
            d = _dot(kx, iqt_ref[h * IDX_HD:(h + 1) * IDX_HD, :]) * (IDX_HD ** -0.5)
            s = s + iwt_ref[h:h + 1, :] * jnp.maximum(d, 0.0)
        kpos = j * tq + lax.broadcasted_iota(jnp.int32, (tq, tq), 0)
        vis = kpos <= qpos
        sc_ref[j] = jnp.where(vis, s, -INF)
        mn = jnp.minimum(mn, jnp.min(jnp.where(vis, s, INF), axis=0, keepdims=True))
        mx = jnp.maximum(mx, jnp.max(jnp.where(vis, s, -INF), axis=0, keepdims=True))
        return mn, mx

    mn, mx = lax.fori_loop(0, nkv, score_body,
                           (jnp.full((1, tq), INF, F32), jnp.full((1, tq), -INF, F32)))

    def count(pred):
        def body(j, acc):
            return acc + _fold_rows(jnp.where(pred(sc_ref[j]), 1.0, 0.0))
        return jnp.sum(lax.fori_loop(0, nkv, body, jnp.zeros((8, tq), F32)), axis=0, keepdims=True)

    def bisect_body(_, carry):
        lo, hi = carry
        mid = 0.5 * (lo + hi)
        ge = count(lambda t: t >= mid) >= kq
        return jnp.where(ge, mid, lo), jnp.where(ge, hi, mid)

    lo, _ = lax.fori_loop(0, BISECT_ITERS, bisect_body, (mn, mx + 1.0))

    def snap_body(j, acc):
        t = sc_ref[j]
        return jnp.minimum(acc, jnp.min(jnp.where(t >= lo, t, INF), axis=0, keepdims=True))

    tau = lax.fori_loop(0, nkv, snap_body, jnp.full((1, tq), INF, F32))
    need = kq - count(lambda t: t > tau)

    tril = tril_ref[...]

    def mask_body(j, run):
        t = sc_ref[j]
        eqf = jnp.where(t == tau, 1.0, 0.0)
        prefix = _dot(tril, eqf.astype(BF16))
        sel = (t > tau) | ((t == tau) & ((run + prefix) <= need))
        sc_ref[j] = jnp.where(sel, 0.0, NEG)
        return run + jnp.sum(eqf, axis=0, keepdims=True)

    lax.fori_loop(0, nkv, mask_body, jnp.zeros((1, tq), F32))

    m_ref[...] = jnp.full(m_ref.shape, M_INIT, F32)
    l_ref[...] = jnp.zeros(l_ref.shape, F32)
    acc_ref[...] = jnp.zeros(acc_ref.shape, F32)

    def att_body(j, carry):
        rows = pl.ds(pl.multiple_of(j * tq, tq), tq)
        for h in range(ATT_HEADS):
            g = h // group
            kg = k_ref[rows, g * ATT_HD:(g + 1) * ATT_HD]
            vg = vt_ref[0, j, g * ATT_HD:(g + 1) * ATT_HD, :]
            for c in range(tq // LANES):
                ls = slice(c * LANES, (c + 1) * LANES)
                s = _dot(kg, aqt_ref[h * ATT_HD:(h + 1) * ATT_HD, ls]) * (ATT_HD ** -0.5) + sc_ref[j, :, ls]
                m_old = m_ref[h, :, ls]
                m_new = jnp.maximum(m_old, jnp.max(s, axis=0, keepdims=True))
                p = jnp.exp(s - m_new)
                alpha = jnp.exp(m_old - m_new)
                l_ref[h, :, ls] = alpha * l_ref[h, :, ls] + jnp.sum(p, axis=0, keepdims=True)
                acc_ref[h, :, ls] = alpha * acc_ref[h, :, ls] + _dot(vg, p.astype(BF16))
                m_ref[h, :, ls] = m_new
        return carry

    lax.fori_loop(0, nkv, att_body, 0)
    for h in range(ATT_HEADS):
        o_ref[h * ATT_HD:(h + 1) * ATT_HD, :] = (acc_ref[h] / l_ref[h]).astype(o_ref.dtype)


def _attention_prompt(aq, iq, iw, akb, avb, ikb, nb, seq):
    tq = min(256, seq)
    assert seq % tq == 0 and tq % LANES == 0
    nq = seq // tq
    topk = min(TOPK_ATTN, seq // 4)
    n = aq.shape[0]
    tril = jnp.tril(jnp.ones((tq, tq), F32)).astype(BF16)
    kvw = ATT_KV_HEADS * ATT_HD
    vt = avb.reshape(nb, nq, tq, kvw).transpose(0, 1, 3, 2)
    ot = pl.pallas_call(
        functools.partial(_attn_prompt_kernel, tq=tq, topk=topk),
        grid=(nb, nq),
        in_specs=[
            pl.BlockSpec((ATT_HEADS * ATT_HD, tq), lambda b, i: (0, b * nq + i)),
            pl.BlockSpec((IDX_HEADS * IDX_HD, tq), lambda b, i: (0, b * nq + i)),
            pl.BlockSpec((IDX_HEADS, tq), lambda b, i: (0, b * nq + i)),
            pl.BlockSpec((seq, kvw), lambda b, i: (b, 0)),
            pl.BlockSpec((1, nq, kvw, tq), lambda b, i: (b, 0, 0, 0)),
            pl.BlockSpec((seq, IDX_HD), lambda b, i: (b, 0)),
            pl.BlockSpec((tq, tq), lambda b, i: (0, 0)),
        ],
        out_specs=pl.BlockSpec((ATT_HEADS * ATT_HD, tq), lambda b, i: (0, b * nq + i)),
        out_shape=jax.ShapeDtypeStruct((ATT_HEADS * ATT_HD, n), BF16),
        scratch_shapes=[
            pltpu.VMEM((nq, tq, tq), F32),
            pltpu.VMEM((ATT_HEADS, 1, tq), F32),
            pltpu.VMEM((ATT_HEADS, 1, tq), F32),
            pltpu.VMEM((ATT_HEADS, ATT_HD, tq), F32),
        ],
        compiler_params=_params(("parallel", "arbitrary")),
        name="attention_prompt",
    )(aq.T, iq.T, iw.T, akb, vt, ikb, tril)
    return ot.T


def _page_scores(iq_ref, iw_ref, kxt):
    tp = iq_ref.shape[1] // IDX_HEADS
    d = _dot(iq_ref[0], kxt) * (IDX_HD ** -0.5)
    d = iw_ref[0] * jnp.maximum(d, 0.0)
    s = d[:tp]
    for h in range(1, IDX_HEADS):
        s = s + d[h * tp:(h + 1) * tp]
    return s


def _sample_scores_kernel(pt_ref, iq_ref, iw_ref, *refs, npp, n_new, topk):
    page_refs = refs[:npp]
    knew_ref = refs[npp]
    sc_ref, tau_ref, need_ref = refs[npp + 1:]
    s = pl.program_id(1)
    npages = sc_ref.shape[1] - 1
    tp, page = sc_ref.shape[2], sc_ref.shape[3]

    kxt = jnp.concatenate([r[0] for r in page_refs], axis=1).astype(BF16)
    sc = _page_scores(iq_ref, iw_ref, kxt)
    for p in range(npp):
        sc_ref[0, s * npp + p] = sc[:, p * page:(p + 1) * page]

    @pl.when(s == pl.num_programs(1) - 1)
    def _():
        snew = _page_scores(iq_ref, iw_ref, knew_ref[0].astype(BF16))
        t = jnp.minimum(lax.broadcasted_iota(jnp.int32, (tp, page), 0), n_new - 1)
        j = lax.broadcasted_iota(jnp.int32, (tp, page), 1)
        sc_ref[0, npages] = jnp.where(j <= t, snew, -INF)

        def reduce_tiles(fn, init):
            def body(c, acc):
                for p in range(npp):
                    acc = fn(acc, sc_ref[0, c * npp + p])
                return acc
            acc = lax.fori_loop(0, npages // npp, body, jnp.full((tp, page), init, F32))
            return fn(acc, sc_ref[0, npages])

        mn = jnp.min(reduce_tiles(lambda a, x: jnp.minimum(a, jnp.where(x == -INF, INF, x)), INF),
                     axis=-1, keepdims=True)
        mx = jnp.max(reduce_tiles(jnp.maximum, -INF), axis=-1, keepdims=True)
        tau, need = _kth_largest(reduce_tiles, mn, mx, jnp.full((tp, 1), float(topk), F32))
        tau_ref[0] = _lanes(tau)
        need_ref[0] = _lanes(need)


def _sample_attend_kernel(pt_ref, q_ref, sc_ref, tau_ref, need_ref, tri_ref, *refs, npp):
    k_refs = refs[:npp]
    v_refs = refs[npp:2 * npp]
    knew_ref, vnew_ref, o_ref, m_ref, l_ref, acc_ref, run_ref = refs[2 * npp:]
    s = pl.program_id(1)
    npages = sc_ref.shape[1] - 1
    page = sc_ref.shape[3]
    group = ATT_HEADS // ATT_KV_HEADS
    grows = q_ref.shape[1] // ATT_KV_HEADS

    @pl.when(s == 0)
    def _():
        m_ref[...] = jnp.full(m_ref.shape, M_INIT, F32)
        l_ref[...] = jnp.zeros(l_ref.shape, F32)
        acc_ref[...] = jnp.zeros(acc_ref.shape, F32)
        run_ref[...] = jnp.zeros(run_ref.shape, F32)

    tau = tau_ref[0][:, :1]
    need = need_ref[0][:, :1]
    tri = tri_ref[...]

    def attend(pages, krefs, vrefs):
        tp = sc_ref.shape[2]
        scs = [sc_ref[0, pg] for pg in pages]
        eqs = [jnp.where(t == tau, 1.0, 0.0) for t in scs]
        prefix = _dot(jnp.concatenate(eqs, axis=0).astype(BF16), tri)
        run = run_ref[...]
        madds = []
        for k, t in enumerate(scs):
            sel = (t > tau) | ((t == tau) & ((run + prefix[k * tp:(k + 1) * tp]) <= need))
            madds.append(jnp.where(sel, 0.0, NEG))
            run = run + jnp.sum(eqs[k], axis=-1, keepdims=True)
        run_ref[...] = run
        madd = jnp.concatenate(madds, axis=1)
        maddg = jnp.concatenate([madd] * group, axis=0)
        for g in range(ATT_KV_HEADS):
            head_rows = pl.ds(g, page, stride=ATT_KV_HEADS)
            kg = jnp.concatenate([r[0, head_rows, :] for r in krefs], axis=0).astype(BF16)
            vg = jnp.concatenate([r[0, head_rows, :] for r in vrefs], axis=0).astype(BF16)
            sg = _dot_nt(q_ref[0, g * grows:(g + 1) * grows, :], kg) * (ATT_HD ** -0.5) + maddg
            _softmax_step(sg, m_ref, l_ref, acc_ref, g, vg)

    attend([s * npp + p for p in range(npp)], k_refs, v_refs)

    @pl.when(s == pl.num_programs(1) - 1)
    def _():
        attend([npages], [knew_ref], [vnew_ref])
        for g in range(ATT_KV_HEADS):
            o_ref[0, g * grows:(g + 1) * grows, :] = acc_ref[g] / l_ref[g]


def _attention_sample(aq, iq, iw, ak, av, ik, cache_k, cache_v, cache_kidx, page_table, nb, t_new):
    n_phys, page = cache_kidx.shape[0], cache_kidx.shape[1]
    npages = page_table.shape[1]
    npp = SAMPLE_PAGES_PER_STEP if npages % SAMPLE_PAGES_PER_STEP == 0 else 1
    nsteps = npages // npp
    tp = BF16_SUBLANES
    topk = min(TOPK_ATTN, (npages * page + t_new) // 4)
    prow = page * ATT_KV_HEADS

    def pad_rows(a, per_seq, rows):
        a = a.reshape(nb, per_seq, a.shape[-1])
        return jnp.pad(a, ((0, 0), (0, rows - per_seq), (0, 0)))

    iq_p = pad_rows(iq, t_new, tp).reshape(nb, tp, IDX_HEADS, IDX_HD).transpose(0, 2, 1, 3)
    iq_p = iq_p.reshape(nb, IDX_HEADS * tp, IDX_HD)
    iw_p = pad_rows(iw, t_new, tp).transpose(0, 2, 1).reshape(nb, IDX_HEADS * tp, 1)
    q_p = pad_rows(aq, t_new, tp).reshape(nb, tp, ATT_HEADS, ATT_HD).transpose(0, 2, 1, 3)
    q_p = q_p.reshape(nb, ATT_HEADS * tp, ATT_HD)
    knew = pad_rows(ak, t_new * ATT_KV_HEADS, prow)
    vnew = pad_rows(av, t_new * ATT_KV_HEADS, prow)
    kinew = pad_rows(ik, t_new, page).swapaxes(1, 2)
    kidx_t = cache_kidx.swapaxes(1, 2)
    ck = cache_k.reshape(n_phys, prow, ATT_HD)
    cv = cache_v.reshape(n_phys, prow, ATT_HD)
    tri = jnp.triu(jnp.ones((page, page), F32)).astype(BF16)

    def page_spec(rows, width, p):
        return pl.BlockSpec((1, rows, width), lambda b, s, pt, p=p: (pt[b, s * npp + p], 0, 0))

    def seq_spec(shape):
        return pl.BlockSpec((1,) + shape, lambda b, s, pt: (b,) + (0,) * len(shape))

    sc, tau, need = pl.pallas_call(
        functools.partial(_sample_scores_kernel, npp=npp, n_new=t_new, topk=topk),
        grid_spec=pltpu.PrefetchScalarGridSpec(
            num_scalar_prefetch=1,
            grid=(nb, nsteps),
            in_specs=[seq_spec((IDX_HEADS * tp, IDX_HD)), seq_spec((IDX_HEADS * tp, 1))]
                     + [page_spec(IDX_HD, page, p) for p in range(npp)] + [seq_spec((IDX_HD, page))],
            out_specs=[seq_spec((npages + 1, tp, page)), seq_spec((tp, page)), seq_spec((tp, page))],
        ),
        out_shape=[jax.ShapeDtypeStruct((nb, npages + 1, tp, page), F32),
                   jax.ShapeDtypeStruct((nb, tp, page), F32),
                   jax.ShapeDtypeStruct((nb, tp, page), F32)],
        compiler_params=_params(("parallel", "arbitrary")),
        name="sample_scores",
    )(page_table, iq_p, iw_p, *([kidx_t] * npp), kinew)

    grows = (ATT_HEADS // ATT_KV_HEADS) * tp
    o = pl.pallas_call(
        functools.partial(_sample_attend_kernel, npp=npp),
        grid_spec=pltpu.PrefetchScalarGridSpec(
            num_scalar_prefetch=1,
            grid=(nb, nsteps),
            in_specs=[seq_spec((ATT_HEADS * tp, ATT_HD)), seq_spec((npages + 1, tp, page)),
                      seq_spec((tp, page)), seq_spec((tp, page)),
                      pl.BlockSpec((page, page), lambda b, s, pt: (0, 0))]
                     + [page_spec(prow, ATT_HD, p) for p in range(npp)]
                     + [page_spec(prow, ATT_HD, p) for p in range(npp)]
                     + [seq_spec((prow, ATT_HD)), seq_spec((prow, ATT_HD))],
            out_specs=seq_spec((ATT_HEADS * tp, ATT_HD)),
            scratch_shapes=[
                pltpu.VMEM((ATT_KV_HEADS, grows, 1), F32),
                pltpu.VMEM((ATT_KV_HEADS, grows, 1), F32),
                pltpu.VMEM((ATT_KV_HEADS, grows, ATT_HD), F32),
                pltpu.VMEM((tp, 1), F32),
            ],
        ),
        out_shape=jax.ShapeDtypeStruct((nb, ATT_HEADS * tp, ATT_HD), F32),
        compiler_params=_params(("parallel", "arbitrary")),
        name="sample_attend",
    )(page_table, q_p, sc, tau, need, tri, *([ck] * npp), *([cv] * npp), knew, vnew)
    o = o.reshape(nb, ATT_HEADS, tp, ATT_HD)[:, :, :t_new].transpose(0, 2, 1, 3)
    return o.reshape(nb * t_new, ATT_HEADS * ATT_HD).astype(BF16)


def _layer_norm(y, g, b):
    mu = jnp.mean(y, axis=-1, keepdims=True)
    var = jnp.mean(jnp.square(y - mu), axis=-1, keepdims=True)
    return (y - mu) * lax.rsqrt(var + LN_EPS) * g + b


def _mix_kernel(x_ref, ret_ref, att_ref, sga_ref, sgb_ref, wr_ref, wa_ref, wo_ref, g_ref, b_ref,
                o_ref, ot_ref, *, alpha):
    branch = sga_ref[...] * _dot(ret_ref[...], wr_ref[...]) + sgb_ref[...] * _dot(att_ref[...], wa_ref[...])
    y = alpha * x_ref[...] + _dot(branch.astype(BF16), wo_ref[...])
    x1 = _layer_norm(y, g_ref[...], b_ref[...])
    o_ref[...] = x1
    ot_ref[...] = x1.T.astype(BF16)


def _mix(x, ret_o, att_o, sgab, w_ret_o, w_att_o, w_out, g, b, tm, alpha):
    n, dm = x.shape
    hv = RET_HEADS * RET_DV
    ha = ATT_HEADS * ATT_HD

    def full(shape):
        return pl.BlockSpec(shape, lambda i: (0,) * len(shape))

    return pl.pallas_call(
        functools.partial(_mix_kernel, alpha=alpha),
        grid=(n // tm,),
        in_specs=[pl.BlockSpec((tm, dm), lambda i: (i, 0)),
                  pl.BlockSpec((tm, hv), lambda i: (i, 0)),
                  pl.BlockSpec((tm, ha), lambda i: (i, 0)),
                  pl.BlockSpec((tm, dm), lambda i: (i, 0)),
                  pl.BlockSpec((tm, dm), lambda i: (i, 1)),
                  full((hv, dm)), full((ha, dm)), full((dm, dm)), full((1, dm)), full((1, dm))],
        out_specs=[pl.BlockSpec((tm, dm), lambda i: (i, 0)),
                   pl.BlockSpec((dm, tm), lambda i: (0, i))],
        out_shape=[jax.ShapeDtypeStruct((n, dm), F32), jax.ShapeDtypeStruct((dm, n), BF16)],
        compiler_params=_params(("parallel",)),
        name="branch_mix_ln1",
    )(x, ret_o, att_o, sgab, sgab, w_ret_o, w_att_o, w_out, g, b)


PEER_EXPERT_ROWS = 8
PEER_ROW_TILE = 32


def _top_values(cur, k):
    rows = []
    for _ in range(k):
        m = jnp.max(cur, axis=0, keepdims=True)
        rows.append(m)
        cur = jnp.where(cur == m, -INF, cur)
    return rows


def _peer_route_kernel(xt_ref, wqt_ref, sk_ref, c_ref, a_ref, s2_ref, e2_ref):
    xt = xt_ref[...]
    half = PEER_DKEY // 2
    k1 = PEER_TOPK + 1
    pairs = [(a, b) for a in range(k1) for b in range(k1 // (a + 1))]
    for h in range(PEER_HEADS):
        qt = _dot(wqt_ref[h * PEER_DKEY:(h + 1) * PEER_DKEY, :], xt).astype(BF16)
        s1_ref = c_ref.at[h]
        s1_ref[...] = _dot(sk_ref[h, 0], qt[:half])
        s2_ref[h] = _dot(sk_ref[h, 1], qt[half:])
        for lc in range(xt.shape[1] // LANES):
            ls = slice(lc * LANES, (lc + 1) * LANES)
            s1 = s1_ref[:, ls]
            s2 = s2_ref[h, :, ls]
            v1 = _top_values(s1, k1)
            v2 = _top_values(s2, k1)
            rows = [v1[a] + v2[b] for a, b in pairs]
            rows += [jnp.full_like(rows[0], -INF)] * ((-len(rows)) % 8)
            cand = jnp.concatenate(rows, axis=0)
            best = _top_values(cand, k1)
            theta = 0.5 * (best[PEER_TOPK - 1] + best[PEER_TOPK])
            z = jnp.sum(jnp.where(cand > theta, jnp.exp(cand - best[0]), 0.0), axis=0, keepdims=True)
            c_ref[h, :, ls] = theta - s1
            a_ref[h, :, ls] = jnp.exp(s1 - v1[0]) / z
            e2_ref[h, :, ls] = jnp.exp(s2 - v2[0])


def _peer_dense_kernel(xt_ref, c_ref, a_ref, s2_ref, e2_ref, u_ref, vt_ref, o_ref, act_ref, gm_ref):
    e = pl.program_id(1)
    nk = PEER_NKEYS
    nsub = PEER_EXPERT_ROWS
    tt = xt_ref.shape[1]

    @pl.when(e == 0)
    def _():
        o_ref[...] = jnp.zeros(o_ref.shape, F32)

    def activations(rows, ls):
        act = _dot(u_ref[rows, :], xt_ref[:, ls])
        act_ref[rows, ls] = 0.5 * act * (1.0 + lax.erf(act * (2.0 ** -0.5)))

    def weights_tile(lc, rt):
        ls = slice(lc * LANES, (lc + 1) * LANES)
        rs = slice(rt * PEER_ROW_TILE, (rt + 1) * PEER_ROW_TILE)
        g = [jnp.zeros((PEER_ROW_TILE, LANES), F32)] * nsub
        for h in range(PEER_HEADS):
            s2 = s2_ref[h, rs, ls]
            e2 = e2_ref[h, rs, ls]
            for c in range(nsub):
                g[c] = g[c] + jnp.where(s2 > c_ref[h, c:c + 1, ls], e2 * a_ref[h, c:c + 1, ls], 0.0)
        for c in range(nsub):
            er = slice(c * nk + rt * PEER_ROW_TILE, c * nk + (rt + 1) * PEER_ROW_TILE)
            gm_ref[er, ls] = (g[c] * act_ref[er, ls]).astype(BF16)

    def outputs(rows, ls):
        o_ref[rows, ls] += _dot(vt_ref[rows, :], gm_ref[:, ls])

    nrt = nk // PEER_ROW_TILE
    nlc = tt // LANES
    if nlc % 2:
        activations(slice(None), slice(None))
        for lc in range(nlc):
            for rt in range(nrt):
                weights_tile(lc, rt)
        outputs(slice(None), slice(None))
        return

    halves = [slice(0, tt // 2), slice(tt // 2, tt)]
    npiece = 4
    arows = [slice(q * (nsub * nk // npiece), (q + 1) * (nsub * nk // npiece)) for q in range(npiece)]
    orows = [slice(q * (o_ref.shape[0] // npiece), (q + 1) * (o_ref.shape[0] // npiece)) for q in range(npiece)]
    tiles = [[(lc, rt) for lc in range(hf * nlc // 2, (hf + 1) * nlc // 2) for rt in range(nrt)] for hf in range(2)]
    per = len(tiles[0]) // npiece
    for q in range(npiece):
        activations(arows[q], halves[0])
    for q in range(npiece):
        activations(arows[q], halves[1])
        for lc, rt in tiles[0][q * per:(q + 1) * per]:
            weights_tile(lc, rt)
    for q in range(npiece):
        outputs(orows[q], halves[0])
        for lc, rt in tiles[1][q * per:(q + 1) * per]:
            weights_tile(lc, rt)
    for q in range(npiece):
        outputs(orows[q], halves[1])


def _peer(x1t, wqt, subkeys, u, vt, tt):
    dm, n = x1t.shape
    assert n % tt == 0 and tt % LANES == 0
    nt = n // tt
    nk = PEER_NKEYS
    nsub = PEER_EXPERT_ROWS
    hshape = (PEER_HEADS, nk, n)
    hspec = pl.BlockSpec((PEER_HEADS, nk, tt), lambda i: (0, 0, i))
    c, a, s2, e2 = pl.pallas_call(
        _peer_route_kernel,
        grid=(nt,),
        in_specs=[pl.BlockSpec((dm, tt), lambda i: (0, i)),
                  pl.BlockSpec(wqt.shape, lambda i: (0, 0)),
                  pl.BlockSpec(subkeys.shape, lambda i: (0, 0, 0, 0))],
        out_specs=[hspec] * 4,
        out_shape=[jax.ShapeDtypeStruct(hshape, F32)] * 4,
        compiler_params=_params(("parallel",)),
        name="peer_route",
    )(x1t, wqt, subkeys)

    full_spec = pl.BlockSpec((PEER_HEADS, nk, tt), lambda i, e: (0, 0, i))
    row_spec = pl.BlockSpec((PEER_HEADS, nsub, tt), lambda i, e: (0, e, i))
    return pl.pallas_call(
        _peer_dense_kernel,
        grid=(nt, nk // nsub),
        in_specs=[pl.BlockSpec((dm, tt), lambda i, e: (0, i)),
                  row_spec, row_spec, full_spec, full_spec,
                  pl.BlockSpec((nsub * nk, dm), lambda i, e: (e, 0)),
                  pl.BlockSpec((dm, nsub * nk), lambda i, e: (0, e))],
        out_specs=pl.BlockSpec((dm, tt), lambda i, e: (0, i)),
        out_shape=jax.ShapeDtypeStruct((dm, n), F32),
        scratch_shapes=[pltpu.VMEM((nsub * nk, tt), F32), pltpu.VMEM((nsub * nk, tt), BF16)],
        compiler_params=_params(("parallel", "arbitrary")),
        name="peer_dense",
    )(x1t, c, a, s2, e2, u, vt)


def _final_kernel(x1_ref, pt_ref, p_ref, wg_ref, wp_ref, g_ref, b_ref, o_ref, *, alpha):
    x1 = x1_ref[...]
    ple = jax.nn.sigmoid(_dot(x1.astype(BF16), wg_ref[...])) * _dot(p_ref[...], wp_ref[...])
    y = alpha * x1 + pt_ref[...].T + ple
    o_ref[...] = _layer_norm(y, g_ref[...], b_ref[...])


def _final(x1, peer_t, p_emb, w_gate, w_ple, g, b, tm, alpha):
    n, dm = x1.shape
    pd = p_emb.shape[1]

    def full(shape):
        return pl.BlockSpec(shape, lambda i: (0,) * len(shape))

    return pl.pallas_call(
        functools.partial(_final_kernel, alpha=alpha),
        grid=(n // tm,),
        in_specs=[pl.BlockSpec((tm, dm), lambda i: (i, 0)),
                  pl.BlockSpec((dm, tm), lambda i: (0, i)),
                  pl.BlockSpec((tm, pd), lambda i: (i, 0)),
                  full((dm, dm)), full((pd, dm)), full((1, dm)), full((1, dm))],
        out_specs=pl.BlockSpec((tm, dm), lambda i: (i, 0)),
        out_shape=jax.ShapeDtypeStruct((n, dm), F32),
        compiler_params=_params(("parallel",)),
        name="ple_ln2",
    )(x1, peer_t, p_emb, w_gate, w_ple, g, b)


def _layer(x, p_emb, pos_rows, tm, weights, attention_fn, state_in, chunk, chunk_eff, alpha):
    n = x.shape[0]
    nb = state_in.shape[0]
    t_seq = n // nb
    pr = _project(x.astype(BF16), weights["w_in"], weights["ikg"], weights["ikb"], pos_rows, tm)

    def pad_chunks(a):
        if t_seq % chunk == 0:
            return a
        a = a.reshape(nb, t_seq, a.shape[-1])
        return jnp.pad(a, ((0, 0), (0, chunk - t_seq), (0, 0))).reshape(nb * chunk, a.shape[-1])

    ret_o, s_out = _retention(pad_chunks(pr["rq"]), pad_chunks(pr["rk"]), pad_chunks(pr["rv"]),
                              pad_chunks(pr["rg"]), weights["gn_g"], weights["gn_b"],
                              state_in, chunk, chunk_eff)
    if t_seq % chunk != 0:
        ret_o = ret_o.reshape(nb, chunk, -1)[:, :t_seq].reshape(n, -1)
    att_o = attention_fn(pr)
    x1, x1t = _mix(x, ret_o, att_o, pr["sgab"], weights["w_ret_o"], weights["w_att_o"], weights["w_out"],
                   weights["ln1_g"], weights["ln1_b"], tm, alpha)
    peer_t = _peer(x1t, weights["peer_wqt"], weights["peer_subkeys"], weights["peer_u"], weights["peer_vt"],
                   min(512, n))
    y = _final(x1, peer_t, p_emb.astype(BF16), weights["w_ple_gate"], weights["w_ple"],
               weights["ln2_g"], weights["ln2_b"], tm, alpha)
    return y, pr["ak"], pr["av"], pr["ik"], s_out


def kernel(x_prompt, x_sample, cache_k, cache_v, cache_kidx, state_ret, page_table, p_prompt, p_sample,
           w_in, idx_k_g, idx_k_b, gn_g, gn_b, w_ret_o, w_att_o, w_out, ln1_g, ln1_b, peer_wq,
           peer_subkeys, peer_u, peer_v, w_ple_gate, w_ple, ln2_g, ln2_b):
    depth = w_in.shape[0]
    assert depth == 1, "single-layer trunk"
    nbp, seq, dm = x_prompt.shape
    nbs, t_new, _ = x_sample.shape
    past_len = page_table.shape[1] * cache_k.shape[2]
    alpha = (2.0 * depth) ** 0.25
    i = 0
    weights = dict(
        w_in=w_in[i].astype(BF16),
        ikg=jnp.concatenate([idx_k_g[i], idx_k_g[i]])[None, :],
        ikb=jnp.concatenate([idx_k_b[i], idx_k_b[i]])[None, :],
        gn_g=gn_g[i][None, :], gn_b=gn_b[i][None, :],
        w_ret_o=w_ret_o[i].astype(BF16), w_att_o=w_att_o[i].astype(BF16), w_out=w_out[i].astype(BF16),
        ln1_g=ln1_g[i][None, :], ln1_b=ln1_b[i][None, :],
        peer_wqt=peer_wq[i].T.astype(BF16), peer_subkeys=peer_subkeys[i].astype(BF16),
        peer_u=peer_u[i].astype(BF16), peer_vt=peer_v[i].T.astype(BF16),
        w_ple_gate=w_ple_gate[i].astype(BF16), w_ple=w_ple[i].astype(BF16),
        ln2_g=ln2_g[i][None, :], ln2_b=ln2_b[i][None, :],
    )

    def attn_p(pr):
        return _attention_prompt(pr["aq"], pr["iq"], pr["iw"], pr["akb"], pr["avb"], pr["ikb"], nbp, seq)

    yp, kp, vp, ikp, rp = _layer(
        x_prompt.reshape(nbp * seq, dm), p_prompt[i].reshape(nbp * seq, -1),
        jnp.arange(seq, dtype=jnp.int32), min(512, seq), weights, attn_p,
        jnp.zeros((nbp,) + state_ret.shape[2:], F32), RET_CHUNK, RET_CHUNK, alpha)

    def attn_s(pr):
        return _attention_sample(pr["aq"], pr["iq"], pr["iw"], pr["ak"], pr["av"], pr["ik"],
                                 cache_k[i], cache_v[i], cache_kidx[i], page_table, nbs, t_new)

    ns = nbs * t_new
    pos_s = jnp.tile(past_len + jnp.arange(t_new, dtype=jnp.int32), nbs)
    ys, ks, vs, iks, rs = _layer(
        x_sample.reshape(ns, dm), p_sample[i].reshape(ns, -1), pos_s, ns, weights, attn_s,
        state_ret[i].astype(F32), BF16_SUBLANES, t_new, alpha)

    return (yp.reshape(nbp, seq, dm), ys.reshape(nbs, t_new, dm),
            kp.reshape(1, nbp, seq, ATT_KV_HEADS, ATT_HD), vp.reshape(1, nbp, seq, ATT_KV_HEADS, ATT_HD),
            ikp.reshape(1, nbp, seq, IDX_HD), rp[None].astype(state_ret.dtype),
            ks.reshape(1, nbs, t_new, ATT_KV_HEADS, ATT_HD), vs.reshape(1, nbs, t_new, ATT_KV_HEADS, ATT_HD),
            iks.reshape(1, nbs, t_new, IDX_HD), rs[None].astype(state_ret.dtype))
```

```python
import functools

import jax
import jax.numpy as jnp
from jax import lax
from jax.experimental import pallas as pl
from jax.experimental.pallas import tpu as pltpu

F32 = jnp.float32
BF16 = jnp.bfloat16

RET_HEADS = 4
RET_DK = 256
RET_DV = 512
RET_CHUNK = 256
ATT_HEADS = 8
ATT_KV_HEADS = 2
ATT_HD = 128
IDX_HEADS = 8
IDX_HD = 64
TOPK_ATTN = 256
ROPE_THETA = 10000.0
PEER_HEADS = 8
PEER_NKEYS = 128
PEER_DKEY = 256
PEER_TOPK = 16
LN_EPS = 1e-5
NEG = -1e30
INF = float("inf")

LANES = 128
BF16_SUBLANES = 16
VMEM_LIMIT = 48 * 1024 * 1024
BISECT_ITERS = 32
SAMPLE_PAGES_PER_STEP = 16
SAMPLE_SCORE_PAGES_PER_STEP = 64


def _dot(a, b):
    return jnp.dot(a, b, preferred_element_type=F32)


def _dot_nt(a, b):
    return lax.dot_general(a, b, (((1,), (1,)), ((), ())), preferred_element_type=F32)


def _dot_tn(a, b):
    return lax.dot_general(a, b, (((0,), (0,)), ((), ())), preferred_element_type=F32)


def _params(sem):
    return pltpu.CompilerParams(dimension_semantics=sem, vmem_limit_bytes=VMEM_LIMIT)


def _rope_tables(pos, d, width):
    half = d // 2
    inv = ROPE_THETA ** (-jnp.arange(half, dtype=F32) / half)
    ang = pos.astype(F32)[:, None] * inv[None, :]
    cos, sin = jnp.cos(ang), jnp.sin(ang)
    cosf = jnp.concatenate([cos, cos], axis=-1)
    sinf = jnp.concatenate([-sin, sin], axis=-1)
    reps = width // d
    return jnp.tile(cosf, (1, reps)), jnp.tile(sinf, (1, reps))


def _partner(y, d):
    if d == 2 * LANES:
        return jnp.concatenate([y[:, LANES:], y[:, :LANES]], axis=-1)
    if d == LANES:
        return pltpu.roll(y, LANES // 2, axis=1)
    lane = lax.broadcasted_iota(jnp.int32, y.shape, 1)
    first_half = (lane & (d - 1)) < (d // 2)
    return jnp.where(first_half, pltpu.roll(y, LANES - d // 2, axis=1), pltpu.roll(y, d // 2, axis=1))


def _rope(y, cosf, sinf, d):
    return y * cosf + _partner(y, d) * sinf


def _ret_qk_kernel(x_ref, w_ref, cos_ref, sin_ref, q_ref, k_ref):
    x = x_ref[...]
    cosf, sinf = cos_ref[...], sin_ref[...]
    for h in range(2 * RET_HEADS):
        y = _dot(x, w_ref[:, h * RET_DK:(h + 1) * RET_DK])
        r = _rope(y, cosf, sinf, RET_DK)
        if h < RET_HEADS:
            q_ref[:, h * RET_DK:(h + 1) * RET_DK] = r.astype(BF16)
        else:
            hh = h - RET_HEADS
            k_ref[:, hh * RET_DK:(hh + 1) * RET_DK] = (r * (RET_DK ** -0.5)).astype(BF16)


def _plain_kernel(x_ref, w_ref, o_ref, *, sigmoid):
    y = _dot(x_ref[...], w_ref[...])
    if sigmoid:
        y = jax.nn.sigmoid(y)
    o_ref[...] = y.astype(o_ref.dtype)


def _att_proj_kernel(x_ref, wq_ref, wk_ref, wv_ref, cos_ref, sin_ref,
                     q_ref, k_ref, v_ref, kb_ref, vb_ref):
    x = x_ref[...]
    cosf, sinf = cos_ref[...], sin_ref[...]
    for h in range(ATT_HEADS):
        sl = slice(h * ATT_HD, (h + 1) * ATT_HD)
        q_ref[:, sl] = _rope(_dot(x, wq_ref[:, sl]), cosf, sinf, ATT_HD).astype(BF16)
    tm = x.shape[0]
    v = _dot(x, wv_ref[...])
    vb_ref[...] = v.astype(BF16)
    for h in range(ATT_KV_HEADS):
        sl = slice(h * ATT_HD, (h + 1) * ATT_HD)
        k = _rope(_dot(x, wk_ref[:, sl]), cosf, sinf, ATT_HD)
        kb_ref[:, sl] = k.astype(BF16)
        head_rows = pl.ds(h, tm, stride=ATT_KV_HEADS)
        k_ref[head_rows, :] = k
        v_ref[head_rows, :] = v[:, sl]


def _idx_proj_kernel(x_ref, wq_ref, wk_ref, ww_ref, cos_ref, sin_ref, g_ref, b_ref,
                     q_ref, k_ref, kb_ref, w_ref):
    x = x_ref[...]
    cosf, sinf = cos_ref[...], sin_ref[...]
    for c in range(IDX_HEADS * IDX_HD // LANES):
        sl = slice(c * LANES, (c + 1) * LANES)
        q_ref[:, sl] = _rope(_dot(x, wq_ref[:, sl]), cosf, sinf, IDX_HD).astype(BF16)
    y = _dot(x, wk_ref[...])
    mu = jnp.mean(y, axis=-1, keepdims=True)
    var = jnp.mean(jnp.square(y - mu), axis=-1, keepdims=True)
    yn = (y - mu) * lax.rsqrt(var + LN_EPS) * g_ref[...] + b_ref[...]
    k = yn * cosf + pltpu.roll(yn, IDX_HD // 2, axis=1) * sinf
    k_ref[...] = k[:, :IDX_HD]
    kb_ref[...] = k[:, :IDX_HD].astype(BF16)
    w = _dot(x, ww_ref[...])
    w_ref[...] = w[:, :IDX_HEADS] * (IDX_HEADS ** -0.5)


def _project(xb, w, ikg, ikb, pos_rows, tm):
    n, dm = xb.shape
    nt = n // tm
    rows = pos_rows.shape[0]
    tab_blocks = rows // tm
    hq = RET_HEADS * RET_DK
    hv = RET_HEADS * RET_DV
    o = 0
    w_rqk = w[:, o:o + 2 * hq]; o += 2 * hq
    w_rv = w[:, o:o + hv]; o += hv
    w_rg = w[:, o:o + hv]; o += hv
    w_aq = w[:, o:o + ATT_HEADS * ATT_HD]; o += ATT_HEADS * ATT_HD
    w_ak = w[:, o:o + ATT_KV_HEADS * ATT_HD]; o += ATT_KV_HEADS * ATT_HD
    w_av = w[:, o:o + ATT_KV_HEADS * ATT_HD]; o += ATT_KV_HEADS * ATT_HD
    w_iq = w[:, o:o + IDX_HEADS * IDX_HD]; o += IDX_HEADS * IDX_HD
    w_ik = w[:, o:o + IDX_HD]; o += IDX_HD
    w_iw = w[:, o:o + IDX_HEADS]; o += IDX_HEADS
    w_gab = w[:, o:]

    x_spec = pl.BlockSpec((tm, dm), lambda i: (i, 0))

    def tab_spec(width):
        return pl.BlockSpec((tm, width), lambda i: (i % tab_blocks, 0))

    def full(shape):
        return pl.BlockSpec(shape, lambda i: (0,) * len(shape))

    def row_spec(width):
        return pl.BlockSpec((tm, width), lambda i: (i, 0))

    cos256, sin256 = _rope_tables(pos_rows, RET_DK, RET_DK)
    rq, rk = pl.pallas_call(
        _ret_qk_kernel,
        grid=(nt,),
        in_specs=[x_spec, full((dm, 2 * hq)), tab_spec(RET_DK), tab_spec(RET_DK)],
        out_specs=[row_spec(hq), row_spec(hq)],
        out_shape=[jax.ShapeDtypeStruct((n, hq), BF16)] * 2,
        compiler_params=_params(("parallel",)),
        name="proj_ret_qk",
    )(xb, w_rqk, cos256, sin256)

    def plain(wc, dtype, sigmoid, name):
        cols = wc.shape[1]
        tn = min(1024, cols)
        return pl.pallas_call(
            functools.partial(_plain_kernel, sigmoid=sigmoid),
            grid=(cols // tn, nt),
            in_specs=[pl.BlockSpec((tm, dm), lambda j, i: (i, 0)),
                      pl.BlockSpec((dm, tn), lambda j, i: (0, j))],
            out_specs=pl.BlockSpec((tm, tn), lambda j, i: (i, j)),
            out_shape=jax.ShapeDtypeStruct((n, cols), dtype),
            compiler_params=_params(("parallel", "parallel")),
            name=name,
        )(xb, wc)

    rv = plain(w_rv, BF16, False, "proj_ret_v")
    rg = plain(w_rg, F32, False, "proj_ret_gate")
    sgab = plain(w_gab, F32, True, "proj_branch_gates")

    cos128, sin128 = _rope_tables(pos_rows, ATT_HD, ATT_HD)
    kvw = ATT_KV_HEADS * ATT_HD
    aq, ak, av, akb, avb = pl.pallas_call(
        _att_proj_kernel,
        grid=(nt,),
        in_specs=[x_spec, full((dm, ATT_HEADS * ATT_HD)), full((dm, kvw)), full((dm, kvw)),
                  tab_spec(ATT_HD), tab_spec(ATT_HD)],
        out_specs=[row_spec(ATT_HEADS * ATT_HD),
                   pl.BlockSpec((tm * ATT_KV_HEADS, ATT_HD), lambda i: (i, 0)),
                   pl.BlockSpec((tm * ATT_KV_HEADS, ATT_HD), lambda i: (i, 0)),
                   row_spec(kvw), row_spec(kvw)],
        out_shape=[jax.ShapeDtypeStruct((n, ATT_HEADS * ATT_HD), BF16),
                   jax.ShapeDtypeStruct((n * ATT_KV_HEADS, ATT_HD), F32),
                   jax.ShapeDtypeStruct((n * ATT_KV_HEADS, ATT_HD), F32),
                   jax.ShapeDtypeStruct((n, kvw), BF16), jax.ShapeDtypeStruct((n, kvw), BF16)],
        compiler_params=_params(("parallel",)),
        name="proj_att",
    )(xb, w_aq, w_ak, w_av, cos128, sin128)

    cos64, sin64 = _rope_tables(pos_rows, IDX_HD, LANES)
    w_ik2 = jnp.concatenate([w_ik, w_ik], axis=1)
    w_iw_pad = jnp.pad(w_iw, ((0, 0), (0, LANES - IDX_HEADS)))
    iq, ik, ikb, iw = pl.pallas_call(
        _idx_proj_kernel,
        grid=(nt,),
        in_specs=[x_spec, full((dm, IDX_HEADS * IDX_HD)), full((dm, LANES)), full((dm, LANES)),
                  tab_spec(LANES), tab_spec(LANES), full((1, LANES)), full((1, LANES))],
        out_specs=[row_spec(IDX_HEADS * IDX_HD), row_spec(IDX_HD), row_spec(IDX_HD), row_spec(IDX_HEADS)],
        out_shape=[jax.ShapeDtypeStruct((n, IDX_HEADS * IDX_HD), BF16),
                   jax.ShapeDtypeStruct((n, IDX_HD), F32), jax.ShapeDtypeStruct((n, IDX_HD), BF16),
                   jax.ShapeDtypeStruct((n, IDX_HEADS), F32)],
        compiler_params=_params(("parallel",)),
        name="proj_idx",
    )(xb, w_iq, w_ik2, w_iw_pad, cos64, sin64, ikg, ikb)
    return dict(rq=rq, rk=rk, rv=rv, rg=rg, sgab=sgab, aq=aq, ak=ak, av=av, akb=akb, avb=avb,
                iq=iq, ik=ik, ikb=ikb, iw=iw)


def _retention_kernel(sdec_ref, q_ref, k_ref, v_ref, gate_ref, gg_ref, gb_ref,
                      d_ref, cdec_ref, kdec_ref, si_ref, o_ref, so_ref):
    h = pl.program_id(1)
    c = pl.program_id(2)

    @pl.when(c == 0)
    def _():
        so_ref[0, 0] = si_ref[0, 0]

    q = q_ref[...]
    k = k_ref[...]
    v = v_ref[...]
    state = so_ref[0, 0]
    s = _dot_nt(q, k)
    inner = _dot((s * d_ref[0]).astype(BF16), v)
    cross = _dot(q, state.astype(BF16)) * cdec_ref[0]
    o = inner + cross
    kd = (k.astype(F32) * kdec_ref[0]).astype(BF16)
    so_ref[0, 0] = state * sdec_ref[h] + _dot_tn(kd, v)
    mu = jnp.mean(o, axis=-1, keepdims=True)
    var = jnp.mean(jnp.square(o - mu), axis=-1, keepdims=True)
    on = (o - mu) * lax.rsqrt(var + LN_EPS) * gg_ref[...] + gb_ref[...]
    o_ref[...] = (jax.nn.silu(gate_ref[...]) * on).astype(o_ref.dtype)


def _retention(rq, rk, rv, rg, gn_g, gn_b, state_in, chunk, chunk_eff):
    n = rq.shape[0]
    nb = state_in.shape[0]
    nc = n // (nb * chunk)
    lg = jnp.log1p(-jnp.exp2(-5.0 - jnp.arange(RET_HEADS, dtype=F32)))[:, None]
    i = jnp.arange(chunk, dtype=F32)
    diff = i[:, None] - i[None, :]
    decay = jnp.where(diff >= 0, jnp.exp(lg[:, :, None] * jnp.maximum(diff, 0.0)), 0.0)
    cdec = jnp.exp(lg * (i + 1.0))[:, :, None]
    kdec = jnp.where(i < chunk_eff, jnp.exp(lg * jnp.maximum(chunk_eff - 1.0 - i, 0.0)), 0.0)[:, :, None]
    sdec = jnp.exp(lg[:, 0] * chunk_eff)

    o, s_out = pl.pallas_call(
        _retention_kernel,
        grid=(nb, RET_HEADS, nc),
        in_specs=[
            pl.BlockSpec(memory_space=pltpu.SMEM),
            pl.BlockSpec((chunk, RET_DK), lambda b, h, c: (b * nc + c, h)),
            pl.BlockSpec((chunk, RET_DK), lambda b, h, c: (b * nc + c, h)),
            pl.BlockSpec((chunk, RET_DV), lambda b, h, c: (b * nc + c, h)),
            pl.BlockSpec((chunk, RET_DV), lambda b, h, c: (b * nc + c, h)),
            pl.BlockSpec((1, RET_DV), lambda b, h, c: (0, h)),
            pl.BlockSpec((1, RET_DV), lambda b, h, c: (0, h)),
            pl.BlockSpec((1, chunk, chunk), lambda b, h, c: (h, 0, 0)),
            pl.BlockSpec((1, chunk, 1), lambda b, h, c: (h, 0, 0)),
            pl.BlockSpec((1, chunk, 1), lambda b, h, c: (h, 0, 0)),
            pl.BlockSpec((1, 1, RET_DK, RET_DV), lambda b, h, c: (b, h, 0, 0)),
        ],
        out_specs=[
            pl.BlockSpec((chunk, RET_DV), lambda b, h, c: (b * nc + c, h)),
            pl.BlockSpec((1, 1, RET_DK, RET_DV), lambda b, h, c: (b, h, 0, 0)),
        ],
        out_shape=[jax.ShapeDtypeStruct((n, RET_HEADS * RET_DV), BF16),
                   jax.ShapeDtypeStruct(state_in.shape, F32)],
        compiler_params=_params(("parallel", "parallel", "arbitrary")),
        name="retention",
    )(sdec, rq, rk, rv, rg, gn_g, gn_b, decay, cdec, kdec, state_in)
    return o, s_out


M_INIT = -1e29


def _lanes(col):
    return jnp.broadcast_to(col, (col.shape[0], LANES))


def _kth_largest(reduce_tiles, mn, mx, kq):
    kq_b = _lanes(kq)

    def bisect_body(_, carry):
        lo, hi = carry
        mid = 0.5 * (lo + hi)
        cnt = reduce_tiles(lambda a, t: a + jnp.where(t >= mid, 1.0, 0.0), 0.0)
        ge = _lanes(jnp.sum(cnt, axis=-1, keepdims=True)) >= kq_b
        return jnp.where(ge, mid, lo), jnp.where(ge, hi, mid)

    lo, _ = lax.fori_loop(0, BISECT_ITERS, bisect_body, (_lanes(mn), _lanes(mx + 1.0)))
    tau = jnp.min(reduce_tiles(lambda a, t: jnp.minimum(a, jnp.where(t >= lo, t, INF)), INF),
                  axis=-1, keepdims=True)
    tau_b = _lanes(tau)
    cgt = jnp.sum(reduce_tiles(lambda a, t: a + jnp.where(t > tau_b, 1.0, 0.0), 0.0),
                  axis=-1, keepdims=True)
    return tau, kq - cgt


def _softmax_step(s, m_ref, l_ref, acc_ref, idx, v):
    m_old = m_ref[idx]
    m_new = jnp.maximum(m_old, jnp.max(s, axis=-1, keepdims=True))
    p = jnp.exp(s - m_new)
    alpha = jnp.exp(m_old - m_new)
    l_ref[idx] = alpha * l_ref[idx] + jnp.sum(p, axis=-1, keepdims=True)
    acc_ref[idx] = alpha * acc_ref[idx] + _dot(p.astype(BF16), v)
    m_ref[idx] = m_new


def _fold_rows(x):
    return jnp.sum(x.reshape(x.shape[0] // 8, 8, x.shape[1]), axis=0)


def _attn_prompt_kernel(aqt_ref, iqt_ref, iwt_ref, k_ref, vt_ref, kidx_ref, tril_ref, o_ref,
                        sc_ref, m_ref, l_ref, acc_ref, *, tq, topk):
    i = pl.program_id(1)
    nkv = i + 1
    qpos = i * tq + lax.broadcasted_iota(jnp.int32, (1, tq), 1)
    kq = jnp.minimum(topk, qpos + 1).astype(F32)
    group = ATT_HEADS // ATT_KV_HEADS

    def score_body(j, carry):
        mn, mx = carry
        kx = kidx_ref[pl.ds(pl.multiple_of(j * tq, tq), tq), :]
        s = jnp.zeros((tq, tq), F32)
        for h in range(IDX_HEADS):
            d = _dot(kx, iqt_ref[h * IDX_HD:(h + 1) * IDX_HD, :]) * (IDX_HD ** -0.5)
            s = s + iwt_ref[h:h + 1, :] * jnp.maximum(d, 0.0)
        kpos = j * tq + lax.broadcasted_iota(jnp.int32, (tq, tq), 0)
        vis = kpos <= qpos
        sc_ref[j] = jnp.where(vis, s, -INF)
        mn = jnp.minimum(mn, jnp.min(jnp.where(vis, s, INF), axis=0, keepdims=True))
        mx = jnp.maximum(mx, jnp.max(jnp.where(vis, s, -INF), axis=0, keepdims=True))
        return mn, mx

    mn, mx = lax.fori_loop(0, nkv, score_body,
                           (jnp.full((1, tq), INF, F32), jnp.full((1, tq), -INF, F32)))

    def count(pred):
        def body(j, acc):
            return acc + _fold_rows(jnp.where(pred(sc_ref[j]), 1.0, 0.0))
        return jnp.sum(lax.fori_loop(0, nkv, body, jnp.zeros((8, tq), F32)), axis=0, keepdims=True)

    def bisect_body(_, carry):
        lo, hi = carry
        mid = 0.5 * (lo + hi)
        ge = count(lambda t: t >= mid) >= kq
        return jnp.where(ge, mid, lo), jnp.where(ge, hi, mid)

    lo, _ = lax.fori_loop(0, BISECT_ITERS, bisect_body, (mn, mx + 1.0))

    def snap_body(j, acc):
        t = sc_ref[j]
        return jnp.minimum(acc, jnp.min(jnp.where(t >= lo, t, INF), axis=0, keepdims=True))

    tau = lax.fori_loop(0, nkv, snap_body, jnp.full((1, tq), INF, F32))
    need = kq - count(lambda t: t > tau)

    tril = tril_ref[...]

    def mask_body(j, run):
        t = sc_ref[j]
        eqf = jnp.where(t == tau, 1.0, 0.0)
        prefix = _dot(tril, eqf.astype(BF16))
        sel = (t > tau) | ((t == tau) & ((run + prefix) <= need))
        sc_ref[j] = jnp.where(sel, 0.0, NEG)
        return run + jnp.sum(eqf, axis=0, keepdims=True)

    lax.fori_loop(0, nkv, mask_body, jnp.zeros((1, tq), F32))

    m_ref[...] = jnp.full(m_ref.shape, M_INIT, F32)
    l_ref[...] = jnp.zeros(l_ref.shape, F32)
    acc_ref[...] = jnp.zeros(acc_ref.shape, F32)

    def att_body(j, carry):
        rows = pl.ds(pl.multiple_of(j * tq, tq), tq)
        for h in range(ATT_HEADS):
            g = h // group
            kg = k_ref[rows, g * ATT_HD:(g + 1) * ATT_HD]
            vg = vt_ref[0, j, g * ATT_HD:(g + 1) * ATT_HD, :]
            for c in range(tq // LANES):
                ls = slice(c * LANES, (c + 1) * LANES)
                s = _dot(kg, aqt_ref[h * ATT_HD:(h + 1) * ATT_HD, ls]) * (ATT_HD ** -0.5) + sc_ref[j, :, ls]
                m_old = m_ref[h, :, ls]
                m_new = jnp.maximum(m_old, jnp.max(s, axis=0, keepdims=True))
                p = jnp.exp(s - m_new)
                alpha = jnp.exp(m_old - m_new)
                l_ref[h, :, ls] = alpha * l_ref[h, :, ls] + jnp.sum(p, axis=0, keepdims=True)
                acc_ref[h, :, ls] = alpha * acc_ref[h, :, ls] + _dot(vg, p.astype(BF16))
                m_ref[h, :, ls] = m_new
        return carry

    lax.fori_loop(0, nkv, att_body, 0)
    for h in range(ATT_HEADS):
        o_ref[h * ATT_HD:(h + 1) * ATT_HD, :] = (acc_ref[h] / l_ref[h]).astype(o_ref.dtype)


def _attention_prompt(aq, iq, iw, akb, avb, ikb, nb, seq):
    tq = min(256, seq)
    assert seq % tq == 0 and tq % LANES == 0
    nq = seq // tq
    topk = min(TOPK_ATTN, seq // 4)
    n = aq.shape[0]
    tril = jnp.tril(jnp.ones((tq, tq), F32)).astype(BF16)
    kvw = ATT_KV_HEADS * ATT_HD
    vt = avb.reshape(nb, nq, tq, kvw).transpose(0, 1, 3, 2)
    ot = pl.pallas_call(
        functools.partial(_attn_prompt_kernel, tq=tq, topk=topk),
        grid=(nb, nq),
        in_specs=[
            pl.BlockSpec((ATT_HEADS * ATT_HD, tq), lambda b, i: (0, b * nq + i)),
            pl.BlockSpec((IDX_HEADS * IDX_HD, tq), lambda b, i: (0, b * nq + i)),
            pl.BlockSpec((IDX_HEADS, tq), lambda b, i: (0, b * nq + i)),
            pl.BlockSpec((seq, kvw), lambda b, i: (b, 0)),
            pl.BlockSpec((1, nq, kvw, tq), lambda b, i: (b, 0, 0, 0)),
            pl.BlockSpec((seq, IDX_HD), lambda b, i: (b, 0)),
            pl.BlockSpec((tq, tq), lambda b, i: (0, 0)),
        ],
        out_specs=pl.BlockSpec((ATT_HEADS * ATT_HD, tq), lambda b, i: (0, b * nq + i)),
        out_shape=jax.ShapeDtypeStruct((ATT_HEADS * ATT_HD, n), BF16),
        scratch_shapes=[
            pltpu.VMEM((nq, tq, tq), F32),
            pltpu.VMEM((ATT_HEADS, 1, tq), F32),
            pltpu.VMEM((ATT_HEADS, 1, tq), F32),
            pltpu.VMEM((ATT_HEADS, ATT_HD, tq), F32),
        ],
        compiler_params=_params(("parallel", "arbitrary")),
        name="attention_prompt",
    )(aq.T, iq.T, iw.T, akb, vt, ikb, tril)
    return ot.T


def _page_scores(iq_ref, iw_ref, kxt):
    tp = iq_ref.shape[1] // IDX_HEADS
    d = _dot(iq_ref[0], kxt) * (IDX_HD ** -0.5)
    d = iw_ref[0] * jnp.maximum(d, 0.0)
    s = d[:tp]
    for h in range(1, IDX_HEADS):
        s = s + d[h * tp:(h + 1) * tp]
    return s


def _sample_scores_kernel(pt_ref, iq_ref, iw_ref, *refs, npp, n_new, topk):
    page_refs = refs[:npp]
    knew_ref = refs[npp]
    sc_ref, tau_ref, need_ref = refs[npp + 1:]
    s = pl.program_id(1)
    npages = sc_ref.shape[1] - 1
    tp, page = sc_ref.shape[2], sc_ref.shape[3]

    kxt = jnp.concatenate([r[0] for r in page_refs], axis=1).astype(BF16)
    sc = _page_scores(iq_ref, iw_ref, kxt)
    for p in range(npp):
        sc_ref[0, s * npp + p] = sc[:, p * page:(p + 1) * page]

    @pl.when(s == pl.num_programs(1) - 1)
    def _():
        snew = _page_scores(iq_ref, iw_ref, knew_ref[0].astype(BF16))
        t = jnp.minimum(lax.broadcasted_iota(jnp.int32, (tp, page), 0), n_new - 1)
        j = lax.broadcasted_iota(jnp.int32, (tp, page), 1)
        sc_ref[0, npages] = jnp.where(j <= t, snew, -INF)

        def reduce_tiles(fn, init):
            def body(c, acc):
                for p in range(npp):
                    acc = fn(acc, sc_ref[0, c * npp + p])
                return acc
            acc = lax.fori_loop(0, npages // npp, body, jnp.full((tp, page), init, F32))
            return fn(acc, sc_ref[0, npages])

        mn = jnp.min(reduce_tiles(lambda a, x: jnp.minimum(a, jnp.where(x == -INF, INF, x)), INF),
                     axis=-1, keepdims=True)
        mx = jnp.max(reduce_tiles(jnp.maximum, -INF), axis=-1, keepdims=True)
        tau, need = _kth_largest(reduce_tiles, mn, mx, jnp.full((tp, 1), float(topk), F32))
        tau_ref[0] = _lanes(tau)
        need_ref[0] = _lanes(need)


def _sample_attend_kernel(pt_ref, q_ref, sc_ref, tau_ref, need_ref, tri_ref, *refs, npp):
    k_refs = refs[:npp]
    v_refs = refs[npp:2 * npp]
    knew_ref, vnew_ref, o_ref, m_ref, l_ref, acc_ref, run_ref = refs[2 * npp:]
    s = pl.program_id(1)
    npages = sc_ref.shape[1] - 1
    page = sc_ref.shape[3]
    group = ATT_HEADS // ATT_KV_HEADS
    grows = q_ref.shape[1] // ATT_KV_HEADS

    @pl.when(s == 0)
    def _():
        m_ref[...] = jnp.full(m_ref.shape, M_INIT, F32)
        l_ref[...] = jnp.zeros(l_ref.shape, F32)
        acc_ref[...] = jnp.zeros(acc_ref.shape, F32)
        run_ref[...] = jnp.zeros(run_ref.shape, F32)

    tau = tau_ref[0][:, :1]
    need = need_ref[0][:, :1]
    tri = tri_ref[...]

    def attend(pages, krefs, vrefs):
        tp = sc_ref.shape[2]
        scs = [sc_ref[0, pg] for pg in pages]
        eqs = [jnp.where(t == tau, 1.0, 0.0) for t in scs]
        prefix = _dot(jnp.concatenate(eqs, axis=0).astype(BF16), tri)
        run = run_ref[...]
        madds = []
        for k, t in enumerate(scs):
            sel = (t > tau) | ((t == tau) & ((run + prefix[k * tp:(k + 1) * tp]) <= need))
            madds.append(jnp.where(sel, 0.0, NEG))
            run = run + jnp.sum(eqs[k], axis=-1, keepdims=True)
        run_ref[...] = run
        madd = jnp.concatenate(madds, axis=1)
        maddg = jnp.concatenate([madd] * group, axis=0)
        for g in range(ATT_KV_HEADS):
            head_rows = pl.ds(g, page, stride=ATT_KV_HEADS)
            kg = jnp.concatenate([r[0, head_rows, :] for r in krefs], axis=0).astype(BF16)
            vg = jnp.concatenate([r[0, head_rows, :] for r in vrefs], axis=0).astype(BF16)
            sg = _dot_nt(q_ref[0, g * grows:(g + 1) * grows, :], kg) * (ATT_HD ** -0.5) + maddg
            _softmax_step(sg, m_ref, l_ref, acc_ref, g, vg)

    attend([s * npp + p for p in range(npp)], k_refs, v_refs)

    @pl.when(s == pl.num_programs(1) - 1)
    def _():
        attend([npages], [knew_ref], [vnew_ref])
        for g in range(ATT_KV_HEADS):
            o_ref[0, g * grows:(g + 1) * grows, :] = acc_ref[g] / l_ref[g]


def _attention_sample(aq, iq, iw, ak, av, ik, cache_k, cache_v, cache_kidx, page_table, nb, t_new):
    n_phys, page = cache_kidx.shape[0], cache_kidx.shape[1]
    npages = page_table.shape[1]
    npp = SAMPLE_PAGES_PER_STEP if npages % SAMPLE_PAGES_PER_STEP == 0 else 1
    nsteps = npages // npp
    npp_sc = SAMPLE_SCORE_PAGES_PER_STEP if npages % SAMPLE_SCORE_PAGES_PER_STEP == 0 else npp
    tp = BF16_SUBLANES
    topk = min(TOPK_ATTN, (npages * page + t_new) // 4)
    prow = page * ATT_KV_HEADS

    def pad_rows(a, per_seq, rows):
        a = a.reshape(nb, per_seq, a.shape[-1])
        return jnp.pad(a, ((0, 0), (0, rows - per_seq), (0, 0)))

    iq_p = pad_rows(iq, t_new, tp).reshape(nb, tp, IDX_HEADS, IDX_HD).transpose(0, 2, 1, 3)
    iq_p = iq_p.reshape(nb, IDX_HEADS * tp, IDX_HD)
    iw_p = pad_rows(iw, t_new, tp).transpose(0, 2, 1).reshape(nb, IDX_HEADS * tp, 1)
    q_p = pad_rows(aq, t_new, tp).reshape(nb, tp, ATT_HEADS, ATT_HD).transpose(0, 2, 1, 3)
    q_p = q_p.reshape(nb, ATT_HEADS * tp, ATT_HD)
    knew = pad_rows(ak, t_new * ATT_KV_HEADS, prow)
    vnew = pad_rows(av, t_new * ATT_KV_HEADS, prow)
    kinew = pad_rows(ik, t_new, page).swapaxes(1, 2)
    kidx_t = cache_kidx.swapaxes(1, 2)
    ck = cache_k.reshape(n_phys, prow, ATT_HD)
    cv = cache_v.reshape(n_phys, prow, ATT_HD)
    tri = jnp.triu(jnp.ones((page, page), F32)).astype(BF16)

    def page_spec(rows, width, p, per_step=npp):
        return pl.BlockSpec((1, rows, width), lambda b, s, pt, p=p: (pt[b, s * per_step + p], 0, 0))

    def seq_spec(shape):
        return pl.BlockSpec((1,) + shape, lambda b, s, pt: (b,) + (0,) * len(shape))

    sc, tau, need = pl.pallas_call(
        functools.partial(_sample_scores_kernel, npp=npp_sc, n_new=t_new, topk=topk),
        grid_spec=pltpu.PrefetchScalarGridSpec(
            num_scalar_prefetch=1,
            grid=(nb, npages // npp_sc),
            in_specs=[seq_spec((IDX_HEADS * tp, IDX_HD)), seq_spec((IDX_HEADS * tp, 1))]
                     + [page_spec(IDX_HD, page, p, npp_sc) for p in range(npp_sc)] + [seq_spec((IDX_HD, page))],
            out_specs=[seq_spec((npages + 1, tp, page)), seq_spec((tp, page)), seq_spec((tp, page))],
        ),
        out_shape=[jax.ShapeDtypeStruct((nb, npages + 1, tp, page), F32),
                   jax.ShapeDtypeStruct((nb, tp, page), F32),
                   jax.ShapeDtypeStruct((nb, tp, page), F32)],
        compiler_params=_params(("parallel", "arbitrary")),
        name="sample_scores",
    )(page_table, iq_p, iw_p, *([kidx_t] * npp_sc), kinew)

    grows = (ATT_HEADS // ATT_KV_HEADS) * tp
    o = pl.pallas_call(
        functools.partial(_sample_attend_kernel, npp=npp),
        grid_spec=pltpu.PrefetchScalarGridSpec(
            num_scalar_prefetch=1,
            grid=(nb, nsteps),
            in_specs=[seq_spec((ATT_HEADS * tp, ATT_HD)), seq_spec((npages + 1, tp, page)),
                      seq_spec((tp, page)), seq_spec((tp, page)),
                      pl.BlockSpec((page, page), lambda b, s, pt: (0, 0))]
                     + [page_spec(prow, ATT_HD, p) for p in range(npp)]
                     + [page_spec(prow, ATT_HD, p) for p in range(npp)]
                     + [seq_spec((prow, ATT_HD)), seq_spec((prow, ATT_HD))],
            out_specs=seq_spec((ATT_HEADS * tp, ATT_HD)),
            scratch_shapes=[
                pltpu.VMEM((ATT_KV_HEADS, grows, 1), F32),
                pltpu.VMEM((ATT_KV_HEADS, grows, 1), F32),
                pltpu.VMEM((ATT_KV_HEADS, grows, ATT_HD), F32),
                pltpu.VMEM((tp, 1), F32),
            ],
        ),
        out_shape=jax.ShapeDtypeStruct((nb, ATT_HEADS * tp, ATT_HD), F32),
        compiler_params=_params(("parallel", "arbitrary")),
        name="sample_attend",
    )(page_table, q_p, sc, tau, need, tri, *([ck] * npp), *([cv] * npp), knew, vnew)
    o = o.reshape(nb, ATT_HEADS, tp, ATT_HD)[:, :, :t_new].transpose(0, 2, 1, 3)
    return o.reshape(nb * t_new, ATT_HEADS * ATT_HD).astype(BF16)


def _layer_norm(y, g, b):
    mu = jnp.mean(y, axis=-1, keepdims=True)
    var = jnp.mean(jnp.square(y - mu), axis=-1, keepdims=True)
    return (y - mu) * lax.rsqrt(var + LN_EPS) * g + b


def _mix_kernel(x_ref, ret_ref, att_ref, sga_ref, sgb_ref, wr_ref, wa_ref, wo_ref, g_ref, b_ref,
                o_ref, ot_ref, *, alpha):
    branch = sga_ref[...] * _dot(ret_ref[...], wr_ref[...]) + sgb_ref[...] * _dot(att_ref[...], wa_ref[...])
    y = alpha * x_ref[...] + _dot(branch.astype(BF16), wo_ref[...])
    x1 = _layer_norm(y, g_ref[...], b_ref[...])
    o_ref[...] = x1
    ot_ref[...] = x1.T.astype(BF16)


def _mix(x, ret_o, att_o, sgab, w_ret_o, w_att_o, w_out, g, b, tm, alpha):
    n, dm = x.shape
    hv = RET_HEADS * RET_DV
    ha = ATT_HEADS * ATT_HD

    def full(shape):
        return pl.BlockSpec(shape, lambda i: (0,) * len(shape))

    return pl.pallas_call(
        functools.partial(_mix_kernel, alpha=alpha),
        grid=(n // tm,),
        in_specs=[pl.BlockSpec((tm, dm), lambda i: (i, 0)),
                  pl.BlockSpec((tm, hv), lambda i: (i, 0)),
                  pl.BlockSpec((tm, ha), lambda i: (i, 0)),
                  pl.BlockSpec((tm, dm), lambda i: (i, 0)),
                  pl.BlockSpec((tm, dm), lambda i: (i, 1)),
                  full((hv, dm)), full((ha, dm)), full((dm, dm)), full((1, dm)), full((1, dm))],
        out_specs=[pl.BlockSpec((tm, dm), lambda i: (i, 0)),
                   pl.BlockSpec((dm, tm), lambda i: (0, i))],
        out_shape=[jax.ShapeDtypeStruct((n, dm), F32), jax.ShapeDtypeStruct((dm, n), BF16)],
        compiler_params=_params(("parallel",)),
        name="branch_mix_ln1",
    )(x, ret_o, att_o, sgab, sgab, w_ret_o, w_att_o, w_out, g, b)


PEER_EXPERT_ROWS = 8
PEER_ROW_TILE = 32


def _top_values(cur, k):
    rows = []
    for _ in range(k):
        m = jnp.max(cur, axis=0, keepdims=True)
        rows.append(m)
        cur = jnp.where(cur == m, -INF, cur)
    return rows


def _peer_route_kernel(xt_ref, wqt_ref, sk_ref, c_ref, a_ref, s2_ref, e2_ref):
    xt = xt_ref[...]
    half = PEER_DKEY // 2
    k1 = PEER_TOPK + 1
    pairs = [(a, b) for a in range(k1) for b in range(k1 // (a + 1))]
    for h in range(PEER_HEADS):
        qt = _dot(wqt_ref[h * PEER_DKEY:(h + 1) * PEER_DKEY, :], xt).astype(BF16)
        s1_ref = c_ref.at[h]
        s1_ref[...] = _dot(sk_ref[h, 0], qt[:half])
        s2_ref[h] = _dot(sk_ref[h, 1], qt[half:])
        for lc in range(xt.shape[1] // LANES):
            ls = slice(lc * LANES, (lc + 1) * LANES)
            s1 = s1_ref[:, ls]
            s2 = s2_ref[h, :, ls]
            v1 = _top_values(s1, k1)
            v2 = _top_values(s2, k1)
            rows = [v1[a] + v2[b] for a, b in pairs]
            rows += [jnp.full_like(rows[0], -INF)] * ((-len(rows)) % 8)
            cand = jnp.concatenate(rows, axis=0)
            best = _top_values(cand, k1)
            theta = 0.5 * (best[PEER_TOPK - 1] + best[PEER_TOPK])
            z = jnp.sum(jnp.where(cand > theta, jnp.exp(cand - best[0]), 0.0), axis=0, keepdims=True)
            c_ref[h, :, ls] = theta - s1
            a_ref[h, :, ls] = jnp.exp(s1 - v1[0]) / z
            e2_ref[h, :, ls] = jnp.exp(s2 - v2[0])


def _peer_dense_kernel(xt_ref, c_ref, a_ref, s2_ref, e2_ref, u_ref, vt_ref, o_ref, act_ref, gm_ref):
    e = pl.program_id(1)
    nk = PEER_NKEYS
    nsub = PEER_EXPERT_ROWS
    tt = xt_ref.shape[1]

    @pl.when(e == 0)
    def _():
        o_ref[...] = jnp.zeros(o_ref.shape, F32)

    def activations(rows, ls):
        act = _dot(u_ref[rows, :], xt_ref[:, ls])
        act_ref[rows, ls] = 0.5 * act * (1.0 + lax.erf(act * (2.0 ** -0.5)))

    def weights_tile(lc, rt):
        ls = slice(lc * LANES, (lc + 1) * LANES)
        rs = slice(rt * PEER_ROW_TILE, (rt + 1) * PEER_ROW_TILE)
        g = [jnp.zeros((PEER_ROW_TILE, LANES), F32)] * nsub
        for h in range(PEER_HEADS):
            s2 = s2_ref[h, rs, ls]
            e2 = e2_ref[h, rs, ls]
            for c in range(nsub):
                g[c] = g[c] + jnp.where(s2 > c_ref[h, c:c + 1, ls], e2 * a_ref[h, c:c + 1, ls], 0.0)
        for c in range(nsub):
            er = slice(c * nk + rt * PEER_ROW_TILE, c * nk + (rt + 1) * PEER_ROW_TILE)
            gm_ref[er, ls] = (g[c] * act_ref[er, ls]).astype(BF16)

    def outputs(rows, ls):
        o_ref[rows, ls] += _dot(vt_ref[rows, :], gm_ref[:, ls])

    nrt = nk // PEER_ROW_TILE
    nlc = tt // LANES
    if nlc % 2:
        activations(slice(None), slice(None))
        for lc in range(nlc):
            for rt in range(nrt):
                weights_tile(lc, rt)
        outputs(slice(None), slice(None))
        return

    halves = [slice(0, tt // 2), slice(tt // 2, tt)]
    npiece = 4
    arows = [slice(q * (nsub * nk // npiece), (q + 1) * (nsub * nk // npiece)) for q in range(npiece)]
    orows = [slice(q * (o_ref.shape[0] // npiece), (q + 1) * (o_ref.shape[0] // npiece)) for q in range(npiece)]
    tiles = [[(lc, rt) for lc in range(hf * nlc // 2, (hf + 1) * nlc // 2) for rt in range(nrt)] for hf in range(2)]
    per = len(tiles[0]) // npiece
    for q in range(npiece):
        activations(arows[q], halves[0])
    for q in range(npiece):
        activations(arows[q], halves[1])
        for lc, rt in tiles[0][q * per:(q + 1) * per]:
            weights_tile(lc, rt)
    for q in range(npiece):
        outputs(orows[q], halves[0])
        for lc, rt in tiles[1][q * per:(q + 1) * per]:
            weights_tile(lc, rt)
    for q in range(npiece):
        outputs(orows[q], halves[1])


def _peer(x1t, wqt, subkeys, u, vt, tt):
    dm, n = x1t.shape
    assert n % tt == 0 and tt % LANES == 0
    nt = n // tt
    nk = PEER_NKEYS
    nsub = PEER_EXPERT_ROWS
    hshape = (PEER_HEADS, nk, n)
    hspec = pl.BlockSpec((PEER_HEADS, nk, tt), lambda i: (0, 0, i))
    c, a, s2, e2 = pl.pallas_call(
        _peer_route_kernel,
        grid=(nt,),
        in_specs=[pl.BlockSpec((dm, tt), lambda i: (0, i)),
                  pl.BlockSpec(wqt.shape, lambda i: (0, 0)),
                  pl.BlockSpec(subkeys.shape, lambda i: (0, 0, 0, 0))],
        out_specs=[hspec] * 4,
        out_shape=[jax.ShapeDtypeStruct(hshape, F32)] * 4,
        compiler_params=_params(("parallel",)),
        name="peer_route",
    )(x1t, wqt, subkeys)

    full_spec = pl.BlockSpec((PEER_HEADS, nk, tt), lambda i, e: (0, 0, i))
    row_spec = pl.BlockSpec((PEER_HEADS, nsub, tt), lambda i, e: (0, e, i))
    return pl.pallas_call(
        _peer_dense_kernel,
        grid=(nt, nk // nsub),
        in_specs=[pl.BlockSpec((dm, tt), lambda i, e: (0, i)),
                  row_spec, row_spec, full_spec, full_spec,
                  pl.BlockSpec((nsub * nk, dm), lambda i, e: (e, 0)),
                  pl.BlockSpec((dm, nsub * nk), lambda i, e: (0, e))],
        out_specs=pl.BlockSpec((dm, tt), lambda i, e: (0, i)),
        out_shape=jax.ShapeDtypeStruct((dm, n), F32),
        scratch_shapes=[pltpu.VMEM((nsub * nk, tt), F32), pltpu.VMEM((nsub * nk, tt), BF16)],
        compiler_params=_params(("parallel", "arbitrary")),
        name="peer_dense",
    )(x1t, c, a, s2, e2, u, vt)


def _final_kernel(x1_ref, pt_ref, p_ref, wg_ref, wp_ref, g_ref, b_ref, o_ref, *, alpha):
    x1 = x1_ref[...]
    ple = jax.nn.sigmoid(_dot(x1.astype(BF16), wg_ref[...])) * _dot(p_ref[...], wp_ref[...])
    y = alpha * x1 + pt_ref[...].T + ple
    o_ref[...] = _layer_norm(y, g_ref[...], b_ref[...])


def _final(x1, peer_t, p_emb, w_gate, w_ple, g, b, tm, alpha):
    n, dm = x1.shape
    pd = p_emb.shape[1]

    def full(shape):
        return pl.BlockSpec(shape, lambda i: (0,) * len(shape))

    return pl.pallas_call(
        functools.partial(_final_kernel, alpha=alpha),
        grid=(n // tm,),
        in_specs=[pl.BlockSpec((tm, dm), lambda i: (i, 0)),
                  pl.BlockSpec((dm, tm), lambda i: (0, i)),
                  pl.BlockSpec((tm, pd), lambda i: (i, 0)),
                  full((dm, dm)), full((pd, dm)), full((1, dm)), full((1, dm))],
        out_specs=pl.BlockSpec((tm, dm), lambda i: (i, 0)),
        out_shape=jax.ShapeDtypeStruct((n, dm), F32),
        compiler_params=_params(("parallel",)),
        name="ple_ln2",
    )(x1, peer_t, p_emb, w_gate, w_ple, g, b)


def _layer(x, p_emb, pos_rows, tm, weights, attention_fn, state_in, chunk, chunk_eff, alpha):
    n = x.shape[0]
    nb = state_in.shape[0]
    t_seq = n // nb
    pr = _project(x.astype(BF16), weights["w_in"], weights["ikg"], weights["ikb"], pos_rows, tm)

    def pad_chunks(a):
        if t_seq % chunk == 0:
            return a
        a = a.reshape(nb, t_seq, a.shape[-1])
        return jnp.pad(a, ((0, 0), (0, chunk - t_seq), (0, 0))).reshape(nb * chunk, a.shape[-1])

    ret_o, s_out = _retention(pad_chunks(pr["rq"]), pad_chunks(pr["rk"]), pad_chunks(pr["rv"]),
                              pad_chunks(pr["rg"]), weights["gn_g"], weights["gn_b"],
                              state_in, chunk, chunk_eff)
    if t_seq % chunk != 0:
        ret_o = ret_o.reshape(nb, chunk, -1)[:, :t_seq].reshape(n, -1)
    att_o = attention_fn(pr)
    x1, x1t = _mix(x, ret_o, att_o, pr["sgab"], weights["w_ret_o"], weights["w_att_o"], weights["w_out"],
                   weights["ln1_g"], weights["ln1_b"], tm, alpha)
    peer_t = _peer(x1t, weights["peer_wqt"], weights["peer_subkeys"], weights["peer_u"], weights["peer_vt"],
                   min(512, n))
    y = _final(x1, peer_t, p_emb.astype(BF16), weights["w_ple_gate"], weights["w_ple"],
               weights["ln2_g"], weights["ln2_b"], tm, alpha)
    return y, pr["ak"], pr["av"], pr["ik"], s_out


def kernel(x_prompt, x_sample, cache_k, cache_v, cache_kidx, state_ret, page_table, p_prompt, p_sample,
           w_in, idx_k_g, idx_k_b, gn_g, gn_b, w_ret_o, w_att_o, w_out, ln1_g, ln1_b, peer_wq,
           peer_subkeys, peer_u, peer_v, w_ple_gate, w_ple, ln2_g, ln2_b):
    depth = w_in.shape[0]
    assert depth == 1, "single-layer trunk"
    nbp, seq, dm = x_prompt.shape
    nbs, t_new, _ = x_sample.shape
    past_len = page_table.shape[1] * cache_k.shape[2]
    alpha = (2.0 * depth) ** 0.25
    i = 0
    weights = dict(
        w_in=w_in[i].astype(BF16),
        ikg=jnp.concatenate([idx_k_g[i], idx_k_g[i]])[None, :],
        ikb=jnp.concatenate([idx_k_b[i], idx_k_b[i]])[None, :],
        gn_g=gn_g[i][None, :], gn_b=gn_b[i][None, :],
        w_ret_o=w_ret_o[i].astype(BF16), w_att_o=w_att_o[i].astype(BF16), w_out=w_out[i].astype(BF16),
        ln1_g=ln1_g[i][None, :], ln1_b=ln1_b[i][None, :],
        peer_wqt=peer_wq[i].T.astype(BF16), peer_subkeys=peer_subkeys[i].astype(BF16),
        peer_u=peer_u[i].astype(BF16), peer_vt=peer_v[i].T.astype(BF16),
        w_ple_gate=w_ple_gate[i].astype(BF16), w_ple=w_ple[i].astype(BF16),
        ln2_g=ln2_g[i][None, :], ln2_b=ln2_b[i][None, :],
    )

    def attn_p(pr):
        return _attention_prompt(pr["aq"], pr["iq"], pr["iw"], pr["akb"], pr["avb"], pr["ikb"], nbp, seq)

    yp, kp, vp, ikp, rp = _layer(
        x_prompt.reshape(nbp * seq, dm), p_prompt[i].reshape(nbp * seq, -1),
        jnp.arange(seq, dtype=jnp.int32), min(512, seq), weights, attn_p,
        jnp.zeros((nbp,) + state_ret.shape[2:], F32), RET_CHUNK, RET_CHUNK, alpha)

    def attn_s(pr):
        return _attention_sample(pr["aq"], pr["iq"], pr["iw"], pr["ak"], pr["av"], pr["ik"],
                                 cache_k[i], cache_v[i], cache_kidx[i], page_table, nbs, t_new)

    ns = nbs * t_new
    pos_s = jnp.tile(past_len + jnp.arange(t_new, dtype=jnp.int32), nbs)
    ys, ks, vs, iks, rs = _layer(
        x_sample.reshape(ns, dm), p_sample[i].reshape(ns, -1), pos_s, ns, weights, attn_s,
        state_ret[i].astype(F32), BF16_SUBLANES, t_new, alpha)

    return (yp.reshape(nbp, seq, dm), ys.reshape(nbs, t_new, dm),
            kp.reshape(1, nbp, seq, ATT_KV_HEADS, ATT_HD), vp.reshape(1, nbp, seq, ATT_KV_HEADS, ATT_HD),
            ikp.reshape(1, nbp, seq, IDX_HD), rp[None].astype(state_ret.dtype),
            ks.reshape(1, nbs, t_new, ATT_KV_HEADS, ATT_HD), vs.reshape(1, nbs, t_new, ATT_KV_HEADS, ATT_HD),
            iks.reshape(1, nbs, t_new, IDX_HD), rs[None].astype(state_ret.dtype))
```

```python
import functools

import jax
import jax.numpy as jnp
from jax import lax
from jax.experimental import pallas as pl
from jax.experimental.pallas import tpu as pltpu

F32 = jnp.float32
BF16 = jnp.bfloat16

RET_HEADS = 4
RET_DK = 256
RET_DV = 512
RET_CHUNK = 256
ATT_HEADS = 8
ATT_KV_HEADS = 2
ATT_HD = 128
IDX_HEADS = 8
IDX_HD = 64
TOPK_ATTN = 256
ROPE_THETA = 10000.0
PEER_HEADS = 8
PEER_NKEYS = 128
PEER_DKEY = 256
PEER_TOPK = 16
LN_EPS = 1e-5
NEG = -1e30
INF = float("inf")

LANES = 128
BF16_SUBLANES = 16
VMEM_LIMIT = 48 * 1024 * 1024
BISECT_ITERS = 32
SAMPLE_PAGES_PER_STEP = 16
SAMPLE_SCORE_PAGES_PER_STEP = 64
TOKEN_TILE = 512
PROJ_COL_TILE = 1024
ATTN_QUERY_TILE = 256
PEER_TOKEN_TILE = 512


def _dot(a, b):
    return jnp.dot(a, b, preferred_element_type=F32)


def _dot_nt(a, b):
    return lax.dot_general(a, b, (((1,), (1,)), ((), ())), preferred_element_type=F32)


def _dot_tn(a, b):
    return lax.dot_general(a, b, (((0,), (0,)), ((), ())), preferred_element_type=F32)


def _params(sem):
    return pltpu.CompilerParams(dimension_semantics=sem, vmem_limit_bytes=VMEM_LIMIT)


def _rope_tables(pos, d, width):
    half = d // 2
    inv = ROPE_THETA ** (-jnp.arange(half, dtype=F32) / half)
    ang = pos.astype(F32)[:, None] * inv[None, :]
    cos, sin = jnp.cos(ang), jnp.sin(ang)
    cosf = jnp.concatenate([cos, cos], axis=-1)
    sinf = jnp.concatenate([-sin, sin], axis=-1)
    reps = width // d
    return jnp.tile(cosf, (1, reps)), jnp.tile(sinf, (1, reps))


def _partner(y, d):
    if d == 2 * LANES:
        return jnp.concatenate([y[:, LANES:], y[:, :LANES]], axis=-1)
    if d == LANES:
        return pltpu.roll(y, LANES // 2, axis=1)
    lane = lax.broadcasted_iota(jnp.int32, y.shape, 1)
    first_half = (lane & (d - 1)) < (d // 2)
    return jnp.where(first_half, pltpu.roll(y, LANES - d // 2, axis=1), pltpu.roll(y, d // 2, axis=1))


def _rope(y, cosf, sinf, d):
    return y * cosf + _partner(y, d) * sinf


def _ret_qk_kernel(x_ref, w_ref, cos_ref, sin_ref, q_ref, k_ref):
    x = x_ref[...]
    cosf, sinf = cos_ref[...], sin_ref[...]
    for h in range(2 * RET_HEADS):
        y = _dot(x, w_ref[:, h * RET_DK:(h + 1) * RET_DK])
        r = _rope(y, cosf, sinf, RET_DK)
        if h < RET_HEADS:
            q_ref[:, h * RET_DK:(h + 1) * RET_DK] = r.astype(BF16)
        else:
            hh = h - RET_HEADS
            k_ref[:, hh * RET_DK:(hh + 1) * RET_DK] = (r * (RET_DK ** -0.5)).astype(BF16)


def _plain_kernel(x_ref, w_ref, o_ref, *, sigmoid):
    y = _dot(x_ref[...], w_ref[...])
    if sigmoid:
        y = jax.nn.sigmoid(y)
    o_ref[...] = y.astype(o_ref.dtype)


def _att_proj_kernel(x_ref, wq_ref, wk_ref, wv_ref, cos_ref, sin_ref,
                     q_ref, k_ref, v_ref, kb_ref, vb_ref):
    x = x_ref[...]
    cosf, sinf = cos_ref[...], sin_ref[...]
    for h in range(ATT_HEADS):
        sl = slice(h * ATT_HD, (h + 1) * ATT_HD)
        q_ref[:, sl] = _rope(_dot(x, wq_ref[:, sl]), cosf, sinf, ATT_HD).astype(BF16)
    tm = x.shape[0]
    v = _dot(x, wv_ref[...])
    vb_ref[...] = v.astype(BF16)
    for h in range(ATT_KV_HEADS):
        sl = slice(h * ATT_HD, (h + 1) * ATT_HD)
        k = _rope(_dot(x, wk_ref[:, sl]), cosf, sinf, ATT_HD)
        kb_ref[:, sl] = k.astype(BF16)
        head_rows = pl.ds(h, tm, stride=ATT_KV_HEADS)
        k_ref[head_rows, :] = k
        v_ref[head_rows, :] = v[:, sl]


def _idx_proj_kernel(x_ref, wq_ref, wk_ref, ww_ref, cos_ref, sin_ref, g_ref, b_ref,
                     q_ref, k_ref, kb_ref, w_ref):
    x = x_ref[...]
    cosf, sinf = cos_ref[...], sin_ref[...]
    for c in range(IDX_HEADS * IDX_HD // LANES):
        sl = slice(c * LANES, (c + 1) * LANES)
        q_ref[:, sl] = _rope(_dot(x, wq_ref[:, sl]), cosf, sinf, IDX_HD).astype(BF16)
    y = _dot(x, wk_ref[...])
    mu = jnp.mean(y, axis=-1, keepdims=True)
    var = jnp.mean(jnp.square(y - mu), axis=-1, keepdims=True)
    yn = (y - mu) * lax.rsqrt(var + LN_EPS) * g_ref[...] + b_ref[...]
    k = yn * cosf + pltpu.roll(yn, IDX_HD // 2, axis=1) * sinf
    k_ref[...] = k[:, :IDX_HD]
    kb_ref[...] = k[:, :IDX_HD].astype(BF16)
    w = _dot(x, ww_ref[...])
    w_ref[...] = w[:, :IDX_HEADS] * (IDX_HEADS ** -0.5)


def _project(xb, w, ikg, ikb, pos_rows, tm):
    n, dm = xb.shape
    nt = n // tm
    rows = pos_rows.shape[0]
    tab_blocks = rows // tm
    hq = RET_HEADS * RET_DK
    hv = RET_HEADS * RET_DV
    o = 0
    w_rqk = w[:, o:o + 2 * hq]; o += 2 * hq
    w_rv = w[:, o:o + hv]; o += hv
    w_rg = w[:, o:o + hv]; o += hv
    w_aq = w[:, o:o + ATT_HEADS * ATT_HD]; o += ATT_HEADS * ATT_HD
    w_ak = w[:, o:o + ATT_KV_HEADS * ATT_HD]; o += ATT_KV_HEADS * ATT_HD
    w_av = w[:, o:o + ATT_KV_HEADS * ATT_HD]; o += ATT_KV_HEADS * ATT_HD
    w_iq = w[:, o:o + IDX_HEADS * IDX_HD]; o += IDX_HEADS * IDX_HD
    w_ik = w[:, o:o + IDX_HD]; o += IDX_HD
    w_iw = w[:, o:o + IDX_HEADS]; o += IDX_HEADS
    w_gab = w[:, o:]

    x_spec = pl.BlockSpec((tm, dm), lambda i: (i, 0))

    def tab_spec(width):
        return pl.BlockSpec((tm, width), lambda i: (i % tab_blocks, 0))

    def full(shape):
        return pl.BlockSpec(shape, lambda i: (0,) * len(shape))

    def row_spec(width):
        return pl.BlockSpec((tm, width), lambda i: (i, 0))

    cos256, sin256 = _rope_tables(pos_rows, RET_DK, RET_DK)
    rq, rk = pl.pallas_call(
        _ret_qk_kernel,
        grid=(nt,),
        in_specs=[x_spec, full((dm, 2 * hq)), tab_spec(RET_DK), tab_spec(RET_DK)],
        out_specs=[row_spec(hq), row_spec(hq)],
        out_shape=[jax.ShapeDtypeStruct((n, hq), BF16)] * 2,
        compiler_params=_params(("parallel",)),
        name="proj_ret_qk",
    )(xb, w_rqk, cos256, sin256)

    def plain(wc, dtype, sigmoid, name):
        cols = wc.shape[1]
        tn = min(PROJ_COL_TILE, cols)
        return pl.pallas_call(
            functools.partial(_plain_kernel, sigmoid=sigmoid),
            grid=(cols // tn, nt),
            in_specs=[pl.BlockSpec((tm, dm), lambda j, i: (i, 0)),
                      pl.BlockSpec((dm, tn), lambda j, i: (0, j))],
            out_specs=pl.BlockSpec((tm, tn), lambda j, i: (i, j)),
            out_shape=jax.ShapeDtypeStruct((n, cols), dtype),
            compiler_params=_params(("parallel", "parallel")),
            name=name,
        )(xb, wc)

    rv = plain(w_rv, BF16, False, "proj_ret_v")
    rg = plain(w_rg, F32, False, "proj_ret_gate")
    sgab = plain(w_gab, F32, True, "proj_branch_gates")

    cos128, sin128 = _rope_tables(pos_rows, ATT_HD, ATT_HD)
    kvw = ATT_KV_HEADS * ATT_HD
    aq, ak, av, akb, avb = pl.pallas_call(
        _att_proj_kernel,
        grid=(nt,),
        in_specs=[x_spec, full((dm, ATT_HEADS * ATT_HD)), full((dm, kvw)), full((dm, kvw)),
                  tab_spec(ATT_HD), tab_spec(ATT_HD)],
        out_specs=[row_spec(ATT_HEADS * ATT_HD),
                   pl.BlockSpec((tm * ATT_KV_HEADS, ATT_HD), lambda i: (i, 0)),
                   pl.BlockSpec((tm * ATT_KV_HEADS, ATT_HD), lambda i: (i, 0)),
                   row_spec(kvw), row_spec(kvw)],
        out_shape=[jax.ShapeDtypeStruct((n, ATT_HEADS * ATT_HD), BF16),
                   jax.ShapeDtypeStruct((n * ATT_KV_HEADS, ATT_HD), F32),
                   jax.ShapeDtypeStruct((n * ATT_KV_HEADS, ATT_HD), F32),
                   jax.ShapeDtypeStruct((n, kvw), BF16), jax.ShapeDtypeStruct((n, kvw), BF16)],
        compiler_params=_params(("parallel",)),
        name="proj_att",
    )(xb, w_aq, w_ak, w_av, cos128, sin128)

    cos64, sin64 = _rope_tables(pos_rows, IDX_HD, LANES)
    w_ik2 = jnp.concatenate([w_ik, w_ik], axis=1)
    w_iw_pad = jnp.pad(w_iw, ((0, 0), (0, LANES - IDX_HEADS)))
    iq, ik, ikb, iw = pl.pallas_call(
        _idx_proj_kernel,
        grid=(nt,),
        in_specs=[x_spec, full((dm, IDX_HEADS * IDX_HD)), full((dm, LANES)), full((dm, LANES)),
                  tab_spec(LANES), tab_spec(LANES), full((1, LANES)), full((1, LANES))],
        out_specs=[row_spec(IDX_HEADS * IDX_HD), row_spec(IDX_HD), row_spec(IDX_HD), row_spec(IDX_HEADS)],
        out_shape=[jax.ShapeDtypeStruct((n, IDX_HEADS * IDX_HD), BF16),
                   jax.ShapeDtypeStruct((n, IDX_HD), F32), jax.ShapeDtypeStruct((n, IDX_HD), BF16),
                   jax.ShapeDtypeStruct((n, IDX_HEADS), F32)],
        compiler_params=_params(("parallel",)),
        name="proj_idx",
    )(xb, w_iq, w_ik2, w_iw_pad, cos64, sin64, ikg, ikb)
    return dict(rq=rq, rk=rk, rv=rv, rg=rg, sgab=sgab, aq=aq, ak=ak, av=av, akb=akb, avb=avb,
                iq=iq, ik=ik, ikb=ikb, iw=iw)


def _retention_kernel(sdec_ref, q_ref, k_ref, v_ref, gate_ref, gg_ref, gb_ref,
                      d_ref, cdec_ref, kdec_ref, si_ref, o_ref, so_ref):
    h = pl.program_id(1)
    c = pl.program_id(2)

    @pl.when(c == 0)
    def _():
        so_ref[0, 0] = si_ref[0, 0]

    q = q_ref[...]
    k = k_ref[...]
    v = v_ref[...]
    state = so_ref[0, 0]
    s = _dot_nt(q, k)
    inner = _dot((s * d_ref[0]).astype(BF16), v)
    cross = _dot(q, state.astype(BF16)) * cdec_ref[0]
    o = inner + cross
    kd = (k.astype(F32) * kdec_ref[0]).astype(BF16)
    so_ref[0, 0] = state * sdec_ref[h] + _dot_tn(kd, v)
    mu = jnp.mean(o, axis=-1, keepdims=True)
    var = jnp.mean(jnp.square(o - mu), axis=-1, keepdims=True)
    on = (o - mu) * lax.rsqrt(var + LN_EPS) * gg_ref[...] + gb_ref[...]
    o_ref[...] = (jax.nn.silu(gate_ref[...]) * on).astype(o_ref.dtype)


def _retention(rq, rk, rv, rg, gn_g, gn_b, state_in, chunk, chunk_eff):
    n = rq.shape[0]
    nb = state_in.shape[0]
    nc = n // (nb * chunk)
    lg = jnp.log1p(-jnp.exp2(-5.0 - jnp.arange(RET_HEADS, dtype=F32)))[:, None]
    i = jnp.arange(chunk, dtype=F32)
    diff = i[:, None] - i[None, :]
    decay = jnp.where(diff >= 0, jnp.exp(lg[:, :, None] * jnp.maximum(diff, 0.0)), 0.0)
    cdec = jnp.exp(lg * (i + 1.0))[:, :, None]
    kdec = jnp.where(i < chunk_eff, jnp.exp(lg * jnp.maximum(chunk_eff - 1.0 - i, 0.0)), 0.0)[:, :, None]
    sdec = jnp.exp(lg[:, 0] * chunk_eff)

    o, s_out = pl.pallas_call(
        _retention_kernel,
        grid=(nb, RET_HEADS, nc),
        in_specs=[
            pl.BlockSpec(memory_space=pltpu.SMEM),
            pl.BlockSpec((chunk, RET_DK), lambda b, h, c: (b * nc + c, h)),
            pl.BlockSpec((chunk, RET_DK), lambda b, h, c: (b * nc + c, h)),
            pl.BlockSpec((chunk, RET_DV), lambda b, h, c: (b * nc + c, h)),
            pl.BlockSpec((chunk, RET_DV), lambda b, h, c: (b * nc + c, h)),
            pl.BlockSpec((1, RET_DV), lambda b, h, c: (0, h)),
            pl.BlockSpec((1, RET_DV), lambda b, h, c: (0, h)),
            pl.BlockSpec((1, chunk, chunk), lambda b, h, c: (h, 0, 0)),
            pl.BlockSpec((1, chunk, 1), lambda b, h, c: (h, 0, 0)),
            pl.BlockSpec((1, chunk, 1), lambda b, h, c: (h, 0, 0)),
            pl.BlockSpec((1, 1, RET_DK, RET_DV), lambda b, h, c: (b, h, 0, 0)),
        ],
        out_specs=[
            pl.BlockSpec((chunk, RET_DV), lambda b, h, c: (b * nc + c, h)),
            pl.BlockSpec((1, 1, RET_DK, RET_DV), lambda b, h, c: (b, h, 0, 0)),
        ],
        out_shape=[jax.ShapeDtypeStruct((n, RET_HEADS * RET_DV), BF16),
                   jax.ShapeDtypeStruct(state_in.shape, F32)],
        compiler_params=_params(("parallel", "parallel", "arbitrary")),
        name="retention",
    )(sdec, rq, rk, rv, rg, gn_g, gn_b, decay, cdec, kdec, state_in)
    return o, s_out


M_INIT = -1e29


def _lanes(col):
    return jnp.broadcast_to(col, (col.shape[0], LANES))


def _kth_largest(reduce_tiles, mn, mx, kq):
    kq_b = _lanes(kq)

    def bisect_body(_, carry):
        lo, hi = carry
        mid = 0.5 * (lo + hi)
        cnt = reduce_tiles(lambda a, t: a + jnp.where(t >= mid, 1.0, 0.0), 0.0)
        ge = _lanes(jnp.sum(cnt, axis=-1, keepdims=True)) >= kq_b
        return jnp.where(ge, mid, lo), jnp.where(ge, hi, mid)

    lo, _ = lax.fori_loop(0, BISECT_ITERS, bisect_body, (_lanes(mn), _lanes(mx + 1.0)))
    tau = jnp.min(reduce_tiles(lambda a, t: jnp.minimum(a, jnp.where(t >= lo, t, INF)), INF),
                  axis=-1, keepdims=True)
    tau_b = _lanes(tau)
    cgt = jnp.sum(reduce_tiles(lambda a, t: a + jnp.where(t > tau_b, 1.0, 0.0), 0.0),
                  axis=-1, keepdims=True)
    return tau, kq - cgt


def _softmax_step(s, m_ref, l_ref, acc_ref, idx, v):
    m_old = m_ref[idx]
    m_new = jnp.maximum(m_old, jnp.max(s, axis=-1, keepdims=True))
    p = jnp.exp(s - m_new)
    alpha = jnp.exp(m_old - m_new)
    l_ref[idx] = alpha * l_ref[idx] + jnp.sum(p, axis=-1, keepdims=True)
    acc_ref[idx] = alpha * acc_ref[idx] + _dot(p.astype(BF16), v)
    m_ref[idx] = m_new


def _fold_rows(x):
    return jnp.sum(x.reshape(x.shape[0] // 8, 8, x.shape[1]), axis=0)


def _attn_prompt_kernel(aqt_ref, iqt_ref, iwt_ref, k_ref, vt_ref, kidx_ref, tril_ref, o_ref,
                        sc_ref, m_ref, l_ref, acc_ref, *, tq, topk):
    i = pl.program_id(1)
    nkv = i + 1
    qpos = i * tq + lax.broadcasted_iota(jnp.int32, (1, tq), 1)
    kq = jnp.minimum(topk, qpos + 1).astype(F32)
    group = ATT_HEADS // ATT_KV_HEADS

    def score_body(j, carry):
        mn, mx = carry
        kx = kidx_ref[pl.ds(pl.multiple_of(j * tq, tq), tq), :]
        s = jnp.zeros((tq, tq), F32)
        for h in range(IDX_HEADS):
            d = _dot(kx, iqt_ref[h * IDX_HD:(h + 1) * IDX_HD, :]) * (IDX_HD ** -0.5)
            s = s + iwt_ref[h:h + 1, :] * jnp.maximum(d, 0.0)
        kpos = j * tq + lax.broadcasted_iota(jnp.int32, (tq, tq), 0)
        vis = kpos <= qpos
        sc_ref[j] = jnp.where(vis, s, -INF)
        mn = jnp.minimum(mn, jnp.min(jnp.where(vis, s, INF), axis=0, keepdims=True))
        mx = jnp.maximum(mx, jnp.max(jnp.where(vis, s, -INF), axis=0, keepdims=True))
        return mn, mx

    mn, mx = lax.fori_loop(0, nkv, score_body,
                           (jnp.full((1, tq), INF, F32), jnp.full((1, tq), -INF, F32)))

    def count(pred):
        def body(j, acc):
            return acc + _fold_rows(jnp.where(pred(sc_ref[j]), 1.0, 0.0))
        return jnp.sum(lax.fori_loop(0, nkv, body, jnp.zeros((8, tq), F32)), axis=0, keepdims=True)

    def bisect_body(_, carry):
        lo, hi = carry
        mid = 0.5 * (lo + hi)
        ge = count(lambda t: t >= mid) >= kq
        return jnp.where(ge, mid, lo), jnp.where(ge, hi, mid)

    lo, _ = lax.fori_loop(0, BISECT_ITERS, bisect_body, (mn, mx + 1.0))

    def snap_body(j, acc):
        t = sc_ref[j]
        return jnp.minimum(acc, jnp.min(jnp.where(t >= lo, t, INF), axis=0, keepdims=True))

    tau = lax.fori_loop(0, nkv, snap_body, jnp.full((1, tq), INF, F32))
    need = kq - count(lambda t: t > tau)

    tril = tril_ref[...]

    def mask_body(j, run):
        t = sc_ref[j]
        eqf = jnp.where(t == tau, 1.0, 0.0)
        prefix = _dot(tril, eqf.astype(BF16))
        sel = (t > tau) | ((t == tau) & ((run + prefix) <= need))
        sc_ref[j] = jnp.where(sel, 0.0, NEG)
        return run + jnp.sum(eqf, axis=0, keepdims=True)

    lax.fori_loop(0, nkv, mask_body, jnp.zeros((1, tq), F32))

    m_ref[...] = jnp.full(m_ref.shape, M_INIT, F32)
    l_ref[...] = jnp.zeros(l_ref.shape, F32)
    acc_ref[...] = jnp.zeros(acc_ref.shape, F32)

    def att_body(j, carry):
        rows = pl.ds(pl.multiple_of(j * tq, tq), tq)
        for h in range(ATT_HEADS):
            g = h // group
            kg = k_ref[rows, g * ATT_HD:(g + 1) * ATT_HD]
            vg = vt_ref[0, j, g * ATT_HD:(g + 1) * ATT_HD, :]
            for c in range(tq // LANES):
                ls = slice(c * LANES, (c + 1) * LANES)
                s = _dot(kg, aqt_ref[h * ATT_HD:(h + 1) * ATT_HD, ls]) * (ATT_HD ** -0.5) + sc_ref[j, :, ls]
                m_old = m_ref[h, :, ls]
                m_new = jnp.maximum(m_old, jnp.max(s, axis=0, keepdims=True))
                p = jnp.exp(s - m_new)
                alpha = jnp.exp(m_old - m_new)
                l_ref[h, :, ls] = alpha * l_ref[h, :, ls] + jnp.sum(p, axis=0, keepdims=True)
                acc_ref[h, :, ls] = alpha * acc_ref[h, :, ls] + _dot(vg, p.astype(BF16))
                m_ref[h, :, ls] = m_new
        return carry

    lax.fori_loop(0, nkv, att_body, 0)
    for h in range(ATT_HEADS):
        o_ref[h * ATT_HD:(h + 1) * ATT_HD, :] = (acc_ref[h] / l_ref[h]).astype(o_ref.dtype)


def _attention_prompt(aq, iq, iw, akb, avb, ikb, nb, seq):
    tq = min(ATTN_QUERY_TILE, seq)
    assert seq % tq == 0 and tq % LANES == 0
    nq = seq // tq
    topk = min(TOPK_ATTN, seq // 4)
    n = aq.shape[0]
    tril = jnp.tril(jnp.ones((tq, tq), F32)).astype(BF16)
    kvw = ATT_KV_HEADS * ATT_HD
    vt = avb.reshape(nb, nq, tq, kvw).transpose(0, 1, 3, 2)
    ot = pl.pallas_call(
        functools.partial(_attn_prompt_kernel, tq=tq, topk=topk),
        grid=(nb, nq),
        in_specs=[
            pl.BlockSpec((ATT_HEADS * ATT_HD, tq), lambda b, i: (0, b * nq + i)),
            pl.BlockSpec((IDX_HEADS * IDX_HD, tq), lambda b, i: (0, b * nq + i)),
            pl.BlockSpec((IDX_HEADS, tq), lambda b, i: (0, b * nq + i)),
            pl.BlockSpec((seq, kvw), lambda b, i: (b, 0)),
            pl.BlockSpec((1, nq, kvw, tq), lambda b, i: (b, 0, 0, 0)),
            pl.BlockSpec((seq, IDX_HD), lambda b, i: (b, 0)),
            pl.BlockSpec((tq, tq), lambda b, i: (0, 0)),
        ],
        out_specs=pl.BlockSpec((ATT_HEADS * ATT_HD, tq), lambda b, i: (0, b * nq + i)),
        out_shape=jax.ShapeDtypeStruct((ATT_HEADS * ATT_HD, n), BF16),
        scratch_shapes=[
            pltpu.VMEM((nq, tq, tq), F32),
            pltpu.VMEM((ATT_HEADS, 1, tq), F32),
            pltpu.VMEM((ATT_HEADS, 1, tq), F32),
            pltpu.VMEM((ATT_HEADS, ATT_HD, tq), F32),
        ],
        compiler_params=_params(("parallel", "arbitrary")),
        name="attention_prompt",
    )(aq.T, iq.T, iw.T, akb, vt, ikb, tril)
    return ot.T


def _page_scores(iq_ref, iw_ref, kxt):
    tp = iq_ref.shape[1] // IDX_HEADS
    d = _dot(iq_ref[0], kxt) * (IDX_HD ** -0.5)
    d = iw_ref[0] * jnp.maximum(d, 0.0)
    s = d[:tp]
    for h in range(1, IDX_HEADS):
        s = s + d[h * tp:(h + 1) * tp]
    return s


def _sample_scores_kernel(pt_ref, iq_ref, iw_ref, *refs, npp, n_new, topk):
    page_refs = refs[:npp]
    knew_ref = refs[npp]
    sc_ref, tau_ref, need_ref = refs[npp + 1:]
    s = pl.program_id(1)
    npages = sc_ref.shape[1] - 1
    tp, page = sc_ref.shape[2], sc_ref.shape[3]

    kxt = jnp.concatenate([r[0] for r in page_refs], axis=1).astype(BF16)
    sc = _page_scores(iq_ref, iw_ref, kxt)
    for p in range(npp):
        sc_ref[0, s * npp + p] = sc[:, p * page:(p + 1) * page]

    @pl.when(s == pl.num_programs(1) - 1)
    def _():
        snew = _page_scores(iq_ref, iw_ref, knew_ref[0].astype(BF16))
        t = jnp.minimum(lax.broadcasted_iota(jnp.int32, (tp, page), 0), n_new - 1)
        j = lax.broadcasted_iota(jnp.int32, (tp, page), 1)
        sc_ref[0, npages] = jnp.where(j <= t, snew, -INF)

        def reduce_tiles(fn, init):
            def body(c, acc):
                for p in range(npp):
                    acc = fn(acc, sc_ref[0, c * npp + p])
                return acc
            acc = lax.fori_loop(0, npages // npp, body, jnp.full((tp, page), init, F32))
            return fn(acc, sc_ref[0, npages])

        mn = jnp.min(reduce_tiles(lambda a, x: jnp.minimum(a, jnp.where(x == -INF, INF, x)), INF),
                     axis=-1, keepdims=True)
        mx = jnp.max(reduce_tiles(jnp.maximum, -INF), axis=-1, keepdims=True)
        tau, need = _kth_largest(reduce_tiles, mn, mx, jnp.full((tp, 1), float(topk), F32))
        tau_ref[0] = _lanes(tau)
        need_ref[0] = _lanes(need)


def _sample_attend_kernel(pt_ref, q_ref, sc_ref, tau_ref, need_ref, tri_ref, *refs, npp):
    k_refs = refs[:npp]
    v_refs = refs[npp:2 * npp]
    knew_ref, vnew_ref, o_ref, m_ref, l_ref, acc_ref, run_ref = refs[2 * npp:]
    s = pl.program_id(1)
    npages = sc_ref.shape[1] - 1
    page = sc_ref.shape[3]
    group = ATT_HEADS // ATT_KV_HEADS
    grows = q_ref.shape[1] // ATT_KV_HEADS

    @pl.when(s == 0)
    def _():
        m_ref[...] = jnp.full(m_ref.shape, M_INIT, F32)
        l_ref[...] = jnp.zeros(l_ref.shape, F32)
        acc_ref[...] = jnp.zeros(acc_ref.shape, F32)
        run_ref[...] = jnp.zeros(run_ref.shape, F32)

    tau = tau_ref[0][:, :1]
    need = need_ref[0][:, :1]
    tri = tri_ref[...]

    def attend(pages, krefs, vrefs):
        tp = sc_ref.shape[2]
        scs = [sc_ref[0, pg] for pg in pages]
        eqs = [jnp.where(t == tau, 1.0, 0.0) for t in scs]
        prefix = _dot(jnp.concatenate(eqs, axis=0).astype(BF16), tri)
        run = run_ref[...]
        madds = []
        for k, t in enumerate(scs):
            sel = (t > tau) | ((t == tau) & ((run + prefix[k * tp:(k + 1) * tp]) <= need))
            madds.append(jnp.where(sel, 0.0, NEG))
            run = run + jnp.sum(eqs[k], axis=-1, keepdims=True)
        run_ref[...] = run
        madd = jnp.concatenate(madds, axis=1)
        maddg = jnp.concatenate([madd] * group, axis=0)
        for g in range(ATT_KV_HEADS):
            head_rows = pl.ds(g, page, stride=ATT_KV_HEADS)
            kg = jnp.concatenate([r[0, head_rows, :] for r in krefs], axis=0).astype(BF16)
            vg = jnp.concatenate([r[0, head_rows, :] for r in vrefs], axis=0).astype(BF16)
            sg = _dot_nt(q_ref[0, g * grows:(g + 1) * grows, :], kg) * (ATT_HD ** -0.5) + maddg
            _softmax_step(sg, m_ref, l_ref, acc_ref, g, vg)

    attend([s * npp + p for p in range(npp)], k_refs, v_refs)

    @pl.when(s == pl.num_programs(1) - 1)
    def _():
        attend([npages], [knew_ref], [vnew_ref])
        for g in range(ATT_KV_HEADS):
            o_ref[0, g * grows:(g + 1) * grows, :] = acc_ref[g] / l_ref[g]


def _attention_sample(aq, iq, iw, ak, av, ik, cache_k, cache_v, cache_kidx, page_table, nb, t_new):
    n_phys, page = cache_kidx.shape[0], cache_kidx.shape[1]
    npages = page_table.shape[1]
    npp = SAMPLE_PAGES_PER_STEP if npages % SAMPLE_PAGES_PER_STEP == 0 else 1
    nsteps = npages // npp
    npp_sc = SAMPLE_SCORE_PAGES_PER_STEP if npages % SAMPLE_SCORE_PAGES_PER_STEP == 0 else npp
    tp = BF16_SUBLANES
    topk = min(TOPK_ATTN, (npages * page + t_new) // 4)
    prow = page * ATT_KV_HEADS

    def pad_rows(a, per_seq, rows):
        a = a.reshape(nb, per_seq, a.shape[-1])
        return jnp.pad(a, ((0, 0), (0, rows - per_seq), (0, 0)))

    iq_p = pad_rows(iq, t_new, tp).reshape(nb, tp, IDX_HEADS, IDX_HD).transpose(0, 2, 1, 3)
    iq_p = iq_p.reshape(nb, IDX_HEADS * tp, IDX_HD)
    iw_p = pad_rows(iw, t_new, tp).transpose(0, 2, 1).reshape(nb, IDX_HEADS * tp, 1)
    q_p = pad_rows(aq, t_new, tp).reshape(nb, tp, ATT_HEADS, ATT_HD).transpose(0, 2, 1, 3)
    q_p = q_p.reshape(nb, ATT_HEADS * tp, ATT_HD)
    knew = pad_rows(ak, t_new * ATT_KV_HEADS, prow)
    vnew = pad_rows(av, t_new * ATT_KV_HEADS, prow)
    kinew = pad_rows(ik, t_new, page).swapaxes(1, 2)
    kidx_t = cache_kidx.swapaxes(1, 2)
    ck = cache_k.reshape(n_phys, prow, ATT_HD)
    cv = cache_v.reshape(n_phys, prow, ATT_HD)
    tri = jnp.triu(jnp.ones((page, page), F32)).astype(BF16)

    def page_spec(rows, width, p, per_step=npp):
        return pl.BlockSpec((1, rows, width), lambda b, s, pt, p=p: (pt[b, s * per_step + p], 0, 0))

    def seq_spec(shape):
        return pl.BlockSpec((1,) + shape, lambda b, s, pt: (b,) + (0,) * len(shape))

    sc, tau, need = pl.pallas_call(
        functools.partial(_sample_scores_kernel, npp=npp_sc, n_new=t_new, topk=topk),
        grid_spec=pltpu.PrefetchScalarGridSpec(
            num_scalar_prefetch=1,
            grid=(nb, npages // npp_sc),
            in_specs=[seq_spec((IDX_HEADS * tp, IDX_HD)), seq_spec((IDX_HEADS * tp, 1))]
                     + [page_spec(IDX_HD, page, p, npp_sc) for p in range(npp_sc)] + [seq_spec((IDX_HD, page))],
            out_specs=[seq_spec((npages + 1, tp, page)), seq_spec((tp, page)), seq_spec((tp, page))],
        ),
        out_shape=[jax.ShapeDtypeStruct((nb, npages + 1, tp, page), F32),
                   jax.ShapeDtypeStruct((nb, tp, page), F32),
                   jax.ShapeDtypeStruct((nb, tp, page), F32)],
        compiler_params=_params(("parallel", "arbitrary")),
        name="sample_scores",
    )(page_table, iq_p, iw_p, *([kidx_t] * npp_sc), kinew)

    grows = (ATT_HEADS // ATT_KV_HEADS) * tp
    o = pl.pallas_call(
        functools.partial(_sample_attend_kernel, npp=npp),
        grid_spec=pltpu.PrefetchScalarGridSpec(
            num_scalar_prefetch=1,
            grid=(nb, nsteps),
            in_specs=[seq_spec((ATT_HEADS * tp, ATT_HD)), seq_spec((npages + 1, tp, page)),
                      seq_spec((tp, page)), seq_spec((tp, page)),
                      pl.BlockSpec((page, page), lambda b, s, pt: (0, 0))]
                     + [page_spec(prow, ATT_HD, p) for p in range(npp)]
                     + [page_spec(prow, ATT_HD, p) for p in range(npp)]
                     + [seq_spec((prow, ATT_HD)), seq_spec((prow, ATT_HD))],
            out_specs=seq_spec((ATT_HEADS * tp, ATT_HD)),
            scratch_shapes=[
                pltpu.VMEM((ATT_KV_HEADS, grows, 1), F32),
                pltpu.VMEM((ATT_KV_HEADS, grows, 1), F32),
                pltpu.VMEM((ATT_KV_HEADS, grows, ATT_HD), F32),
                pltpu.VMEM((tp, 1), F32),
            ],
        ),
        out_shape=jax.ShapeDtypeStruct((nb, ATT_HEADS * tp, ATT_HD), F32),
        compiler_params=_params(("parallel", "arbitrary")),
        name="sample_attend",
    )(page_table, q_p, sc, tau, need, tri, *([ck] * npp), *([cv] * npp), knew, vnew)
    o = o.reshape(nb, ATT_HEADS, tp, ATT_HD)[:, :, :t_new].transpose(0, 2, 1, 3)
    return o.reshape(nb * t_new, ATT_HEADS * ATT_HD).astype(BF16)


def _layer_norm(y, g, b):
    mu = jnp.mean(y, axis=-1, keepdims=True)
    var = jnp.mean(jnp.square(y - mu), axis=-1, keepdims=True)
    return (y - mu) * lax.rsqrt(var + LN_EPS) * g + b


def _mix_kernel(x_ref, ret_ref, att_ref, sga_ref, sgb_ref, wr_ref, wa_ref, wo_ref, g_ref, b_ref,
                o_ref, ot_ref, *, alpha):
    branch = sga_ref[...] * _dot(ret_ref[...], wr_ref[...]) + sgb_ref[...] * _dot(att_ref[...], wa_ref[...])
    y = alpha * x_ref[...] + _dot(branch.astype(BF16), wo_ref[...])
    x1 = _layer_norm(y, g_ref[...], b_ref[...])
    o_ref[...] = x1
    ot_ref[...] = x1.T.astype(BF16)


def _mix(x, ret_o, att_o, sgab, w_ret_o, w_att_o, w_out, g, b, tm, alpha):
    n, dm = x.shape
    hv = RET_HEADS * RET_DV
    ha = ATT_HEADS * ATT_HD

    def full(shape):
        return pl.BlockSpec(shape, lambda i: (0,) * len(shape))

    return pl.pallas_call(
        functools.partial(_mix_kernel, alpha=alpha),
        grid=(n // tm,),
        in_specs=[pl.BlockSpec((tm, dm), lambda i: (i, 0)),
                  pl.BlockSpec((tm, hv), lambda i: (i, 0)),
                  pl.BlockSpec((tm, ha), lambda i: (i, 0)),
                  pl.BlockSpec((tm, dm), lambda i: (i, 0)),
                  pl.BlockSpec((tm, dm), lambda i: (i, 1)),
                  full((hv, dm)), full((ha, dm)), full((dm, dm)), full((1, dm)), full((1, dm))],
        out_specs=[pl.BlockSpec((tm, dm), lambda i: (i, 0)),
                   pl.BlockSpec((dm, tm), lambda i: (0, i))],
        out_shape=[jax.ShapeDtypeStruct((n, dm), F32), jax.ShapeDtypeStruct((dm, n), BF16)],
        compiler_params=_params(("parallel",)),
        name="branch_mix_ln1",
    )(x, ret_o, att_o, sgab, sgab, w_ret_o, w_att_o, w_out, g, b)


PEER_EXPERT_ROWS = 16
PEER_ROW_TILE = 32


def _top_values(cur, k):
    rows = []
    for _ in range(k):
        m = jnp.max(cur, axis=0, keepdims=True)
        rows.append(m)
        cur = jnp.where(cur == m, -INF, cur)
    return rows


def _peer_route_kernel(xt_ref, wqt_ref, sk_ref, c_ref, a_ref, s2_ref, e2_ref):
    xt = xt_ref[...]
    half = PEER_DKEY // 2
    k1 = PEER_TOPK + 1
    pairs = [(a, b) for a in range(k1) for b in range(k1 // (a + 1))]
    for h in range(PEER_HEADS):
        qt = _dot(wqt_ref[h * PEER_DKEY:(h + 1) * PEER_DKEY, :], xt).astype(BF16)
        s1_ref = c_ref.at[h]
        s1_ref[...] = _dot(sk_ref[h, 0], qt[:half])
        s2_ref[h] = _dot(sk_ref[h, 1], qt[half:])
        for lc in range(xt.shape[1] // LANES):
            ls = slice(lc * LANES, (lc + 1) * LANES)
            s1 = s1_ref[:, ls]
            s2 = s2_ref[h, :, ls]
            v1 = _top_values(s1, k1)
            v2 = _top_values(s2, k1)
            rows = [v1[a] + v2[b] for a, b in pairs]
            rows += [jnp.full_like(rows[0], -INF)] * ((-len(rows)) % 8)
            cand = jnp.concatenate(rows, axis=0)
            best = _top_values(cand, k1)
            theta = 0.5 * (best[PEER_TOPK - 1] + best[PEER_TOPK])
            z = jnp.sum(jnp.where(cand > theta, jnp.exp(cand - best[0]), 0.0), axis=0, keepdims=True)
            c_ref[h, :, ls] = theta - s1
            a_ref[h, :, ls] = jnp.exp(s1 - v1[0]) / z
            e2_ref[h, :, ls] = jnp.exp(s2 - v2[0])


def _peer_dense_kernel(xt_ref, c_ref, a_ref, s2_ref, e2_ref, u_ref, vt_ref, o_ref, act_ref, gm_ref):
    e = pl.program_id(1)
    nk = PEER_NKEYS
    nsub = PEER_EXPERT_ROWS
    tt = xt_ref.shape[1]

    @pl.when(e == 0)
    def _():
        o_ref[...] = jnp.zeros(o_ref.shape, F32)

    def activations(rows, ls):
        act = _dot(u_ref[rows, :], xt_ref[:, ls])
        act_ref[rows, ls] = 0.5 * act * (1.0 + lax.erf(act * (2.0 ** -0.5)))

    def weights_tile(lc, rt):
        ls = slice(lc * LANES, (lc + 1) * LANES)
        rs = slice(rt * PEER_ROW_TILE, (rt + 1) * PEER_ROW_TILE)
        g = [jnp.zeros((PEER_ROW_TILE, LANES), F32)] * nsub
        for h in range(PEER_HEADS):
            s2 = s2_ref[h, rs, ls]
            e2 = e2_ref[h, rs, ls]
            for c in range(nsub):
                g[c] = g[c] + jnp.where(s2 > c_ref[h, c:c + 1, ls], e2 * a_ref[h, c:c + 1, ls], 0.0)
        for c in range(nsub):
            er = slice(c * nk + rt * PEER_ROW_TILE, c * nk + (rt + 1) * PEER_ROW_TILE)
            gm_ref[er, ls] = (g[c] * act_ref[er, ls]).astype(BF16)

    def outputs(rows, ls):
        o_ref[rows, ls] += _dot(vt_ref[rows, :], gm_ref[:, ls])

    nrt = nk // PEER_ROW_TILE
    nlc = tt // LANES
    if nlc % 2:
        activations(slice(None), slice(None))
        for lc in range(nlc):
            for rt in range(nrt):
                weights_tile(lc, rt)
        outputs(slice(None), slice(None))
        return

    halves = [slice(0, tt // 2), slice(tt // 2, tt)]
    npiece = 4
    arows = [slice(q * (nsub * nk // npiece), (q + 1) * (nsub * nk // npiece)) for q in range(npiece)]
    orows = [slice(q * (o_ref.shape[0] // npiece), (q + 1) * (o_ref.shape[0] // npiece)) for q in range(npiece)]
    tiles = [[(lc, rt) for lc in range(hf * nlc // 2, (hf + 1) * nlc // 2) for rt in range(nrt)] for hf in range(2)]
    per = len(tiles[0]) // npiece
    for q in range(npiece):
        activations(arows[q], halves[0])
    for q in range(npiece):
        activations(arows[q], halves[1])
        for lc, rt in tiles[0][q * per:(q + 1) * per]:
            weights_tile(lc, rt)
    for q in range(npiece):
        outputs(orows[q], halves[0])
        for lc, rt in tiles[1][q * per:(q + 1) * per]:
            weights_tile(lc, rt)
    for q in range(npiece):
        outputs(orows[q], halves[1])


def _peer(x1t, wqt, subkeys, u, vt, tt):
    dm, n = x1t.shape
    assert n % tt == 0 and tt % LANES == 0
    nt = n // tt
    nk = PEER_NKEYS
    nsub = PEER_EXPERT_ROWS
    hshape = (PEER_HEADS, nk, n)
    hspec = pl.BlockSpec((PEER_HEADS, nk, tt), lambda i: (0, 0, i))
    c, a, s2, e2 = pl.pallas_call(
        _peer_route_kernel,
        grid=(nt,),
        in_specs=[pl.BlockSpec((dm, tt), lambda i: (0, i)),
                  pl.BlockSpec(wqt.shape, lambda i: (0, 0)),
                  pl.BlockSpec(subkeys.shape, lambda i: (0, 0, 0, 0))],
        out_specs=[hspec] * 4,
        out_shape=[jax.ShapeDtypeStruct(hshape, F32)] * 4,
        compiler_params=_params(("parallel",)),
        name="peer_route",
    )(x1t, wqt, subkeys)

    full_spec = pl.BlockSpec((PEER_HEADS, nk, tt), lambda i, e: (0, 0, i))
    row_spec = pl.BlockSpec((PEER_HEADS, nsub, tt), lambda i, e: (0, e, i))
    return pl.pallas_call(
        _peer_dense_kernel,
        grid=(nt, nk // nsub),
        in_specs=[pl.BlockSpec((dm, tt), lambda i, e: (0, i)),
                  row_spec, row_spec, full_spec, full_spec,
                  pl.BlockSpec((nsub * nk, dm), lambda i, e: (e, 0)),
                  pl.BlockSpec((dm, nsub * nk), lambda i, e: (0, e))],
        out_specs=pl.BlockSpec((dm, tt), lambda i, e: (0, i)),
        out_shape=jax.ShapeDtypeStruct((dm, n), F32),
        scratch_shapes=[pltpu.VMEM((nsub * nk, tt), F32), pltpu.VMEM((nsub * nk, tt), BF16)],
        compiler_params=_params(("parallel", "arbitrary")),
        name="peer_dense",
    )(x1t, c, a, s2, e2, u, vt)


def _final_kernel(x1_ref, pt_ref, p_ref, wg_ref, wp_ref, g_ref, b_ref, o_ref, *, alpha):
    x1 = x1_ref[...]
    ple = jax.nn.sigmoid(_dot(x1.astype(BF16), wg_ref[...])) * _dot(p_ref[...], wp_ref[...])
    y = alpha * x1 + pt_ref[...].T + ple
    o_ref[...] = _layer_norm(y, g_ref[...], b_ref[...])


def _final(x1, peer_t, p_emb, w_gate, w_ple, g, b, tm, alpha):
    n, dm = x1.shape
    pd = p_emb.shape[1]

    def full(shape):
        return pl.BlockSpec(shape, lambda i: (0,) * len(shape))

    return pl.pallas_call(
        functools.partial(_final_kernel, alpha=alpha),
        grid=(n // tm,),
        in_specs=[pl.BlockSpec((tm, dm), lambda i: (i, 0)),
                  pl.BlockSpec((dm, tm), lambda i: (0, i)),
                  pl.BlockSpec((tm, pd), lambda i: (i, 0)),
                  full((dm, dm)), full((pd, dm)), full((1, dm)), full((1, dm))],
        out_specs=pl.BlockSpec((tm, dm), lambda i: (i, 0)),
        out_shape=jax.ShapeDtypeStruct((n, dm), F32),
        compiler_params=_params(("parallel",)),
        name="ple_ln2",
    )(x1, peer_t, p_emb, w_gate, w_ple, g, b)


def _layer(x, p_emb, pos_rows, tm, weights, attention_fn, state_in, chunk, chunk_eff, alpha):
    n = x.shape[0]
    nb = state_in.shape[0]
    t_seq = n // nb
    pr = _project(x.astype(BF16), weights["w_in"], weights["ikg"], weights["ikb"], pos_rows, tm)

    def pad_chunks(a):
        if t_seq % chunk == 0:
            return a
        a = a.reshape(nb, t_seq, a.shape[-1])
        return jnp.pad(a, ((0, 0), (0, chunk - t_seq), (0, 0))).reshape(nb * chunk, a.shape[-1])

    ret_o, s_out = _retention(pad_chunks(pr["rq"]), pad_chunks(pr["rk"]), pad_chunks(pr["rv"]),
                              pad_chunks(pr["rg"]), weights["gn_g"], weights["gn_b"],
                              state_in, chunk, chunk_eff)
    if t_seq % chunk != 0:
        ret_o = ret_o.reshape(nb, chunk, -1)[:, :t_seq].reshape(n, -1)
    att_o = attention_fn(pr)
    x1, x1t = _mix(x, ret_o, att_o, pr["sgab"], weights["w_ret_o"], weights["w_att_o"], weights["w_out"],
                   weights["ln1_g"], weights["ln1_b"], tm, alpha)
    peer_t = _peer(x1t, weights["peer_wqt"], weights["peer_subkeys"], weights["peer_u"], weights["peer_vt"],
                   min(PEER_TOKEN_TILE, n))
    y = _final(x1, peer_t, p_emb.astype(BF16), weights["w_ple_gate"], weights["w_ple"],
               weights["ln2_g"], weights["ln2_b"], tm, alpha)
    return y, pr["ak"], pr["av"], pr["ik"], s_out


def kernel(x_prompt, x_sample, cache_k, cache_v, cache_kidx, state_ret, page_table, p_prompt, p_sample,
           w_in, idx_k_g, idx_k_b, gn_g, gn_b, w_ret_o, w_att_o, w_out, ln1_g, ln1_b, peer_wq,
           peer_subkeys, peer_u, peer_v, w_ple_gate, w_ple, ln2_g, ln2_b):
    depth = w_in.shape[0]
    assert depth == 1, "single-layer trunk"
    nbp, seq, dm = x_prompt.shape
    nbs, t_new, _ = x_sample.shape
    past_len = page_table.shape[1] * cache_k.shape[2]
    alpha = (2.0 * depth) ** 0.25
    i = 0
    weights = dict(
        w_in=w_in[i].astype(BF16),
        ikg=jnp.concatenate([idx_k_g[i], idx_k_g[i]])[None, :],
        ikb=jnp.concatenate([idx_k_b[i], idx_k_b[i]])[None, :],
        gn_g=gn_g[i][None, :], gn_b=gn_b[i][None, :],
        w_ret_o=w_ret_o[i].astype(BF16), w_att_o=w_att_o[i].astype(BF16), w_out=w_out[i].astype(BF16),
        ln1_g=ln1_g[i][None, :], ln1_b=ln1_b[i][None, :],
        peer_wqt=peer_wq[i].T.astype(BF16), peer_subkeys=peer_subkeys[i].astype(BF16),
        peer_u=peer_u[i].astype(BF16), peer_vt=peer_v[i].T.astype(BF16),
        w_ple_gate=w_ple_gate[i].astype(BF16), w_ple=w_ple[i].astype(BF16),
        ln2_g=ln2_g[i][None, :], ln2_b=ln2_b[i][None, :],
    )

    def attn_p(pr):
        return _attention_prompt(pr["aq"], pr["iq"], pr["iw"], pr["akb"], pr["avb"], pr["ikb"], nbp, seq)

    yp, kp, vp, ikp, rp = _layer(
        x_prompt.reshape(nbp * seq, dm), p_prompt[i].reshape(nbp * seq, -1),
        jnp.arange(seq, dtype=jnp.int32), min(TOKEN_TILE, seq), weights, attn_p,
        jnp.zeros((nbp,) + state_ret.shape[2:], F32), RET_CHUNK, RET_CHUNK, alpha)

    def attn_s(pr):
        return _attention_sample(pr["aq"], pr["iq"], pr["iw"], pr["ak"], pr["av"], pr["ik"],
                                 cache_k[i], cache_v[i], cache_kidx[i], page_table, nbs, t_new)

    ns = nbs * t_new
    pos_s = jnp.tile(past_len + jnp.arange(t_new, dtype=jnp.int32), nbs)
    ys, ks, vs, iks, rs = _layer(
        x_sample.reshape(ns, dm), p_sample[i].reshape(ns, -1), pos_s, ns, weights, attn_s,
        state_ret[i].astype(F32), BF16_SUBLANES, t_new, alpha)

    return (yp.reshape(nbp, seq, dm), ys.reshape(nbs, t_new, dm),
            kp.reshape(1, nbp, seq, ATT_KV_HEADS, ATT_HD), vp.reshape(1, nbp, seq, ATT_KV_HEADS, ATT_HD),
            ikp.reshape(1, nbp, seq, IDX_HD), rp[None].astype(state_ret.dtype),
            ks.reshape(1, nbs, t_new, ATT_KV_HEADS, ATT_HD), vs.reshape(1, nbs, t_new, ATT_KV_HEADS, ATT_HD),
            iks.reshape(1, nbs, t_new, IDX_HD), rs[None].astype(state_ret.dtype))
```

```python
import functools

import jax
import jax.numpy as jnp
from jax import lax
from jax.experimental import pallas as pl
from jax.experimental.pallas import tpu as pltpu

F32 = jnp.float32
BF16 = jnp.bfloat16

RET_HEADS = 4
RET_DK = 256
RET_DV = 512
RET_CHUNK = 256
ATT_HEADS = 8
ATT_KV_HEADS = 2
ATT_HD = 128
IDX_HEADS = 8
IDX_HD = 64
TOPK_ATTN = 256
ROPE_THETA = 10000.0
PEER_HEADS = 8
PEER_NKEYS = 128
PEER_DKEY = 256
PEER_TOPK = 16
LN_EPS = 1e-5
NEG = -1e30
INF = float("inf")

LANES = 128
BF16_SUBLANES = 16
VMEM_LIMIT = 48 * 1024 * 1024
BISECT_ITERS = 32
SAMPLE_PAGES_PER_STEP = 32
SAMPLE_SCORE_PAGES_PER_STEP = 64
TOKEN_TILE = 512
PROJ_COL_TILE = 1024
ATTN_QUERY_TILE = 256
PEER_TOKEN_TILE = 512


def _dot(a, b):
    return jnp.dot(a, b, preferred_element_type=F32)


def _dot_nt(a, b):
    return lax.dot_general(a, b, (((1,), (1,)), ((), ())), preferred_element_type=F32)


def _dot_tn(a, b):
    return lax.dot_general(a, b, (((0,), (0,)), ((), ())), preferred_element_type=F32)


def _params(sem):
    return pltpu.CompilerParams(dimension_semantics=sem, vmem_limit_bytes=VMEM_LIMIT)


def _rope_tables(pos, d, width):
    half = d // 2
    inv = ROPE_THETA ** (-jnp.arange(half, dtype=F32) / half)
    ang = pos.astype(F32)[:, None] * inv[None, :]
    cos, sin = jnp.cos(ang), jnp.sin(ang)
    cosf = jnp.concatenate([cos, cos], axis=-1)
    sinf = jnp.concatenate([-sin, sin], axis=-1)
    reps = width // d
    return jnp.tile(cosf, (1, reps)), jnp.tile(sinf, (1, reps))


def _partner(y, d):
    if d == 2 * LANES:
        return jnp.concatenate([y[:, LANES:], y[:, :LANES]], axis=-1)
    if d == LANES:
        return pltpu.roll(y, LANES // 2, axis=1)
    lane = lax.broadcasted_iota(jnp.int32, y.shape, 1)
    first_half = (lane & (d - 1)) < (d // 2)
    return jnp.where(first_half, pltpu.roll(y, LANES - d // 2, axis=1), pltpu.roll(y, d // 2, axis=1))


def _rope(y, cosf, sinf, d):
    return y * cosf + _partner(y, d) * sinf


def _ret_qk_kernel(x_ref, w_ref, cos_ref, sin_ref, q_ref, k_ref):
    x = x_ref[...]
    cosf, sinf = cos_ref[...], sin_ref[...]
    for h in range(2 * RET_HEADS):
        y = _dot(x, w_ref[:, h * RET_DK:(h + 1) * RET_DK])
        r = _rope(y, cosf, sinf, RET_DK)
        if h < RET_HEADS:
            q_ref[:, h * RET_DK:(h + 1) * RET_DK] = r.astype(BF16)
        else:
            hh = h - RET_HEADS
            k_ref[:, hh * RET_DK:(hh + 1) * RET_DK] = (r * (RET_DK ** -0.5)).astype(BF16)


def _plain_kernel(x_ref, w_ref, o_ref, *, sigmoid):
    y = _dot(x_ref[...], w_ref[...])
    if sigmoid:
        y = jax.nn.sigmoid(y)
    o_ref[...] = y.astype(o_ref.dtype)


def _att_proj_kernel(x_ref, wq_ref, wk_ref, wv_ref, cos_ref, sin_ref,
                     q_ref, k_ref, v_ref, kb_ref, vb_ref):
    x = x_ref[...]
    cosf, sinf = cos_ref[...], sin_ref[...]
    for h in range(ATT_HEADS):
        sl = slice(h * ATT_HD, (h + 1) * ATT_HD)
        q_ref[:, sl] = _rope(_dot(x, wq_ref[:, sl]), cosf, sinf, ATT_HD).astype(BF16)
    tm = x.shape[0]
    v = _dot(x, wv_ref[...])
    vb_ref[...] = v.astype(BF16)
    for h in range(ATT_KV_HEADS):
        sl = slice(h * ATT_HD, (h + 1) * ATT_HD)
        k = _rope(_dot(x, wk_ref[:, sl]), cosf, sinf, ATT_HD)
        kb_ref[:, sl] = k.astype(BF16)
        head_rows = pl.ds(h, tm, stride=ATT_KV_HEADS)
        k_ref[head_rows, :] = k
        v_ref[head_rows, :] = v[:, sl]


def _idx_proj_kernel(x_ref, wq_ref, wk_ref, ww_ref, cos_ref, sin_ref, g_ref, b_ref,
                     q_ref, k_ref, kb_ref, w_ref):
    x = x_ref[...]
    cosf, sinf = cos_ref[...], sin_ref[...]
    for c in range(IDX_HEADS * IDX_HD // LANES):
        sl = slice(c * LANES, (c + 1) * LANES)
        q_ref[:, sl] = _rope(_dot(x, wq_ref[:, sl]), cosf, sinf, IDX_HD).astype(BF16)
    y = _dot(x, wk_ref[...])
    mu = jnp.mean(y, axis=-1, keepdims=True)
    var = jnp.mean(jnp.square(y - mu), axis=-1, keepdims=True)
    yn = (y - mu) * lax.rsqrt(var + LN_EPS) * g_ref[...] + b_ref[...]
    k = yn * cosf + pltpu.roll(yn, IDX_HD // 2, axis=1) * sinf
    k_ref[...] = k[:, :IDX_HD]
    kb_ref[...] = k[:, :IDX_HD].astype(BF16)
    w = _dot(x, ww_ref[...])
    w_ref[...] = w[:, :IDX_HEADS] * (IDX_HEADS ** -0.5)


def _project(xb, w, ikg, ikb, pos_rows, tm):
    n, dm = xb.shape
    nt = n // tm
    rows = pos_rows.shape[0]
    tab_blocks = rows // tm
    hq = RET_HEADS * RET_DK
    hv = RET_HEADS * RET_DV
    o = 0
    w_rqk = w[:, o:o + 2 * hq]; o += 2 * hq
    w_rv = w[:, o:o + hv]; o += hv
    w_rg = w[:, o:o + hv]; o += hv
    w_aq = w[:, o:o + ATT_HEADS * ATT_HD]; o += ATT_HEADS * ATT_HD
    w_ak = w[:, o:o + ATT_KV_HEADS * ATT_HD]; o += ATT_KV_HEADS * ATT_HD
    w_av = w[:, o:o + ATT_KV_HEADS * ATT_HD]; o += ATT_KV_HEADS * ATT_HD
    w_iq = w[:, o:o + IDX_HEADS * IDX_HD]; o += IDX_HEADS * IDX_HD
    w_ik = w[:, o:o + IDX_HD]; o += IDX_HD
    w_iw = w[:, o:o + IDX_HEADS]; o += IDX_HEADS
    w_gab = w[:, o:]

    x_spec = pl.BlockSpec((tm, dm), lambda i: (i, 0))

    def tab_spec(width):
        return pl.BlockSpec((tm, width), lambda i: (i % tab_blocks, 0))

    def full(shape):
        return pl.BlockSpec(shape, lambda i: (0,) * len(shape))

    def row_spec(width):
        return pl.BlockSpec((tm, width), lambda i: (i, 0))

    cos256, sin256 = _rope_tables(pos_rows, RET_DK, RET_DK)
    rq, rk = pl.pallas_call(
        _ret_qk_kernel,
        grid=(nt,),
        in_specs=[x_spec, full((dm, 2 * hq)), tab_spec(RET_DK), tab_spec(RET_DK)],
        out_specs=[row_spec(hq), row_spec(hq)],
        out_shape=[jax.ShapeDtypeStruct((n, hq), BF16)] * 2,
        compiler_params=_params(("parallel",)),
        name="proj_ret_qk",
    )(xb, w_rqk, cos256, sin256)

    def plain(wc, dtype, sigmoid, name):
        cols = wc.shape[1]
        tn = min(PROJ_COL_TILE, cols)
        return pl.pallas_call(
            functools.partial(_plain_kernel, sigmoid=sigmoid),
            grid=(cols // tn, nt),
            in_specs=[pl.BlockSpec((tm, dm), lambda j, i: (i, 0)),
                      pl.BlockSpec((dm, tn), lambda j, i: (0, j))],
            out_specs=pl.BlockSpec((tm, tn), lambda j, i: (i, j)),
            out_shape=jax.ShapeDtypeStruct((n, cols), dtype),
            compiler_params=_params(("parallel", "parallel")),
            name=name,
        )(xb, wc)

    rv = plain(w_rv, BF16, False, "proj_ret_v")
    rg = plain(w_rg, F32, False, "proj_ret_gate")
    sgab = plain(w_gab, F32, True, "proj_branch_gates")

    cos128, sin128 = _rope_tables(pos_rows, ATT_HD, ATT_HD)
    kvw = ATT_KV_HEADS * ATT_HD
    aq, ak, av, akb, avb = pl.pallas_call(
        _att_proj_kernel,
        grid=(nt,),
        in_specs=[x_spec, full((dm, ATT_HEADS * ATT_HD)), full((dm, kvw)), full((dm, kvw)),
                  tab_spec(ATT_HD), tab_spec(ATT_HD)],
        out_specs=[row_spec(ATT_HEADS * ATT_HD),
                   pl.BlockSpec((tm * ATT_KV_HEADS, ATT_HD), lambda i: (i, 0)),
                   pl.BlockSpec((tm * ATT_KV_HEADS, ATT_HD), lambda i: (i, 0)),
                   row_spec(kvw), row_spec(kvw)],
        out_shape=[jax.ShapeDtypeStruct((n, ATT_HEADS * ATT_HD), BF16),
                   jax.ShapeDtypeStruct((n * ATT_KV_HEADS, ATT_HD), F32),
                   jax.ShapeDtypeStruct((n * ATT_KV_HEADS, ATT_HD), F32),
                   jax.ShapeDtypeStruct((n, kvw), BF16), jax.ShapeDtypeStruct((n, kvw), BF16)],
        compiler_params=_params(("parallel",)),
        name="proj_att",
    )(xb, w_aq, w_ak, w_av, cos128, sin128)

    cos64, sin64 = _rope_tables(pos_rows, IDX_HD, LANES)
    w_ik2 = jnp.concatenate([w_ik, w_ik], axis=1)
    w_iw_pad = jnp.pad(w_iw, ((0, 0), (0, LANES - IDX_HEADS)))
    iq, ik, ikb, iw = pl.pallas_call(
        _idx_proj_kernel,
        grid=(nt,),
        in_specs=[x_spec, full((dm, IDX_HEADS * IDX_HD)), full((dm, LANES)), full((dm, LANES)),
                  tab_spec(LANES), tab_spec(LANES), full((1, LANES)), full((1, LANES))],
        out_specs=[row_spec(IDX_HEADS * IDX_HD), row_spec(IDX_HD), row_spec(IDX_HD), row_spec(IDX_HEADS)],
        out_shape=[jax.ShapeDtypeStruct((n, IDX_HEADS * IDX_HD), BF16),
                   jax.ShapeDtypeStruct((n, IDX_HD), F32), jax.ShapeDtypeStruct((n, IDX_HD), BF16),
                   jax.ShapeDtypeStruct((n, IDX_HEADS), F32)],
        compiler_params=_params(("parallel",)),
        name="proj_idx",
    )(xb, w_iq, w_ik2, w_iw_pad, cos64, sin64, ikg, ikb)
    return dict(rq=rq, rk=rk, rv=rv, rg=rg, sgab=sgab, aq=aq, ak=ak, av=av, akb=akb, avb=avb,
                iq=iq, ik=ik, ikb=ikb, iw=iw)


def _retention_kernel(sdec_ref, q_ref, k_ref, v_ref, gate_ref, gg_ref, gb_ref,
                      d_ref, cdec_ref, kdec_ref, si_ref, o_ref, so_ref):
    h = pl.program_id(1)
    c = pl.program_id(2)

    @pl.when(c == 0)
    def _():
        so_ref[0, 0] = si_ref[0, 0]

    q = q_ref[...]
    k = k_ref[...]
    v = v_ref[...]
    state = so_ref[0, 0]
    s = _dot_nt(q, k)
    inner = _dot((s * d_ref[0]).astype(BF16), v)
    cross = _dot(q, state.astype(BF16)) * cdec_ref[0]
    o = inner + cross
    kd = (k.astype(F32) * kdec_ref[0]).astype(BF16)
    so_ref[0, 0] = state * sdec_ref[h] + _dot_tn(kd, v)
    mu = jnp.mean(o, axis=-1, keepdims=True)
    var = jnp.mean(jnp.square(o - mu), axis=-1, keepdims=True)
    on = (o - mu) * lax.rsqrt(var + LN_EPS) * gg_ref[...] + gb_ref[...]
    o_ref[...] = (jax.nn.silu(gate_ref[...]) * on).astype(o_ref.dtype)


def _retention(rq, rk, rv, rg, gn_g, gn_b, state_in, chunk, chunk_eff):
    n = rq.shape[0]
    nb = state_in.shape[0]
    nc = n // (nb * chunk)
    lg = jnp.log1p(-jnp.exp2(-5.0 - jnp.arange(RET_HEADS, dtype=F32)))[:, None]
    i = jnp.arange(chunk, dtype=F32)
    diff = i[:, None] - i[None, :]
    decay = jnp.where(diff >= 0, jnp.exp(lg[:, :, None] * jnp.maximum(diff, 0.0)), 0.0)
    cdec = jnp.exp(lg * (i + 1.0))[:, :, None]
    kdec = jnp.where(i < chunk_eff, jnp.exp(lg * jnp.maximum(chunk_eff - 1.0 - i, 0.0)), 0.0)[:, :, None]
    sdec = jnp.exp(lg[:, 0] * chunk_eff)

    o, s_out = pl.pallas_call(
        _retention_kernel,
        grid=(nb, RET_HEADS, nc),
        in_specs=[
            pl.BlockSpec(memory_space=pltpu.SMEM),
            pl.BlockSpec((chunk, RET_DK), lambda b, h, c: (b * nc + c, h)),
            pl.BlockSpec((chunk, RET_DK), lambda b, h, c: (b * nc + c, h)),
            pl.BlockSpec((chunk, RET_DV), lambda b, h, c: (b * nc + c, h)),
            pl.BlockSpec((chunk, RET_DV), lambda b, h, c: (b * nc + c, h)),
            pl.BlockSpec((1, RET_DV), lambda b, h, c: (0, h)),
            pl.BlockSpec((1, RET_DV), lambda b, h, c: (0, h)),
            pl.BlockSpec((1, chunk, chunk), lambda b, h, c: (h, 0, 0)),
            pl.BlockSpec((1, chunk, 1), lambda b, h, c: (h, 0, 0)),
            pl.BlockSpec((1, chunk, 1), lambda b, h, c: (h, 0, 0)),
            pl.BlockSpec((1, 1, RET_DK, RET_DV), lambda b, h, c: (b, h, 0, 0)),
        ],
        out_specs=[
            pl.BlockSpec((chunk, RET_DV), lambda b, h, c: (b * nc + c, h)),
            pl.BlockSpec((1, 1, RET_DK, RET_DV), lambda b, h, c: (b, h, 0, 0)),
        ],
        out_shape=[jax.ShapeDtypeStruct((n, RET_HEADS * RET_DV), BF16),
                   jax.ShapeDtypeStruct(state_in.shape, F32)],
        compiler_params=_params(("parallel", "parallel", "arbitrary")),
        name="retention",
    )(sdec, rq, rk, rv, rg, gn_g, gn_b, decay, cdec, kdec, state_in)
    return o, s_out


M_INIT = -1e29


def _lanes(col):
    return jnp.broadcast_to(col, (col.shape[0], LANES))


def _kth_largest(reduce_tiles, mn, mx, kq):
    kq_b = _lanes(kq)

    def bisect_body(_, carry):
        lo, hi = carry
        mid = 0.5 * (lo + hi)
        cnt = reduce_tiles(lambda a, t: a + jnp.where(t >= mid, 1.0, 0.0), 0.0)
        ge = _lanes(jnp.sum(cnt, axis=-1, keepdims=True)) >= kq_b
        return jnp.where(ge, mid, lo), jnp.where(ge, hi, mid)

    lo, _ = lax.fori_loop(0, BISECT_ITERS, bisect_body, (_lanes(mn), _lanes(mx + 1.0)))
    tau = jnp.min(reduce_tiles(lambda a, t: jnp.minimum(a, jnp.where(t >= lo, t, INF)), INF),
                  axis=-1, keepdims=True)
    tau_b = _lanes(tau)
    cgt = jnp.sum(reduce_tiles(lambda a, t: a + jnp.where(t > tau_b, 1.0, 0.0), 0.0),
                  axis=-1, keepdims=True)
    return tau, kq - cgt


def _softmax_step(s, m_ref, l_ref, acc_ref, idx, v):
    m_old = m_ref[idx]
    m_new = jnp.maximum(m_old, jnp.max(s, axis=-1, keepdims=True))
    p = jnp.exp(s - m_new)
    alpha = jnp.exp(m_old - m_new)
    l_ref[idx] = alpha * l_ref[idx] + jnp.sum(p, axis=-1, keepdims=True)
    acc_ref[idx] = alpha * acc_ref[idx] + _dot(p.astype(BF16), v)
    m_ref[idx] = m_new


def _fold_rows(x):
    return jnp.sum(x.reshape(x.shape[0] // 8, 8, x.shape[1]), axis=0)


def _attn_prompt_kernel(aqt_ref, iqt_ref, iwt_ref, k_ref, vt_ref, kidx_ref, tril_ref, o_ref,
                        sc_ref, m_ref, l_ref, acc_ref, *, tq, topk):
    i = pl.program_id(1)
    nkv = i + 1
    qpos = i * tq + lax.broadcasted_iota(jnp.int32, (1, tq), 1)
    kq = jnp.minimum(topk, qpos + 1).astype(F32)
    group = ATT_HEADS // ATT_KV_HEADS

    def score_body(j, carry):
        mn, mx = carry
        kx = kidx_ref[pl.ds(pl.multiple_of(j * tq, tq), tq), :]
        s = jnp.zeros((tq, tq), F32)
        for h in range(IDX_HEADS):
            d = _dot(kx, iqt_ref[h * IDX_HD:(h + 1) * IDX_HD, :]) * (IDX_HD ** -0.5)
            s = s + iwt_ref[h:h + 1, :] * jnp.maximum(d, 0.0)
        kpos = j * tq + lax.broadcasted_iota(jnp.int32, (tq, tq), 0)
        vis = kpos <= qpos
        sc_ref[j] = jnp.where(vis, s, -INF)
        mn = jnp.minimum(mn, jnp.min(jnp.where(vis, s, INF), axis=0, keepdims=True))
        mx = jnp.maximum(mx, jnp.max(jnp.where(vis, s, -INF), axis=0, keepdims=True))
        return mn, mx

    mn, mx = lax.fori_loop(0, nkv, score_body,
                           (jnp.full((1, tq), INF, F32), jnp.full((1, tq), -INF, F32)))

    def count(pred):
        def body(j, acc):
            return acc + _fold_rows(jnp.where(pred(sc_ref[j]), 1.0, 0.0))
        return jnp.sum(lax.fori_loop(0, nkv, body, jnp.zeros((8, tq), F32)), axis=0, keepdims=True)

    def bisect_body(_, carry):
        lo, hi = carry
        mid = 0.5 * (lo + hi)
        ge = count(lambda t: t >= mid) >= kq
        return jnp.where(ge, mid, lo), jnp.where(ge, hi, mid)

    lo, _ = lax.fori_loop(0, BISECT_ITERS, bisect_body, (mn, mx + 1.0))

    def snap_body(j, acc):
        t = sc_ref[j]
        return jnp.minimum(acc, jnp.min(jnp.where(t >= lo, t, INF), axis=0, keepdims=True))

    tau = lax.fori_loop(0, nkv, snap_body, jnp.full((1, tq), INF, F32))
    need = kq - count(lambda t: t > tau)

    tril = tril_ref[...]

    def mask_body(j, run):
        t = sc_ref[j]
        eqf = jnp.where(t == tau, 1.0, 0.0)
        prefix = _dot(tril, eqf.astype(BF16))
        sel = (t > tau) | ((t == tau) & ((run + prefix) <= need))
        sc_ref[j] = jnp.where(sel, 0.0, NEG)
        return run + jnp.sum(eqf, axis=0, keepdims=True)

    lax.fori_loop(0, nkv, mask_body, jnp.zeros((1, tq), F32))

    m_ref[...] = jnp.full(m_ref.shape, M_INIT, F32)
    l_ref[...] = jnp.zeros(l_ref.shape, F32)
    acc_ref[...] = jnp.zeros(acc_ref.shape, F32)

    def att_body(j, carry):
        rows = pl.ds(pl.multiple_of(j * tq, tq), tq)
        for h in range(ATT_HEADS):
            g = h // group
            kg = k_ref[rows, g * ATT_HD:(g + 1) * ATT_HD]
            vg = vt_ref[0, j, g * ATT_HD:(g + 1) * ATT_HD, :]
            for c in range(tq // LANES):
                ls = slice(c * LANES, (c + 1) * LANES)
                s = _dot(kg, aqt_ref[h * ATT_HD:(h + 1) * ATT_HD, ls]) * (ATT_HD ** -0.5) + sc_ref[j, :, ls]
                m_old = m_ref[h, :, ls]
                m_new = jnp.maximum(m_old, jnp.max(s, axis=0, keepdims=True))
                p = jnp.exp(s - m_new)
                alpha = jnp.exp(m_old - m_new)
                l_ref[h, :, ls] = alpha * l_ref[h, :, ls] + jnp.sum(p, axis=0, keepdims=True)
                acc_ref[h, :, ls] = alpha * acc_ref[h, :, ls] + _dot(vg, p.astype(BF16))
                m_ref[h, :, ls] = m_new
        return carry

    lax.fori_loop(0, nkv, att_body, 0)
    for h in range(ATT_HEADS):
        o_ref[h * ATT_HD:(h + 1) * ATT_HD, :] = (acc_ref[h] / l_ref[h]).astype(o_ref.dtype)


def _attention_prompt(aq, iq, iw, akb, avb, ikb, nb, seq):
    tq = min(ATTN_QUERY_TILE, seq)
    assert seq % tq == 0 and tq % LANES == 0
    nq = seq // tq
    topk = min(TOPK_ATTN, seq // 4)
    n = aq.shape[0]
    tril = jnp.tril(jnp.ones((tq, tq), F32)).astype(BF16)
    kvw = ATT_KV_HEADS * ATT_HD
    vt = avb.reshape(nb, nq, tq, kvw).transpose(0, 1, 3, 2)
    ot = pl.pallas_call(
        functools.partial(_attn_prompt_kernel, tq=tq, topk=topk),
        grid=(nb, nq),
        in_specs=[
            pl.BlockSpec((ATT_HEADS * ATT_HD, tq), lambda b, i: (0, b * nq + i)),
            pl.BlockSpec((IDX_HEADS * IDX_HD, tq), lambda b, i: (0, b * nq + i)),
            pl.BlockSpec((IDX_HEADS, tq), lambda b, i: (0, b * nq + i)),
            pl.BlockSpec((seq, kvw), lambda b, i: (b, 0)),
            pl.BlockSpec((1, nq, kvw, tq), lambda b, i: (b, 0, 0, 0)),
            pl.BlockSpec((seq, IDX_HD), lambda b, i: (b, 0)),
            pl.BlockSpec((tq, tq), lambda b, i: (0, 0)),
        ],
        out_specs=pl.BlockSpec((ATT_HEADS * ATT_HD, tq), lambda b, i: (0, b * nq + i)),
        out_shape=jax.ShapeDtypeStruct((ATT_HEADS * ATT_HD, n), BF16),
        scratch_shapes=[
            pltpu.VMEM((nq, tq, tq), F32),
            pltpu.VMEM((ATT_HEADS, 1, tq), F32),
            pltpu.VMEM((ATT_HEADS, 1, tq), F32),
            pltpu.VMEM((ATT_HEADS, ATT_HD, tq), F32),
        ],
        compiler_params=_params(("parallel", "arbitrary")),
        name="attention_prompt",
    )(aq.T, iq.T, iw.T, akb, vt, ikb, tril)
    return ot.T


def _page_scores(iq_ref, iw_ref, kxt):
    tp = iq_ref.shape[1] // IDX_HEADS
    d = _dot(iq_ref[0], kxt) * (IDX_HD ** -0.5)
    d = iw_ref[0] * jnp.maximum(d, 0.0)
    s = d[:tp]
    for h in range(1, IDX_HEADS):
        s = s + d[h * tp:(h + 1) * tp]
    return s


def _sample_scores_kernel(pt_ref, iq_ref, iw_ref, *refs, npp, n_new, topk):
    page_refs = refs[:npp]
    knew_ref = refs[npp]
    sc_ref, tau_ref, need_ref = refs[npp + 1:]
    s = pl.program_id(1)
    npages = sc_ref.shape[1] - 1
    tp, page = sc_ref.shape[2], sc_ref.shape[3]

    kxt = jnp.concatenate([r[0] for r in page_refs], axis=1).astype(BF16)
    sc = _page_scores(iq_ref, iw_ref, kxt)
    for p in range(npp):
        sc_ref[0, s * npp + p] = sc[:, p * page:(p + 1) * page]

    @pl.when(s == pl.num_programs(1) - 1)
    def _():
        snew = _page_scores(iq_ref, iw_ref, knew_ref[0].astype(BF16))
        t = jnp.minimum(lax.broadcasted_iota(jnp.int32, (tp, page), 0), n_new - 1)
        j = lax.broadcasted_iota(jnp.int32, (tp, page), 1)
        sc_ref[0, npages] = jnp.where(j <= t, snew, -INF)

        def reduce_tiles(fn, init):
            def body(c, acc):
                for p in range(npp):
                    acc = fn(acc, sc_ref[0, c * npp + p])
                return acc
            acc = lax.fori_loop(0, npages // npp, body, jnp.full((tp, page), init, F32))
            return fn(acc, sc_ref[0, npages])

        mn = jnp.min(reduce_tiles(lambda a, x: jnp.minimum(a, jnp.where(x == -INF, INF, x)), INF),
                     axis=-1, keepdims=True)
        mx = jnp.max(reduce_tiles(jnp.maximum, -INF), axis=-1, keepdims=True)
        tau, need = _kth_largest(reduce_tiles, mn, mx, jnp.full((tp, 1), float(topk), F32))
        tau_ref[0] = _lanes(tau)
        need_ref[0] = _lanes(need)


def _sample_attend_kernel(pt_ref, q_ref, sc_ref, tau_ref, need_ref, tri_ref, *refs, npp):
    k_refs = refs[:npp]
    v_refs = refs[npp:2 * npp]
    knew_ref, vnew_ref, o_ref, m_ref, l_ref, acc_ref, run_ref = refs[2 * npp:]
    s = pl.program_id(1)
    npages = sc_ref.shape[1] - 1
    page = sc_ref.shape[3]
    group = ATT_HEADS // ATT_KV_HEADS
    grows = q_ref.shape[1] // ATT_KV_HEADS

    @pl.when(s == 0)
    def _():
        m_ref[...] = jnp.full(m_ref.shape, M_INIT, F32)
        l_ref[...] = jnp.zeros(l_ref.shape, F32)
        acc_ref[...] = jnp.zeros(acc_ref.shape, F32)
        run_ref[...] = jnp.zeros(run_ref.shape, F32)

    tau = tau_ref[0][:, :1]
    need = need_ref[0][:, :1]
    tri = tri_ref[...]

    def attend(pages, krefs, vrefs):
        tp = sc_ref.shape[2]
        scs = [sc_ref[0, pg] for pg in pages]
        eqs = [jnp.where(t == tau, 1.0, 0.0) for t in scs]
        prefix = _dot(jnp.concatenate(eqs, axis=0).astype(BF16), tri)
        run = run_ref[...]
        madds = []
        for k, t in enumerate(scs):
            sel = (t > tau) | ((t == tau) & ((run + prefix[k * tp:(k + 1) * tp]) <= need))
            madds.append(jnp.where(sel, 0.0, NEG))
            run = run + jnp.sum(eqs[k], axis=-1, keepdims=True)
        run_ref[...] = run
        madd = jnp.concatenate(madds, axis=1)
        maddg = jnp.concatenate([madd] * group, axis=0)
        for g in range(ATT_KV_HEADS):
            head_rows = pl.ds(g, page, stride=ATT_KV_HEADS)
            kg = jnp.concatenate([r[0, head_rows, :] for r in krefs], axis=0).astype(BF16)
            vg = jnp.concatenate([r[0, head_rows, :] for r in vrefs], axis=0).astype(BF16)
            sg = _dot_nt(q_ref[0, g * grows:(g + 1) * grows, :], kg) * (ATT_HD ** -0.5) + maddg
            _softmax_step(sg, m_ref, l_ref, acc_ref, g, vg)

    attend([s * npp + p for p in range(npp)], k_refs, v_refs)

    @pl.when(s == pl.num_programs(1) - 1)
    def _():
        attend([npages], [knew_ref], [vnew_ref])
        for g in range(ATT_KV_HEADS):
            o_ref[0, g * grows:(g + 1) * grows, :] = acc_ref[g] / l_ref[g]


def _attention_sample(aq, iq, iw, ak, av, ik, cache_k, cache_v, cache_kidx, page_table, nb, t_new):
    n_phys, page = cache_kidx.shape[0], cache_kidx.shape[1]
    npages = page_table.shape[1]
    npp = SAMPLE_PAGES_PER_STEP if npages % SAMPLE_PAGES_PER_STEP == 0 else 1
    nsteps = npages // npp
    npp_sc = SAMPLE_SCORE_PAGES_PER_STEP if npages % SAMPLE_SCORE_PAGES_PER_STEP == 0 else npp
    tp = BF16_SUBLANES
    topk = min(TOPK_ATTN, (npages * page + t_new) // 4)
    prow = page * ATT_KV_HEADS

    def pad_rows(a, per_seq, rows):
        a = a.reshape(nb, per_seq, a.shape[-1])
        return jnp.pad(a, ((0, 0), (0, rows - per_seq), (0, 0)))

    iq_p = pad_rows(iq, t_new, tp).reshape(nb, tp, IDX_HEADS, IDX_HD).transpose(0, 2, 1, 3)
    iq_p = iq_p.reshape(nb, IDX_HEADS * tp, IDX_HD)
    iw_p = pad_rows(iw, t_new, tp).transpose(0, 2, 1).reshape(nb, IDX_HEADS * tp, 1)
    q_p = pad_rows(aq, t_new, tp).reshape(nb, tp, ATT_HEADS, ATT_HD).transpose(0, 2, 1, 3)
    q_p = q_p.reshape(nb, ATT_HEADS * tp, ATT_HD)
    knew = pad_rows(ak, t_new * ATT_KV_HEADS, prow)
    vnew = pad_rows(av, t_new * ATT_KV_HEADS, prow)
    kinew = pad_rows(ik, t_new, page).swapaxes(1, 2)
    kidx_t = cache_kidx.swapaxes(1, 2)
    ck = cache_k.reshape(n_phys, prow, ATT_HD)
    cv = cache_v.reshape(n_phys, prow, ATT_HD)
    tri = jnp.triu(jnp.ones((page, page), F32)).astype(BF16)

    def page_spec(rows, width, p, per_step=npp):
        return pl.BlockSpec((1, rows, width), lambda b, s, pt, p=p: (pt[b, s * per_step + p], 0, 0))

    def seq_spec(shape):
        return pl.BlockSpec((1,) + shape, lambda b, s, pt: (b,) + (0,) * len(shape))

    sc, tau, need = pl.pallas_call(
        functools.partial(_sample_scores_kernel, npp=npp_sc, n_new=t_new, topk=topk),
        grid_spec=pltpu.PrefetchScalarGridSpec(
            num_scalar_prefetch=1,
            grid=(nb, npages // npp_sc),
            in_specs=[seq_spec((IDX_HEADS * tp, IDX_HD)), seq_spec((IDX_HEADS * tp, 1))]
                     + [page_spec(IDX_HD, page, p, npp_sc) for p in range(npp_sc)] + [seq_spec((IDX_HD, page))],
            out_specs=[seq_spec((npages + 1, tp, page)), seq_spec((tp, page)), seq_spec((tp, page))],
        ),
        out_shape=[jax.ShapeDtypeStruct((nb, npages + 1, tp, page), F32),
                   jax.ShapeDtypeStruct((nb, tp, page), F32),
                   jax.ShapeDtypeStruct((nb, tp, page), F32)],
        compiler_params=_params(("parallel", "arbitrary")),
        name="sample_scores",
    )(page_table, iq_p, iw_p, *([kidx_t] * npp_sc), kinew)

    grows = (ATT_HEADS // ATT_KV_HEADS) * tp
    o = pl.pallas_call(
        functools.partial(_sample_attend_kernel, npp=npp),
        grid_spec=pltpu.PrefetchScalarGridSpec(
            num_scalar_prefetch=1,
            grid=(nb, nsteps),
            in_specs=[seq_spec((ATT_HEADS * tp, ATT_HD)), seq_spec((npages + 1, tp, page)),
                      seq_spec((tp, page)), seq_spec((tp, page)),
                      pl.BlockSpec((page, page), lambda b, s, pt: (0, 0))]
                     + [page_spec(prow, ATT_HD, p) for p in range(npp)]
                     + [page_spec(prow, ATT_HD, p) for p in range(npp)]
                     + [seq_spec((prow, ATT_HD)), seq_spec((prow, ATT_HD))],
            out_specs=seq_spec((ATT_HEADS * tp, ATT_HD)),
            scratch_shapes=[
                pltpu.VMEM((ATT_KV_HEADS, grows, 1), F32),
                pltpu.VMEM((ATT_KV_HEADS, grows, 1), F32),
                pltpu.VMEM((ATT_KV_HEADS, grows, ATT_HD), F32),
                pltpu.VMEM((tp, 1), F32),
            ],
        ),
        out_shape=jax.ShapeDtypeStruct((nb, ATT_HEADS * tp, ATT_HD), F32),
        compiler_params=_params(("parallel", "arbitrary")),
        name="sample_attend",
    )(page_table, q_p, sc, tau, need, tri, *([ck] * npp), *([cv] * npp), knew, vnew)
    o = o.reshape(nb, ATT_HEADS, tp, ATT_HD)[:, :, :t_new].transpose(0, 2, 1, 3)
    return o.reshape(nb * t_new, ATT_HEADS * ATT_HD).astype(BF16)


def _layer_norm(y, g, b):
    mu = jnp.mean(y, axis=-1, keepdims=True)
    var = jnp.mean(jnp.square(y - mu), axis=-1, keepdims=True)
    return (y - mu) * lax.rsqrt(var + LN_EPS) * g + b


def _mix_kernel(x_ref, ret_ref, att_ref, sga_ref, sgb_ref, wr_ref, wa_ref, wo_ref, g_ref, b_ref,
                o_ref, ot_ref, *, alpha):
    branch = sga_ref[...] * _dot(ret_ref[...], wr_ref[...]) + sgb_ref[...] * _dot(att_ref[...], wa_ref[...])
    y = alpha * x_ref[...] + _dot(branch.astype(BF16), wo_ref[...])
    x1 = _layer_norm(y, g_ref[...], b_ref[...])
    o_ref[...] = x1
    ot_ref[...] = x1.T.astype(BF16)


def _mix(x, ret_o, att_o, sgab, w_ret_o, w_att_o, w_out, g, b, tm, alpha):
    n, dm = x.shape
    hv = RET_HEADS * RET_DV
    ha = ATT_HEADS * ATT_HD

    def full(shape):
        return pl.BlockSpec(shape, lambda i: (0,) * len(shape))

    return pl.pallas_call(
        functools.partial(_mix_kernel, alpha=alpha),
        grid=(n // tm,),
        in_specs=[pl.BlockSpec((tm, dm), lambda i: (i, 0)),
                  pl.BlockSpec((tm, hv), lambda i: (i, 0)),
                  pl.BlockSpec((tm, ha), lambda i: (i, 0)),
                  pl.BlockSpec((tm, dm), lambda i: (i, 0)),
                  pl.BlockSpec((tm, dm), lambda i: (i, 1)),
                  full((hv, dm)), full((ha, dm)), full((dm, dm)), full((1, dm)), full((1, dm))],
        out_specs=[pl.BlockSpec((tm, dm), lambda i: (i, 0)),
                   pl.BlockSpec((dm, tm), lambda i: (0, i))],
        out_shape=[jax.ShapeDtypeStruct((n, dm), F32), jax.ShapeDtypeStruct((dm, n), BF16)],
        compiler_params=_params(("parallel",)),
        name="branch_mix_ln1",
    )(x, ret_o, att_o, sgab, sgab, w_ret_o, w_att_o, w_out, g, b)


PEER_EXPERT_ROWS = 16
PEER_ROW_TILE = 32


def _top_values(cur, k):
    rows = []
    for _ in range(k):
        m = jnp.max(cur, axis=0, keepdims=True)
        rows.append(m)
        cur = jnp.where(cur == m, -INF, cur)
    return rows


def _peer_route_kernel(xt_ref, wqt_ref, sk_ref, c_ref, a_ref, s2_ref, e2_ref):
    xt = xt_ref[...]
    half = PEER_DKEY // 2
    k1 = PEER_TOPK + 1
    pairs = [(a, b) for a in range(k1) for b in range(k1 // (a + 1))]
    for h in range(PEER_HEADS):
        qt = _dot(wqt_ref[h * PEER_DKEY:(h + 1) * PEER_DKEY, :], xt).astype(BF16)
        s1_ref = c_ref.at[h]
        s1_ref[...] = _dot(sk_ref[h, 0], qt[:half])
        s2_ref[h] = _dot(sk_ref[h, 1], qt[half:])
        for lc in range(xt.shape[1] // LANES):
            ls = slice(lc * LANES, (lc + 1) * LANES)
            s1 = s1_ref[:, ls]
            s2 = s2_ref[h, :, ls]
            v1 = _top_values(s1, k1)
            v2 = _top_values(s2, k1)
            rows = [v1[a] + v2[b] for a, b in pairs]
            rows += [jnp.full_like(rows[0], -INF)] * ((-len(rows)) % 8)
            cand = jnp.concatenate(rows, axis=0)
            best = _top_values(cand, k1)
            theta = 0.5 * (best[PEER_TOPK - 1] + best[PEER_TOPK])
            z = jnp.sum(jnp.where(cand > theta, jnp.exp(cand - best[0]), 0.0), axis=0, keepdims=True)
            c_ref[h, :, ls] = theta - s1
            a_ref[h, :, ls] = jnp.exp(s1 - v1[0]) / z
            e2_ref[h, :, ls] = jnp.exp(s2 - v2[0])


def _peer_dense_kernel(xt_ref, c_ref, a_ref, s2_ref, e2_ref, u_ref, vt_ref, o_ref, act_ref, gm_ref):
    e = pl.program_id(1)
    nk = PEER_NKEYS
    nsub = PEER_EXPERT_ROWS
    tt = xt_ref.shape[1]

    @pl.when(e == 0)
    def _():
        o_ref[...] = jnp.zeros(o_ref.shape, F32)

    def activations(rows, ls):
        act = _dot(u_ref[rows, :], xt_ref[:, ls])
        act_ref[rows, ls] = 0.5 * act * (1.0 + lax.erf(act * (2.0 ** -0.5)))

    def weights_tile(lc, rt):
        ls = slice(lc * LANES, (lc + 1) * LANES)
        rs = slice(rt * PEER_ROW_TILE, (rt + 1) * PEER_ROW_TILE)
        g = [jnp.zeros((PEER_ROW_TILE, LANES), F32)] * nsub
        for h in range(PEER_HEADS):
            s2 = s2_ref[h, rs, ls]
            e2 = e2_ref[h, rs, ls]
            for c in range(nsub):
                g[c] = g[c] + jnp.where(s2 > c_ref[h, c:c + 1, ls], e2 * a_ref[h, c:c + 1, ls], 0.0)
        for c in range(nsub):
            er = slice(c * nk + rt * PEER_ROW_TILE, c * nk + (rt + 1) * PEER_ROW_TILE)
            gm_ref[er, ls] = (g[c] * act_ref[er, ls]).astype(BF16)

    def outputs(rows, ls):
        o_ref[rows, ls] += _dot(vt_ref[rows, :], gm_ref[:, ls])

    nrt = nk // PEER_ROW_TILE
    nlc = tt // LANES
    if nlc % 2:
        activations(slice(None), slice(None))
        for lc in range(nlc):
            for rt in range(nrt):
                weights_tile(lc, rt)
        outputs(slice(None), slice(None))
        return

    halves = [slice(0, tt // 2), slice(tt // 2, tt)]
    npiece = 8
    arows = [slice(q * (nsub * nk // npiece), (q + 1) * (nsub * nk // npiece)) for q in range(npiece)]
    orows = [slice(q * (o_ref.shape[0] // npiece), (q + 1) * (o_ref.shape[0] // npiece)) for q in range(npiece)]
    tiles = [[(lc, rt) for lc in range(hf * nlc // 2, (hf + 1) * nlc // 2) for rt in range(nrt)] for hf in range(2)]
    per = len(tiles[0]) // npiece
    for q in range(npiece):
        activations(arows[q], halves[0])
    for q in range(npiece):
        activations(arows[q], halves[1])
        for lc, rt in tiles[0][q * per:(q + 1) * per]:
            weights_tile(lc, rt)
    for q in range(npiece):
        outputs(orows[q], halves[0])
        for lc, rt in tiles[1][q * per:(q + 1) * per]:
            weights_tile(lc, rt)
    for q in range(npiece):
        outputs(orows[q], halves[1])


def _peer(x1t, wqt, subkeys, u, vt, tt):
    dm, n = x1t.shape
    assert n % tt == 0 and tt % LANES == 0
    nt = n // tt
    nk = PEER_NKEYS
    nsub = PEER_EXPERT_ROWS
    hshape = (PEER_HEADS, nk, n)
    hspec = pl.BlockSpec((PEER_HEADS, nk, tt), lambda i: (0, 0, i))
    c, a, s2, e2 = pl.pallas_call(
        _peer_route_kernel,
        grid=(nt,),
        in_specs=[pl.BlockSpec((dm, tt), lambda i: (0, i)),
                  pl.BlockSpec(wqt.shape, lambda i: (0, 0)),
                  pl.BlockSpec(subkeys.shape, lambda i: (0, 0, 0, 0))],
        out_specs=[hspec] * 4,
        out_shape=[jax.ShapeDtypeStruct(hshape, F32)] * 4,
        compiler_params=_params(("parallel",)),
        name="peer_route",
    )(x1t, wqt, subkeys)

    full_spec = pl.BlockSpec((PEER_HEADS, nk, tt), lambda i, e: (0, 0, i))
    row_spec = pl.BlockSpec((PEER_HEADS, nsub, tt), lambda i, e: (0, e, i))
    return pl.pallas_call(
        _peer_dense_kernel,
        grid=(nt, nk // nsub),
        in_specs=[pl.BlockSpec((dm, tt), lambda i, e: (0, i)),
                  row_spec, row_spec, full_spec, full_spec,
                  pl.BlockSpec((nsub * nk, dm), lambda i, e: (e, 0)),
                  pl.BlockSpec((dm, nsub * nk), lambda i, e: (0, e))],
        out_specs=pl.BlockSpec((dm, tt), lambda i, e: (0, i)),
        out_shape=jax.ShapeDtypeStruct((dm, n), F32),
        scratch_shapes=[pltpu.VMEM((nsub * nk, tt), F32), pltpu.VMEM((nsub * nk, tt), BF16)],
        compiler_params=_params(("parallel", "arbitrary")),
        name="peer_dense",
    )(x1t, c, a, s2, e2, u, vt)


def _final_kernel(x1_ref, pt_ref, p_ref, wg_ref, wp_ref, g_ref, b_ref, o_ref, *, alpha):
    x1 = x1_ref[...]
    ple = jax.nn.sigmoid(_dot(x1.astype(BF16), wg_ref[...])) * _dot(p_ref[...], wp_ref[...])
    y = alpha * x1 + pt_ref[...].T + ple
    o_ref[...] = _layer_norm(y, g_ref[...], b_ref[...])


def _final(x1, peer_t, p_emb, w_gate, w_ple, g, b, tm, alpha):
    n, dm = x1.shape
    pd = p_emb.shape[1]

    def full(shape):
        return pl.BlockSpec(shape, lambda i: (0,) * len(shape))

    return pl.pallas_call(
        functools.partial(_final_kernel, alpha=alpha),
        grid=(n // tm,),
        in_specs=[pl.BlockSpec((tm, dm), lambda i: (i, 0)),
                  pl.BlockSpec((dm, tm), lambda i: (0, i)),
                  pl.BlockSpec((tm, pd), lambda i: (i, 0)),
                  full((dm, dm)), full((pd, dm)), full((1, dm)), full((1, dm))],
        out_specs=pl.BlockSpec((tm, dm), lambda i: (i, 0)),
        out_shape=jax.ShapeDtypeStruct((n, dm), F32),
        compiler_params=_params(("parallel",)),
        name="ple_ln2",
    )(x1, peer_t, p_emb, w_gate, w_ple, g, b)


def _layer(x, p_emb, pos_rows, tm, weights, attention_fn, state_in, chunk, chunk_eff, alpha):
    n = x.shape[0]
    nb = state_in.shape[0]
    t_seq = n // nb
    pr = _project(x.astype(BF16), weights["w_in"], weights["ikg"], weights["ikb"], pos_rows, tm)

    def pad_chunks(a):
        if t_seq % chunk == 0:
            return a
        a = a.reshape(nb, t_seq, a.shape[-1])
        return jnp.pad(a, ((0, 0), (0, chunk - t_seq), (0, 0))).reshape(nb * chunk, a.shape[-1])

    ret_o, s_out = _retention(pad_chunks(pr["rq"]), pad_chunks(pr["rk"]), pad_chunks(pr["rv"]),
                              pad_chunks(pr["rg"]), weights["gn_g"], weights["gn_b"],
                              state_in, chunk, chunk_eff)
    if t_seq % chunk != 0:
        ret_o = ret_o.reshape(nb, chunk, -1)[:, :t_seq].reshape(n, -1)
    att_o = attention_fn(pr)
    x1, x1t = _mix(x, ret_o, att_o, pr["sgab"], weights["w_ret_o"], weights["w_att_o"], weights["w_out"],
                   weights["ln1_g"], weights["ln1_b"], tm, alpha)
    peer_t = _peer(x1t, weights["peer_wqt"], weights["peer_subkeys"], weights["peer_u"], weights["peer_vt"],
                   min(PEER_TOKEN_TILE, n))
    y = _final(x1, peer_t, p_emb.astype(BF16), weights["w_ple_gate"], weights["w_ple"],
               weights["ln2_g"], weights["ln2_b"], tm, alpha)
    return y, pr["ak"], pr["av"], pr["ik"], s_out


def kernel(x_prompt, x_sample, cache_k, cache_v, cache_kidx, state_ret, page_table, p_prompt, p_sample,
           w_in, idx_k_g, idx_k_b, gn_g, gn_b, w_ret_o, w_att_o, w_out, ln1_g, ln1_b, peer_wq,
           peer_subkeys, peer_u, peer_v, w_ple_gate, w_ple, ln2_g, ln2_b):
    depth = w_in.shape[0]
    assert depth == 1, "single-layer trunk"
    nbp, seq, dm = x_prompt.shape
    nbs, t_new, _ = x_sample.shape
    past_len = page_table.shape[1] * cache_k.shape[2]
    alpha = (2.0 * depth) ** 0.25
    i = 0
    weights = dict(
        w_in=w_in[i].astype(BF16),
        ikg=jnp.concatenate([idx_k_g[i], idx_k_g[i]])[None, :],
        ikb=jnp.concatenate([idx_k_b[i], idx_k_b[i]])[None, :],
        gn_g=gn_g[i][None, :], gn_b=gn_b[i][None, :],
        w_ret_o=w_ret_o[i].astype(BF16), w_att_o=w_att_o[i].astype(BF16), w_out=w_out[i].astype(BF16),
        ln1_g=ln1_g[i][None, :], ln1_b=ln1_b[i][None, :],
        peer_wqt=peer_wq[i].T.astype(BF16), peer_subkeys=peer_subkeys[i].astype(BF16),
        peer_u=peer_u[i].astype(BF16), peer_vt=peer_v[i].T.astype(BF16),
        w_ple_gate=w_ple_gate[i].astype(BF16), w_ple=w_ple[i].astype(BF16),
        ln2_g=ln2_g[i][None, :], ln2_b=ln2_b[i][None, :],
    )

    def attn_p(pr):
        return _attention_prompt(pr["aq"], pr["iq"], pr["iw"], pr["akb"], pr["avb"], pr["ikb"], nbp, seq)

    yp, kp, vp, ikp, rp = _layer(
        x_prompt.reshape(nbp * seq, dm), p_prompt[i].reshape(nbp * seq, -1),
        jnp.arange(seq, dtype=jnp.int32), min(TOKEN_TILE, seq), weights, attn_p,
        jnp.zeros((nbp,) + state_ret.shape[2:], F32), RET_CHUNK, RET_CHUNK, alpha)

    def attn_s(pr):
        return _attention_sample(pr["aq"], pr["iq"], pr["iw"], pr["ak"], pr["av"], pr["ik"],
                                 cache_k[i], cache_v[i], cache_kidx[i], page_table, nbs, t_new)

    ns = nbs * t_new
    pos_s = jnp.tile(past_len + jnp.arange(t_new, dtype=jnp.int32), nbs)
    ys, ks, vs, iks, rs = _layer(
        x_sample.reshape(ns, dm), p_sample[i].reshape(ns, -1), pos_s, ns, weights, attn_s,
        state_ret[i].astype(F32), BF16_SUBLANES, t_new, alpha)

    return (yp.reshape(nbp, seq, dm), ys.reshape(nbs, t_new, dm),
            kp.reshape(1, nbp, seq, ATT_KV_HEADS, ATT_HD), vp.reshape(1, nbp, seq, ATT_KV_HEADS, ATT_HD),
            ikp.reshape(1, nbp, seq, IDX_HD), rp[None].astype(state_ret.dtype),
            ks.reshape(1, nbs, t_new, ATT_KV_HEADS, ATT_HD), vs.reshape(1, nbs, t_new, ATT_KV_HEADS, ATT_HD),
            iks.reshape(1, nbs, t_new, IDX_HD), rs[None].astype(state_ret.dtype))
```

```python
import functools

import jax
import jax.numpy as jnp
from jax import lax
from jax.experimental import pallas as pl
from jax.experimental.pallas import tpu as pltpu

F32 = jnp.float32
BF16 = jnp.bfloat16

RET_HEADS = 4
RET_DK = 256
RET_DV = 512
RET_CHUNK = 256
ATT_HEADS = 8
ATT_KV_HEADS = 2
ATT_HD = 128
IDX_HEADS = 8
IDX_HD = 64
TOPK_ATTN = 256
ROPE_THETA = 10000.0
PEER_HEADS = 8
PEER_NKEYS = 128
PEER_DKEY = 256
PEER_TOPK = 16
LN_EPS = 1e-5
NEG = -1e30
INF = float("inf")

LANES = 128
BF16_SUBLANES = 16
VMEM_LIMIT = 48 * 1024 * 1024
BISECT_ITERS = 32
SAMPLE_PAGES_PER_STEP = 32
SAMPLE_SCORE_PAGES_PER_STEP = 64
TOKEN_TILE = 512
PROJ_COL_TILE = 1024
ATTN_QUERY_TILE = 256
PEER_TOKEN_TILE = 512


def _dot(a, b):
    return jnp.dot(a, b, preferred_element_type=F32)


def _dot_nt(a, b):
    return lax.dot_general(a, b, (((1,), (1,)), ((), ())), preferred_element_type=F32)


def _dot_tn(a, b):
    return lax.dot_general(a, b, (((0,), (0,)), ((), ())), preferred_element_type=F32)


def _params(sem):
    return pltpu.CompilerParams(dimension_semantics=sem, vmem_limit_bytes=VMEM_LIMIT)


def _rope_tables(pos, d, width):
    half = d // 2
    inv = ROPE_THETA ** (-jnp.arange(half, dtype=F32) / half)
    ang = pos.astype(F32)[:, None] * inv[None, :]
    cos, sin = jnp.cos(ang), jnp.sin(ang)
    cosf = jnp.concatenate([cos, cos], axis=-1)
    sinf = jnp.concatenate([-sin, sin], axis=-1)
    reps = width // d
    return jnp.tile(cosf, (1, reps)), jnp.tile(sinf, (1, reps))


def _partner(y, d):
    if d == 2 * LANES:
        return jnp.concatenate([y[:, LANES:], y[:, :LANES]], axis=-1)
    if d == LANES:
        return pltpu.roll(y, LANES // 2, axis=1)
    lane = lax.broadcasted_iota(jnp.int32, y.shape, 1)
    first_half = (lane & (d - 1)) < (d // 2)
    return jnp.where(first_half, pltpu.roll(y, LANES - d // 2, axis=1), pltpu.roll(y, d // 2, axis=1))


def _rope(y, cosf, sinf, d):
    return y * cosf + _partner(y, d) * sinf


def _ret_qk_kernel(x_ref, w_ref, cos_ref, sin_ref, q_ref, k_ref):
    x = x_ref[...]
    cosf, sinf = cos_ref[...], sin_ref[...]
    for h in range(2 * RET_HEADS):
        y = _dot(x, w_ref[:, h * RET_DK:(h + 1) * RET_DK])
        r = _rope(y, cosf, sinf, RET_DK)
        if h < RET_HEADS:
            q_ref[:, h * RET_DK:(h + 1) * RET_DK] = r.astype(BF16)
        else:
            hh = h - RET_HEADS
            k_ref[:, hh * RET_DK:(hh + 1) * RET_DK] = (r * (RET_DK ** -0.5)).astype(BF16)


def _plain_kernel(x_ref, w_ref, o_ref, *, sigmoid):
    y = _dot(x_ref[...], w_ref[...])
    if sigmoid:
        y = jax.nn.sigmoid(y)
    o_ref[...] = y.astype(o_ref.dtype)


def _att_proj_kernel(x_ref, wq_ref, wk_ref, wv_ref, cos_ref, sin_ref,
                     q_ref, k_ref, v_ref, kb_ref, vb_ref):
    x = x_ref[...]
    cosf, sinf = cos_ref[...], sin_ref[...]
    for h in range(ATT_HEADS):
        sl = slice(h * ATT_HD, (h + 1) * ATT_HD)
        q_ref[:, sl] = _rope(_dot(x, wq_ref[:, sl]), cosf, sinf, ATT_HD).astype(BF16)
    tm = x.shape[0]
    v = _dot(x, wv_ref[...])
    vb_ref[...] = v.astype(BF16)
    for h in range(ATT_KV_HEADS):
        sl = slice(h * ATT_HD, (h + 1) * ATT_HD)
        k = _rope(_dot(x, wk_ref[:, sl]), cosf, sinf, ATT_HD)
        kb_ref[:, sl] = k.astype(BF16)
        head_rows = pl.ds(h, tm, stride=ATT_KV_HEADS)
        k_ref[head_rows, :] = k
        v_ref[head_rows, :] = v[:, sl]


def _idx_proj_kernel(x_ref, wq_ref, wk_ref, ww_ref, cos_ref, sin_ref, g_ref, b_ref,
                     q_ref, k_ref, kb_ref, w_ref):
    x = x_ref[...]
    cosf, sinf = cos_ref[...], sin_ref[...]
    for c in range(IDX_HEADS * IDX_HD // LANES):
        sl = slice(c * LANES, (c + 1) * LANES)
        q_ref[:, sl] = _rope(_dot(x, wq_ref[:, sl]), cosf, sinf, IDX_HD).astype(BF16)
    y = _dot(x, wk_ref[...])
    mu = jnp.mean(y, axis=-1, keepdims=True)
    var = jnp.mean(jnp.square(y - mu), axis=-1, keepdims=True)
    yn = (y - mu) * lax.rsqrt(var + LN_EPS) * g_ref[...] + b_ref[...]
    k = yn * cosf + pltpu.roll(yn, IDX_HD // 2, axis=1) * sinf
    k_ref[...] = k[:, :IDX_HD]
    kb_ref[...] = k[:, :IDX_HD].astype(BF16)
    w = _dot(x, ww_ref[...])
    w_ref[...] = w[:, :IDX_HEADS] * (IDX_HEADS ** -0.5)


def _project(xb, w, ikg, ikb, pos_rows, tm):
    n, dm = xb.shape
    nt = n // tm
    rows = pos_rows.shape[0]
    tab_blocks = rows // tm
    hq = RET_HEADS * RET_DK
    hv = RET_HEADS * RET_DV
    o = 0
    w_rqk = w[:, o:o + 2 * hq]; o += 2 * hq
    w_rv = w[:, o:o + hv]; o += hv
    w_rg = w[:, o:o + hv]; o += hv
    w_aq = w[:, o:o + ATT_HEADS * ATT_HD]; o += ATT_HEADS * ATT_HD
    w_ak = w[:, o:o + ATT_KV_HEADS * ATT_HD]; o += ATT_KV_HEADS * ATT_HD
    w_av = w[:, o:o + ATT_KV_HEADS * ATT_HD]; o += ATT_KV_HEADS * ATT_HD
    w_iq = w[:, o:o + IDX_HEADS * IDX_HD]; o += IDX_HEADS * IDX_HD
    w_ik = w[:, o:o + IDX_HD]; o += IDX_HD
    w_iw = w[:, o:o + IDX_HEADS]; o += IDX_HEADS
    w_gab = w[:, o:]

    x_spec = pl.BlockSpec((tm, dm), lambda i: (i, 0))

    def tab_spec(width):
        return pl.BlockSpec((tm, width), lambda i: (i % tab_blocks, 0))

    def full(shape):
        return pl.BlockSpec(shape, lambda i: (0,) * len(shape))

    def row_spec(width):
        return pl.BlockSpec((tm, width), lambda i: (i, 0))

    cos256, sin256 = _rope_tables(pos_rows, RET_DK, RET_DK)
    rq, rk = pl.pallas_call(
        _ret_qk_kernel,
        grid=(nt,),
        in_specs=[x_spec, full((dm, 2 * hq)), tab_spec(RET_DK), tab_spec(RET_DK)],
        out_specs=[row_spec(hq), row_spec(hq)],
        out_shape=[jax.ShapeDtypeStruct((n, hq), BF16)] * 2,
        compiler_params=_params(("parallel",)),
        name="proj_ret_qk",
    )(xb, w_rqk, cos256, sin256)

    def plain(wc, dtype, sigmoid, name):
        cols = wc.shape[1]
        tn = min(PROJ_COL_TILE, cols)
        return pl.pallas_call(
            functools.partial(_plain_kernel, sigmoid=sigmoid),
            grid=(cols // tn, nt),
            in_specs=[pl.BlockSpec((tm, dm), lambda j, i: (i, 0)),
                      pl.BlockSpec((dm, tn), lambda j, i: (0, j))],
            out_specs=pl.BlockSpec((tm, tn), lambda j, i: (i, j)),
            out_shape=jax.ShapeDtypeStruct((n, cols), dtype),
            compiler_params=_params(("parallel", "parallel")),
            name=name,
        )(xb, wc)

    rv = plain(w_rv, BF16, False, "proj_ret_v")
    rg = plain(w_rg, F32, False, "proj_ret_gate")
    sgab = plain(w_gab, F32, True, "proj_branch_gates")

    cos128, sin128 = _rope_tables(pos_rows, ATT_HD, ATT_HD)
    kvw = ATT_KV_HEADS * ATT_HD
    aq, ak, av, akb, avb = pl.pallas_call(
        _att_proj_kernel,
        grid=(nt,),
        in_specs=[x_spec, full((dm, ATT_HEADS * ATT_HD)), full((dm, kvw)), full((dm, kvw)),
                  tab_spec(ATT_HD), tab_spec(ATT_HD)],
        out_specs=[row_spec(ATT_HEADS * ATT_HD),
                   pl.BlockSpec((tm * ATT_KV_HEADS, ATT_HD), lambda i: (i, 0)),
                   pl.BlockSpec((tm * ATT_KV_HEADS, ATT_HD), lambda i: (i, 0)),
                   row_spec(kvw), row_spec(kvw)],
        out_shape=[jax.ShapeDtypeStruct((n, ATT_HEADS * ATT_HD), BF16),
                   jax.ShapeDtypeStruct((n * ATT_KV_HEADS, ATT_HD), F32),
                   jax.ShapeDtypeStruct((n * ATT_KV_HEADS, ATT_HD), F32),
                   jax.ShapeDtypeStruct((n, kvw), BF16), jax.ShapeDtypeStruct((n, kvw), BF16)],
        compiler_params=_params(("parallel",)),
        name="proj_att",
    )(xb, w_aq, w_ak, w_av, cos128, sin128)

    cos64, sin64 = _rope_tables(pos_rows, IDX_HD, LANES)
    w_ik2 = jnp.concatenate([w_ik, w_ik], axis=1)
    w_iw_pad = jnp.pad(w_iw, ((0, 0), (0, LANES - IDX_HEADS)))
    iq, ik, ikb, iw = pl.pallas_call(
        _idx_proj_kernel,
        grid=(nt,),
        in_specs=[x_spec, full((dm, IDX_HEADS * IDX_HD)), full((dm, LANES)), full((dm, LANES)),
                  tab_spec(LANES), tab_spec(LANES), full((1, LANES)), full((1, LANES))],
        out_specs=[row_spec(IDX_HEADS * IDX_HD), row_spec(IDX_HD), row_spec(IDX_HD), row_spec(IDX_HEADS)],
        out_shape=[jax.ShapeDtypeStruct((n, IDX_HEADS * IDX_HD), BF16),
                   jax.ShapeDtypeStruct((n, IDX_HD), F32), jax.ShapeDtypeStruct((n, IDX_HD), BF16),
                   jax.ShapeDtypeStruct((n, IDX_HEADS), F32)],
        compiler_params=_params(("parallel",)),
        name="proj_idx",
    )(xb, w_iq, w_ik2, w_iw_pad, cos64, sin64, ikg, ikb)
    return dict(rq=rq, rk=rk, rv=rv, rg=rg, sgab=sgab, aq=aq, ak=ak, av=av, akb=akb, avb=avb,
                iq=iq, ik=ik, ikb=ikb, iw=iw)


def _retention_kernel(sdec_ref, q_ref, k_ref, v_ref, gate_ref, gg_ref, gb_ref,
                      d_ref, cdec_ref, kdec_ref, si_ref, o_ref, so_ref):
    h = pl.program_id(1)
    c = pl.program_id(2)

    @pl.when(c == 0)
    def _():
        so_ref[0, 0] = si_ref[0, 0]

    q = q_ref[...]
    k = k_ref[...]
    v = v_ref[...]
    state = so_ref[0, 0]
    s = _dot_nt(q, k)
    inner = _dot((s * d_ref[0]).astype(BF16), v)
    cross = _dot(q, state.astype(BF16)) * cdec_ref[0]
    o = inner + cross
    kd = (k.astype(F32) * kdec_ref[0]).astype(BF16)
    so_ref[0, 0] = state * sdec_ref[h] + _dot_tn(kd, v)
    mu = jnp.mean(o, axis=-1, keepdims=True)
    var = jnp.mean(jnp.square(o - mu), axis=-1, keepdims=True)
    on = (o - mu) * lax.rsqrt(var + LN_EPS) * gg_ref[...] + gb_ref[...]
    o_ref[...] = (jax.nn.silu(gate_ref[...]) * on).astype(o_ref.dtype)


def _retention(rq, rk, rv, rg, gn_g, gn_b, state_in, chunk, chunk_eff):
    n = rq.shape[0]
    nb = state_in.shape[0]
    nc = n // (nb * chunk)
    lg = jnp.log1p(-jnp.exp2(-5.0 - jnp.arange(RET_HEADS, dtype=F32)))[:, None]
    i = jnp.arange(chunk, dtype=F32)
    diff = i[:, None] - i[None, :]
    decay = jnp.where(diff >= 0, jnp.exp(lg[:, :, None] * jnp.maximum(diff, 0.0)), 0.0)
    cdec = jnp.exp(lg * (i + 1.0))[:, :, None]
    kdec = jnp.where(i < chunk_eff, jnp.exp(lg * jnp.maximum(chunk_eff - 1.0 - i, 0.0)), 0.0)[:, :, None]
    sdec = jnp.exp(lg[:, 0] * chunk_eff)

    o, s_out = pl.pallas_call(
        _retention_kernel,
        grid=(nb, RET_HEADS, nc),
        in_specs=[
            pl.BlockSpec(memory_space=pltpu.SMEM),
            pl.BlockSpec((chunk, RET_DK), lambda b, h, c: (b * nc + c, h)),
            pl.BlockSpec((chunk, RET_DK), lambda b, h, c: (b * nc + c, h)),
            pl.BlockSpec((chunk, RET_DV), lambda b, h, c: (b * nc + c, h)),
            pl.BlockSpec((chunk, RET_DV), lambda b, h, c: (b * nc + c, h)),
            pl.BlockSpec((1, RET_DV), lambda b, h, c: (0, h)),
            pl.BlockSpec((1, RET_DV), lambda b, h, c: (0, h)),
            pl.BlockSpec((1, chunk, chunk), lambda b, h, c: (h, 0, 0)),
            pl.BlockSpec((1, chunk, 1), lambda b, h, c: (h, 0, 0)),
            pl.BlockSpec((1, chunk, 1), lambda b, h, c: (h, 0, 0)),
            pl.BlockSpec((1, 1, RET_DK, RET_DV), lambda b, h, c: (b, h, 0, 0)),
        ],
        out_specs=[
            pl.BlockSpec((chunk, RET_DV), lambda b, h, c: (b * nc + c, h)),
            pl.BlockSpec((1, 1, RET_DK, RET_DV), lambda b, h, c: (b, h, 0, 0)),
        ],
        out_shape=[jax.ShapeDtypeStruct((n, RET_HEADS * RET_DV), BF16),
                   jax.ShapeDtypeStruct(state_in.shape, F32)],
        compiler_params=_params(("parallel", "parallel", "arbitrary")),
        name="retention",
    )(sdec, rq, rk, rv, rg, gn_g, gn_b, decay, cdec, kdec, state_in)
    return o, s_out


M_INIT = -1e29


def _lanes(col):
    return jnp.broadcast_to(col, (col.shape[0], LANES))


def _kth_largest(reduce_tiles, mn, mx, kq):
    kq_b = _lanes(kq)

    def bisect_body(_, carry):
        lo, hi = carry
        mid = 0.5 * (lo + hi)
        cnt = reduce_tiles(lambda a, t: a + jnp.where(t >= mid, 1.0, 0.0), 0.0)
        ge = _lanes(jnp.sum(cnt, axis=-1, keepdims=True)) >= kq_b
        return jnp.where(ge, mid, lo), jnp.where(ge, hi, mid)

    lo, _ = lax.fori_loop(0, BISECT_ITERS, bisect_body, (_lanes(mn), _lanes(mx + 1.0)))
    tau = jnp.min(reduce_tiles(lambda a, t: jnp.minimum(a, jnp.where(t >= lo, t, INF)), INF),
                  axis=-1, keepdims=True)
    tau_b = _lanes(tau)
    cgt = jnp.sum(reduce_tiles(lambda a, t: a + jnp.where(t > tau_b, 1.0, 0.0), 0.0),
                  axis=-1, keepdims=True)
    return tau, kq - cgt


def _softmax_step(s, m_ref, l_ref, acc_ref, idx, v):
    m_old = m_ref[idx]
    m_new = jnp.maximum(m_old, jnp.max(s, axis=-1, keepdims=True))
    p = jnp.exp(s - m_new)
    alpha = jnp.exp(m_old - m_new)
    l_ref[idx] = alpha * l_ref[idx] + jnp.sum(p, axis=-1, keepdims=True)
    acc_ref[idx] = alpha * acc_ref[idx] + _dot(p.astype(BF16), v)
    m_ref[idx] = m_new


def _fold_rows(x):
    return jnp.sum(x.reshape(x.shape[0] // 8, 8, x.shape[1]), axis=0)


def _attn_prompt_kernel(aqt_ref, iqt_ref, iwt_ref, k_ref, vt_ref, kidx_ref, tril_ref, o_ref,
                        sc_ref, m_ref, l_ref, acc_ref, *, tq, topk):
    i = pl.program_id(1)
    nkv = i + 1
    qpos = i * tq + lax.broadcasted_iota(jnp.int32, (1, tq), 1)
    kq = jnp.minimum(topk, qpos + 1).astype(F32)
    group = ATT_HEADS // ATT_KV_HEADS

    def score_body(j, carry):
        mn, mx = carry
        kx = kidx_ref[pl.ds(pl.multiple_of(j * tq, tq), tq), :]
        s = jnp.zeros((tq, tq), F32)
        for h in range(IDX_HEADS):
            d = _dot(kx, iqt_ref[h * IDX_HD:(h + 1) * IDX_HD, :]) * (IDX_HD ** -0.5)
            s = s + iwt_ref[h:h + 1, :] * jnp.maximum(d, 0.0)
        kpos = j * tq + lax.broadcasted_iota(jnp.int32, (tq, tq), 0)
        vis = kpos <= qpos
        sc_ref[j] = jnp.where(vis, s, -INF)
        mn = jnp.minimum(mn, jnp.min(jnp.where(vis, s, INF), axis=0, keepdims=True))
        mx = jnp.maximum(mx, jnp.max(jnp.where(vis, s, -INF), axis=0, keepdims=True))
        return mn, mx

    mn, mx = lax.fori_loop(0, nkv, score_body,
                           (jnp.full((1, tq), INF, F32), jnp.full((1, tq), -INF, F32)))

    def count(pred):
        def body(j, acc):
            return acc + _fold_rows(jnp.where(pred(sc_ref[j]), 1.0, 0.0))
        return jnp.sum(lax.fori_loop(0, nkv, body, jnp.zeros((8, tq), F32)), axis=0, keepdims=True)

    def bisect_body(_, carry):
        lo, hi = carry
        mid = 0.5 * (lo + hi)
        ge = count(lambda t: t >= mid) >= kq
        return jnp.where(ge, mid, lo), jnp.where(ge, hi, mid)

    lo, _ = lax.fori_loop(0, BISECT_ITERS, bisect_body, (mn, mx + 1.0))

    def snap_body(j, acc):
        t = sc_ref[j]
        return jnp.minimum(acc, jnp.min(jnp.where(t >= lo, t, INF), axis=0, keepdims=True))

    tau = lax.fori_loop(0, nkv, snap_body, jnp.full((1, tq), INF, F32))
    need = kq - count(lambda t: t > tau)

    tril = tril_ref[...]

    def mask_body(j, run):
        t = sc_ref[j]
        eqf = jnp.where(t == tau, 1.0, 0.0)
        prefix = _dot(tril, eqf.astype(BF16))
        sel = (t > tau) | ((t == tau) & ((run + prefix) <= need))
        sc_ref[j] = jnp.where(sel, 0.0, NEG)
        return run + jnp.sum(eqf, axis=0, keepdims=True)

    lax.fori_loop(0, nkv, mask_body, jnp.zeros((1, tq), F32))

    m_ref[...] = jnp.full(m_ref.shape, M_INIT, F32)
    l_ref[...] = jnp.zeros(l_ref.shape, F32)
    acc_ref[...] = jnp.zeros(acc_ref.shape, F32)

    def att_body(j, carry):
        rows = pl.ds(pl.multiple_of(j * tq, tq), tq)
        for h in range(ATT_HEADS):
            g = h // group
            kg = k_ref[rows, g * ATT_HD:(g + 1) * ATT_HD]
            vg = vt_ref[0, j, g * ATT_HD:(g + 1) * ATT_HD, :]
            for c in range(tq // LANES):
                ls = slice(c * LANES, (c + 1) * LANES)
                s = _dot(kg, aqt_ref[h * ATT_HD:(h + 1) * ATT_HD, ls]) * (ATT_HD ** -0.5) + sc_ref[j, :, ls]
                m_old = m_ref[h, :, ls]
                m_new = jnp.maximum(m_old, jnp.max(s, axis=0, keepdims=True))
                p = jnp.exp(s - m_new)
                alpha = jnp.exp(m_old - m_new)
                l_ref[h, :, ls] = alpha * l_ref[h, :, ls] + jnp.sum(p, axis=0, keepdims=True)
                acc_ref[h, :, ls] = alpha * acc_ref[h, :, ls] + _dot(vg, p.astype(BF16))
                m_ref[h, :, ls] = m_new
        return carry

    lax.fori_loop(0, nkv, att_body, 0)
    for h in range(ATT_HEADS):
        o_ref[h * ATT_HD:(h + 1) * ATT_HD, :] = (acc_ref[h] / l_ref[h]).astype(o_ref.dtype)


def _attention_prompt(aq, iq, iw, akb, avb, ikb, nb, seq):
    tq = min(ATTN_QUERY_TILE, seq)
    assert seq % tq == 0 and tq % LANES == 0
    nq = seq // tq
    topk = min(TOPK_ATTN, seq // 4)
    n = aq.shape[0]
    tril = jnp.tril(jnp.ones((tq, tq), F32)).astype(BF16)
    kvw = ATT_KV_HEADS * ATT_HD
    vt = avb.reshape(nb, nq, tq, kvw).transpose(0, 1, 3, 2)
    ot = pl.pallas_call(
        functools.partial(_attn_prompt_kernel, tq=tq, topk=topk),
        grid=(nb, nq),
        in_specs=[
            pl.BlockSpec((ATT_HEADS * ATT_HD, tq), lambda b, i: (0, b * nq + i)),
            pl.BlockSpec((IDX_HEADS * IDX_HD, tq), lambda b, i: (0, b * nq + i)),
            pl.BlockSpec((IDX_HEADS, tq), lambda b, i: (0, b * nq + i)),
            pl.BlockSpec((seq, kvw), lambda b, i: (b, 0)),
            pl.BlockSpec((1, nq, kvw, tq), lambda b, i: (b, 0, 0, 0)),
            pl.BlockSpec((seq, IDX_HD), lambda b, i: (b, 0)),
            pl.BlockSpec((tq, tq), lambda b, i: (0, 0)),
        ],
        out_specs=pl.BlockSpec((ATT_HEADS * ATT_HD, tq), lambda b, i: (0, b * nq + i)),
        out_shape=jax.ShapeDtypeStruct((ATT_HEADS * ATT_HD, n), BF16),
        scratch_shapes=[
            pltpu.VMEM((nq, tq, tq), F32),
            pltpu.VMEM((ATT_HEADS, 1, tq), F32),
            pltpu.VMEM((ATT_HEADS, 1, tq), F32),
            pltpu.VMEM((ATT_HEADS, ATT_HD, tq), F32),
        ],
        compiler_params=_params(("parallel", "arbitrary")),
        name="attention_prompt",
    )(aq.T, iq.T, iw.T, akb, vt, ikb, tril)
    return ot.T


def _page_scores(iq_ref, iw_ref, kxt):
    tp = iq_ref.shape[1] // IDX_HEADS
    d = _dot(iq_ref[0], kxt) * (IDX_HD ** -0.5)
    d = iw_ref[0] * jnp.maximum(d, 0.0)
    s = d[:tp]
    for h in range(1, IDX_HEADS):
        s = s + d[h * tp:(h + 1) * tp]
    return s


def _sample_scores_kernel(pt_ref, iq_ref, iw_ref, *refs, npp, n_new, topk):
    page_refs = refs[:npp]
    knew_ref = refs[npp]
    sc_ref, tau_ref, need_ref = refs[npp + 1:]
    s = pl.program_id(1)
    npages = sc_ref.shape[1] - 1
    tp, page = sc_ref.shape[2], sc_ref.shape[3]

    kxt = jnp.concatenate([r[0] for r in page_refs], axis=1).astype(BF16)
    sc = _page_scores(iq_ref, iw_ref, kxt)
    for p in range(npp):
        sc_ref[0, s * npp + p] = sc[:, p * page:(p + 1) * page]

    @pl.when(s == pl.num_programs(1) - 1)
    def _():
        snew = _page_scores(iq_ref, iw_ref, knew_ref[0].astype(BF16))
        t = jnp.minimum(lax.broadcasted_iota(jnp.int32, (tp, page), 0), n_new - 1)
        j = lax.broadcasted_iota(jnp.int32, (tp, page), 1)
        sc_ref[0, npages] = jnp.where(j <= t, snew, -INF)

        def reduce_tiles(fn, init):
            def body(c, acc):
                for p in range(npp):
                    acc = fn(acc, sc_ref[0, c * npp + p])
                return acc
            acc = lax.fori_loop(0, npages // npp, body, jnp.full((tp, page), init, F32))
            return fn(acc, sc_ref[0, npages])

        mn = jnp.min(reduce_tiles(lambda a, x: jnp.minimum(a, jnp.where(x == -INF, INF, x)), INF),
                     axis=-1, keepdims=True)
        mx = jnp.max(reduce_tiles(jnp.maximum, -INF), axis=-1, keepdims=True)
        tau, need = _kth_largest(reduce_tiles, mn, mx, jnp.full((tp, 1), float(topk), F32))
        tau_ref[0] = _lanes(tau)
        need_ref[0] = _lanes(need)


def _sample_attend_kernel(pt_ref, q_ref, sc_ref, tau_ref, need_ref, tri_ref, *refs, npp):
    k_refs = refs[:npp]
    v_refs = refs[npp:2 * npp]
    knew_ref, vnew_ref, o_ref, m_ref, l_ref, acc_ref, run_ref = refs[2 * npp:]
    s = pl.program_id(1)
    npages = sc_ref.shape[1] - 1
    page = sc_ref.shape[3]
    group = ATT_HEADS // ATT_KV_HEADS
    grows = q_ref.shape[1] // ATT_KV_HEADS

    @pl.when(s == 0)
    def _():
        m_ref[...] = jnp.full(m_ref.shape, M_INIT, F32)
        l_ref[...] = jnp.zeros(l_ref.shape, F32)
        acc_ref[...] = jnp.zeros(acc_ref.shape, F32)
        run_ref[...] = jnp.zeros(run_ref.shape, F32)

    tau = tau_ref[0][:, :1]
    need = need_ref[0][:, :1]
    tri = tri_ref[...]

    def attend(pages, krefs, vrefs):
        tp = sc_ref.shape[2]
        scs = [sc_ref[0, pg] for pg in pages]
        eqs = [jnp.where(t == tau, 1.0, 0.0) for t in scs]
        prefix = _dot(jnp.concatenate(eqs, axis=0).astype(BF16), tri)
        run = run_ref[...]
        madds = []
        for k, t in enumerate(scs):
            sel = (t > tau) | ((t == tau) & ((run + prefix[k * tp:(k + 1) * tp]) <= need))
            madds.append(jnp.where(sel, 0.0, NEG))
            run = run + jnp.sum(eqs[k], axis=-1, keepdims=True)
        run_ref[...] = run
        madd = jnp.concatenate(madds, axis=1)
        maddg = jnp.concatenate([madd] * group, axis=0)
        for g in range(ATT_KV_HEADS):
            head_rows = pl.ds(g, page, stride=ATT_KV_HEADS)
            kg = jnp.concatenate([r[0, head_rows, :] for r in krefs], axis=0).astype(BF16)
            vg = jnp.concatenate([r[0, head_rows, :] for r in vrefs], axis=0).astype(BF16)
            sg = _dot_nt(q_ref[0, g * grows:(g + 1) * grows, :], kg) * (ATT_HD ** -0.5) + maddg
            _softmax_step(sg, m_ref, l_ref, acc_ref, g, vg)

    attend([s * npp + p for p in range(npp)], k_refs, v_refs)

    @pl.when(s == pl.num_programs(1) - 1)
    def _():
        attend([npages], [knew_ref], [vnew_ref])
        for g in range(ATT_KV_HEADS):
            o_ref[0, g * grows:(g + 1) * grows, :] = acc_ref[g] / l_ref[g]


def _attention_sample(aq, iq, iw, ak, av, ik, cache_k, cache_v, cache_kidx, page_table, nb, t_new):
    n_phys, page = cache_kidx.shape[0], cache_kidx.shape[1]
    npages = page_table.shape[1]
    npp = SAMPLE_PAGES_PER_STEP if npages % SAMPLE_PAGES_PER_STEP == 0 else 1
    nsteps = npages // npp
    npp_sc = SAMPLE_SCORE_PAGES_PER_STEP if npages % SAMPLE_SCORE_PAGES_PER_STEP == 0 else npp
    tp = BF16_SUBLANES
    topk = min(TOPK_ATTN, (npages * page + t_new) // 4)
    prow = page * ATT_KV_HEADS

    def pad_rows(a, per_seq, rows):
        a = a.reshape(nb, per_seq, a.shape[-1])
        return jnp.pad(a, ((0, 0), (0, rows - per_seq), (0, 0)))

    iq_p = pad_rows(iq, t_new, tp).reshape(nb, tp, IDX_HEADS, IDX_HD).transpose(0, 2, 1, 3)
    iq_p = iq_p.reshape(nb, IDX_HEADS * tp, IDX_HD)
    iw_p = pad_rows(iw, t_new, tp).transpose(0, 2, 1).reshape(nb, IDX_HEADS * tp, 1)
    q_p = pad_rows(aq, t_new, tp).reshape(nb, tp, ATT_HEADS, ATT_HD).transpose(0, 2, 1, 3)
    q_p = q_p.reshape(nb, ATT_HEADS * tp, ATT_HD)
    knew = pad_rows(ak, t_new * ATT_KV_HEADS, prow)
    vnew = pad_rows(av, t_new * ATT_KV_HEADS, prow)
    kinew = pad_rows(ik, t_new, page).swapaxes(1, 2)
    kidx_t = cache_kidx.swapaxes(1, 2)
    ck = cache_k.reshape(n_phys, prow, ATT_HD)
    cv = cache_v.reshape(n_phys, prow, ATT_HD)
    tri = jnp.triu(jnp.ones((page, page), F32)).astype(BF16)

    def page_spec(rows, width, p, per_step=npp):
        return pl.BlockSpec((1, rows, width), lambda b, s, pt, p=p: (pt[b, s * per_step + p], 0, 0))

    def seq_spec(shape):
        return pl.BlockSpec((1,) + shape, lambda b, s, pt: (b,) + (0,) * len(shape))

    sc, tau, need = pl.pallas_call(
        functools.partial(_sample_scores_kernel, npp=npp_sc, n_new=t_new, topk=topk),
        grid_spec=pltpu.PrefetchScalarGridSpec(
            num_scalar_prefetch=1,
            grid=(nb, npages // npp_sc),
            in_specs=[seq_spec((IDX_HEADS * tp, IDX_HD)), seq_spec((IDX_HEADS * tp, 1))]
                     + [page_spec(IDX_HD, page, p, npp_sc) for p in range(npp_sc)] + [seq_spec((IDX_HD, page))],
            out_specs=[seq_spec((npages + 1, tp, page)), seq_spec((tp, page)), seq_spec((tp, page))],
        ),
        out_shape=[jax.ShapeDtypeStruct((nb, npages + 1, tp, page), F32),
                   jax.ShapeDtypeStruct((nb, tp, page), F32),
                   jax.ShapeDtypeStruct((nb, tp, page), F32)],
        compiler_params=_params(("parallel", "arbitrary")),
        name="sample_scores",
    )(page_table, iq_p, iw_p, *([kidx_t] * npp_sc), kinew)

    grows = (ATT_HEADS // ATT_KV_HEADS) * tp
    o = pl.pallas_call(
        functools.partial(_sample_attend_kernel, npp=npp),
        grid_spec=pltpu.PrefetchScalarGridSpec(
            num_scalar_prefetch=1,
            grid=(nb, nsteps),
            in_specs=[seq_spec((ATT_HEADS * tp, ATT_HD)), seq_spec((npages + 1, tp, page)),
                      seq_spec((tp, page)), seq_spec((tp, page)),
                      pl.BlockSpec((page, page), lambda b, s, pt: (0, 0))]
                     + [page_spec(prow, ATT_HD, p) for p in range(npp)]
                     + [page_spec(prow, ATT_HD, p) for p in range(npp)]
                     + [seq_spec((prow, ATT_HD)), seq_spec((prow, ATT_HD))],
            out_specs=seq_spec((ATT_HEADS * tp, ATT_HD)),
            scratch_shapes=[
                pltpu.VMEM((ATT_KV_HEADS, grows, 1), F32),
                pltpu.VMEM((ATT_KV_HEADS, grows, 1), F32),
                pltpu.VMEM((ATT_KV_HEADS, grows, ATT_HD), F32),
                pltpu.VMEM((tp, 1), F32),
            ],
        ),
        out_shape=jax.ShapeDtypeStruct((nb, ATT_HEADS * tp, ATT_HD), F32),
        compiler_params=_params(("parallel", "arbitrary")),
        name="sample_attend",
    )(page_table, q_p, sc, tau, need, tri, *([ck] * npp), *([cv] * npp), knew, vnew)
    o = o.reshape(nb, ATT_HEADS, tp, ATT_HD)[:, :, :t_new].transpose(0, 2, 1, 3)
    return o.reshape(nb * t_new, ATT_HEADS * ATT_HD).astype(BF16)


def _layer_norm(y, g, b):
    mu = jnp.mean(y, axis=-1, keepdims=True)
    var = jnp.mean(jnp.square(y - mu), axis=-1, keepdims=True)
    return (y - mu) * lax.rsqrt(var + LN_EPS) * g + b


def _mix_kernel(x_ref, ret_ref, att_ref, sga_ref, sgb_ref, wr_ref, wa_ref, wo_ref, g_ref, b_ref,
                o_ref, ot_ref, *, alpha):
    branch = sga_ref[...] * _dot(ret_ref[...], wr_ref[...]) + sgb_ref[...] * _dot(att_ref[...], wa_ref[...])
    y = alpha * x_ref[...] + _dot(branch.astype(BF16), wo_ref[...])
    x1 = _layer_norm(y, g_ref[...], b_ref[...])
    o_ref[...] = x1
    ot_ref[...] = x1.T.astype(BF16)


def _mix(x, ret_o, att_o, sgab, w_ret_o, w_att_o, w_out, g, b, tm, alpha):
    n, dm = x.shape
    hv = RET_HEADS * RET_DV
    ha = ATT_HEADS * ATT_HD

    def full(shape):
        return pl.BlockSpec(shape, lambda i: (0,) * len(shape))

    return pl.pallas_call(
        functools.partial(_mix_kernel, alpha=alpha),
        grid=(n // tm,),
        in_specs=[pl.BlockSpec((tm, dm), lambda i: (i, 0)),
                  pl.BlockSpec((tm, hv), lambda i: (i, 0)),
                  pl.BlockSpec((tm, ha), lambda i: (i, 0)),
                  pl.BlockSpec((tm, dm), lambda i: (i, 0)),
                  pl.BlockSpec((tm, dm), lambda i: (i, 1)),
                  full((hv, dm)), full((ha, dm)), full((dm, dm)), full((1, dm)), full((1, dm))],
        out_specs=[pl.BlockSpec((tm, dm), lambda i: (i, 0)),
                   pl.BlockSpec((dm, tm), lambda i: (0, i))],
        out_shape=[jax.ShapeDtypeStruct((n, dm), F32), jax.ShapeDtypeStruct((dm, n), BF16)],
        compiler_params=_params(("parallel",)),
        name="branch_mix_ln1",
    )(x, ret_o, att_o, sgab, sgab, w_ret_o, w_att_o, w_out, g, b)


PEER_EXPERT_ROWS = 16
PEER_ROW_TILE = 32


def _top_values(cur, k):
    rows = []
    for _ in range(k):
        m = jnp.max(cur, axis=0, keepdims=True)
        rows.append(m)
        cur = jnp.where(cur == m, -INF, cur)
    return rows


def _peer_route_kernel(xt_ref, wqt_ref, sk_ref, c_ref, a_ref, s2_ref, e2_ref):
    xt = xt_ref[...]
    half = PEER_DKEY // 2
    k1 = PEER_TOPK + 1
    pairs = [(a, b) for a in range(k1) for b in range(k1 // (a + 1))]
    for h in range(PEER_HEADS):
        qt = _dot(wqt_ref[h * PEER_DKEY:(h + 1) * PEER_DKEY, :], xt).astype(BF16)
        s1_ref = c_ref.at[h]
        s1_ref[...] = _dot(sk_ref[h, 0], qt[:half])
        s2_ref[h] = _dot(sk_ref[h, 1], qt[half:])
        for lc in range(xt.shape[1] // LANES):
            ls = slice(lc * LANES, (lc + 1) * LANES)
            s1 = s1_ref[:, ls]
            s2 = s2_ref[h, :, ls]
            v1 = _top_values(s1, k1)
            v2 = _top_values(s2, k1)
            rows = [v1[a] + v2[b] for a, b in pairs]
            rows += [jnp.full_like(rows[0], -INF)] * ((-len(rows)) % 8)
            cand = jnp.concatenate(rows, axis=0)
            best = _top_values(cand, k1)
            theta = 0.5 * (best[PEER_TOPK - 1] + best[PEER_TOPK])
            z = jnp.sum(jnp.where(cand > theta, jnp.exp(cand - best[0]), 0.0), axis=0, keepdims=True)
            c_ref[h, :, ls] = theta - s1
            a_ref[h, :, ls] = jnp.exp(s1 - v1[0]) / z
            e2_ref[h, :, ls] = jnp.exp(s2 - v2[0])


def _peer_dense_kernel(xt_ref, c_ref, a_ref, s2_ref, e2_ref, u_ref, vt_ref, o_ref, act_ref, gm_ref):
    e = pl.program_id(1)
    nk = PEER_NKEYS
    nsub = PEER_EXPERT_ROWS
    tt = xt_ref.shape[1]

    @pl.when(e == 0)
    def _():
        o_ref[...] = jnp.zeros(o_ref.shape, F32)

    def activations(rows, ls):
        act = _dot(u_ref[rows, :], xt_ref[:, ls])
        act_ref[rows, ls] = 0.5 * act * (1.0 + lax.erf(act * (2.0 ** -0.5)))

    def weights_tile(lc, rt):
        ls = slice(lc * LANES, (lc + 1) * LANES)
        rs = slice(rt * PEER_ROW_TILE, (rt + 1) * PEER_ROW_TILE)
        g = [jnp.zeros((PEER_ROW_TILE, LANES), F32)] * nsub
        for h in range(PEER_HEADS):
            s2 = s2_ref[h, rs, ls]
            e2 = e2_ref[h, rs, ls]
            for c in range(nsub):
                g[c] = g[c] + jnp.where(s2 > c_ref[h, c:c + 1, ls], e2 * a_ref[h, c:c + 1, ls], 0.0)
        for c in range(nsub):
            er = slice(c * nk + rt * PEER_ROW_TILE, c * nk + (rt + 1) * PEER_ROW_TILE)
            gm_ref[er, ls] = (g[c] * act_ref[er, ls]).astype(BF16)

    def outputs(rows, ls):
        o_ref[rows, ls] += _dot(vt_ref[rows, :], gm_ref[:, ls])

    nrt = nk // PEER_ROW_TILE
    nlc = tt // LANES
    if nlc % 2:
        activations(slice(None), slice(None))
        for lc in range(nlc):
            for rt in range(nrt):
                weights_tile(lc, rt)
        outputs(slice(None), slice(None))
        return

    halves = [slice(0, tt // 2), slice(tt // 2, tt)]
    npiece = 4
    arows = [slice(q * (nsub * nk // npiece), (q + 1) * (nsub * nk // npiece)) for q in range(npiece)]
    orows = [slice(q * (o_ref.shape[0] // npiece), (q + 1) * (o_ref.shape[0] // npiece)) for q in range(npiece)]
    tiles = [[(lc, rt) for lc in range(hf * nlc // 2, (hf + 1) * nlc // 2) for rt in range(nrt)] for hf in range(2)]
    per = len(tiles[0]) // npiece
    for q in range(npiece):
        activations(arows[q], halves[0])
    for q in range(npiece):
        activations(arows[q], halves[1])
        for lc, rt in tiles[0][q * per:(q + 1) * per]:
            weights_tile(lc, rt)
    for q in range(npiece):
        outputs(orows[q], halves[0])
        for lc, rt in tiles[1][q * per:(q + 1) * per]:
            weights_tile(lc, rt)
    for q in range(npiece):
        outputs(orows[q], halves[1])


def _peer(x1t, wqt, subkeys, u, vt, tt):
    dm, n = x1t.shape
    assert n % tt == 0 and tt % LANES == 0
    nt = n // tt
    nk = PEER_NKEYS
    nsub = PEER_EXPERT_ROWS
    hshape = (PEER_HEADS, nk, n)
    hspec = pl.BlockSpec((PEER_HEADS, nk, tt), lambda i: (0, 0, i))
    c, a, s2, e2 = pl.pallas_call(
        _peer_route_kernel,
        grid=(nt,),
        in_specs=[pl.BlockSpec((dm, tt), lambda i: (0, i)),
                  pl.BlockSpec(wqt.shape, lambda i: (0, 0)),
                  pl.BlockSpec(subkeys.shape, lambda i: (0, 0, 0, 0))],
        out_specs=[hspec] * 4,
        out_shape=[jax.ShapeDtypeStruct(hshape, F32)] * 4,
        compiler_params=_params(("parallel",)),
        name="peer_route",
    )(x1t, wqt, subkeys)

    full_spec = pl.BlockSpec((PEER_HEADS, nk, tt), lambda i, e: (0, 0, i))
    row_spec = pl.BlockSpec((PEER_HEADS, nsub, tt), lambda i, e: (0, e, i))
    return pl.pallas_call(
        _peer_dense_kernel,
        grid=(nt, nk // nsub),
        in_specs=[pl.BlockSpec((dm, tt), lambda i, e: (0, i)),
                  row_spec, row_spec, full_spec, full_spec,
                  pl.BlockSpec((nsub * nk, dm), lambda i, e: (e, 0)),
                  pl.BlockSpec((dm, nsub * nk), lambda i, e: (0, e))],
        out_specs=pl.BlockSpec((dm, tt), lambda i, e: (0, i)),
        out_shape=jax.ShapeDtypeStruct((dm, n), F32),
        scratch_shapes=[pltpu.VMEM((nsub * nk, tt), F32), pltpu.VMEM((nsub * nk, tt), BF16)],
        compiler_params=_params(("parallel", "arbitrary")),
        name="peer_dense",
    )(x1t, c, a, s2, e2, u, vt)


def _final_kernel(x1_ref, pt_ref, p_ref, wg_ref, wp_ref, g_ref, b_ref, o_ref, *, alpha):
    x1 = x1_ref[...]
    ple = jax.nn.sigmoid(_dot(x1.astype(BF16), wg_ref[...])) * _dot(p_ref[...], wp_ref[...])
    y = alpha * x1 + pt_ref[...].T + ple
    o_ref[...] = _layer_norm(y, g_ref[...], b_ref[...])


def _final(x1, peer_t, p_emb, w_gate, w_ple, g, b, tm, alpha):
    n, dm = x1.shape
    pd = p_emb.shape[1]

    def full(shape):
        return pl.BlockSpec(shape, lambda i: (0,) * len(shape))

    return pl.pallas_call(
        functools.partial(_final_kernel, alpha=alpha),
        grid=(n // tm,),
        in_specs=[pl.BlockSpec((tm, dm), lambda i: (i, 0)),
                  pl.BlockSpec((dm, tm), lambda i: (0, i)),
                  pl.BlockSpec((tm, pd), lambda i: (i, 0)),
                  full((dm, dm)), full((pd, dm)), full((1, dm)), full((1, dm))],
        out_specs=pl.BlockSpec((tm, dm), lambda i: (i, 0)),
        out_shape=jax.ShapeDtypeStruct((n, dm), F32),
        compiler_params=_params(("parallel",)),
        name="ple_ln2",
    )(x1, peer_t, p_emb, w_gate, w_ple, g, b)


def _layer(x, p_emb, pos_rows, tm, weights, attention_fn, state_in, chunk, chunk_eff, alpha):
    n = x.shape[0]
    nb = state_in.shape[0]
    t_seq = n // nb
    pr = _project(x.astype(BF16), weights["w_in"], weights["ikg"], weights["ikb"], pos_rows, tm)

    def pad_chunks(a):
        if t_seq % chunk == 0:
            return a
        a = a.reshape(nb, t_seq, a.shape[-1])
        return jnp.pad(a, ((0, 0), (0, chunk - t_seq), (0, 0))).reshape(nb * chunk, a.shape[-1])

    ret_o, s_out = _retention(pad_chunks(pr["rq"]), pad_chunks(pr["rk"]), pad_chunks(pr["rv"]),
                              pad_chunks(pr["rg"]), weights["gn_g"], weights["gn_b"],
                              state_in, chunk, chunk_eff)
    if t_seq % chunk != 0:
        ret_o = ret_o.reshape(nb, chunk, -1)[:, :t_seq].reshape(n, -1)
    att_o = attention_fn(pr)
    x1, x1t = _mix(x, ret_o, att_o, pr["sgab"], weights["w_ret_o"], weights["w_att_o"], weights["w_out"],
                   weights["ln1_g"], weights["ln1_b"], tm, alpha)
    peer_t = _peer(x1t, weights["peer_wqt"], weights["peer_subkeys"], weights["peer_u"], weights["peer_vt"],
                   min(PEER_TOKEN_TILE, n))
    y = _final(x1, peer_t, p_emb.astype(BF16), weights["w_ple_gate"], weights["w_ple"],
               weights["ln2_g"], weights["ln2_b"], tm, alpha)
    return y, pr["ak"], pr["av"], pr["ik"], s_out


def kernel(x_prompt, x_sample, cache_k, cache_v, cache_kidx, state_ret, page_table, p_prompt, p_sample,
           w_in, idx_k_g, idx_k_b, gn_g, gn_b, w_ret_o, w_att_o, w_out, ln1_g, ln1_b, peer_wq,
           peer_subkeys, peer_u, peer_v, w_ple_gate, w_ple, ln2_g, ln2_b):
    depth = w_in.shape[0]
    assert depth == 1, "single-layer trunk"
    nbp, seq, dm = x_prompt.shape
    nbs, t_new, _ = x_sample.shape
    past_len = page_table.shape[1] * cache_k.shape[2]
    alpha = (2.0 * depth) ** 0.25
    i = 0
    weights = dict(
        w_in=w_in[i].astype(BF16),
        ikg=jnp.concatenate([idx_k_g[i], idx_k_g[i]])[None, :],
        ikb=jnp.concatenate([idx_k_b[i], idx_k_b[i]])[None, :],
        gn_g=gn_g[i][None, :], gn_b=gn_b[i][None, :],
        w_ret_o=w_ret_o[i].astype(BF16), w_att_o=w_att_o[i].astype(BF16), w_out=w_out[i].astype(BF16),
        ln1_g=ln1_g[i][None, :], ln1_b=ln1_b[i][None, :],
        peer_wqt=peer_wq[i].T.astype(BF16), peer_subkeys=peer_subkeys[i].astype(BF16),
        peer_u=peer_u[i].astype(BF16), peer_vt=peer_v[i].T.astype(BF16),
        w_ple_gate=w_ple_gate[i].astype(BF16), w_ple=w_ple[i].astype(BF16),
        ln2_g=ln2_g[i][None, :], ln2_b=ln2_b[i][None, :],
    )

    def attn_p(pr):
        return _attention_prompt(pr["aq"], pr["iq"], pr["iw"], pr["akb"], pr["avb"], pr["ikb"], nbp, seq)

    yp, kp, vp, ikp, rp = _layer(
        x_prompt.reshape(nbp * seq, dm), p_prompt[i].reshape(nbp * seq, -1),
        jnp.arange(seq, dtype=jnp.int32), min(TOKEN_TILE, seq), weights, attn_p,
        jnp.zeros((nbp,) + state_ret.shape[2:], F32), RET_CHUNK, RET_CHUNK, alpha)

    def attn_s(pr):
        return _attention_sample(pr["aq"], pr["iq"], pr["iw"], pr["ak"], pr["av"], pr["ik"],
                                 cache_k[i], cache_v[i], cache_kidx[i], page_table, nbs, t_new)

    ns = nbs * t_new
    pos_s = jnp.tile(past_len + jnp.arange(t_new, dtype=jnp.int32), nbs)
    ys, ks, vs, iks, rs = _layer(
        x_sample.reshape(ns, dm), p_sample[i].reshape(ns, -1), pos_s, ns, weights, attn_s,
        state_ret[i].astype(F32), BF16_SUBLANES, t_new, alpha)

    return (yp.reshape(nbp, seq, dm), ys.reshape(nbs, t_new, dm),
            kp.reshape(1, nbp, seq, ATT_KV_HEADS, ATT_HD), vp.reshape(1, nbp, seq, ATT_KV_HEADS, ATT_HD),
            ikp.reshape(1, nbp, seq, IDX_HD), rp[None].astype(state_ret.dtype),
            ks.reshape(1, nbs, t_new, ATT_KV_HEADS, ATT_HD), vs.reshape(1, nbs, t_new, ATT_KV_HEADS, ATT_HD),
            iks.reshape(1, nbs, t_new, IDX_HD), rs[None].astype(state_ret.dtype))
```

```python
import functools

import jax
import jax.numpy as jnp
from jax import lax
from jax.experimental import pallas as pl
from jax.experimental.pallas import tpu as pltpu

F32 = jnp.float32
BF16 = jnp.bfloat16

RET_HEADS = 4
RET_DK = 256
RET_DV = 512
RET_CHUNK = 256
ATT_HEADS = 8
ATT_KV_HEADS = 2
ATT_HD = 128
IDX_HEADS = 8
IDX_HD = 64
TOPK_ATTN = 256
ROPE_THETA = 10000.0
PEER_HEADS = 8
PEER_NKEYS = 128
PEER_DKEY = 256
PEER_TOPK = 16
LN_EPS = 1e-5
NEG = -1e30
INF = float("inf")

LANES = 128
BF16_SUBLANES = 16
VMEM_LIMIT = 48 * 1024 * 1024
BISECT_ITERS = 32
SAMPLE_PAGES_PER_STEP = 32
SAMPLE_SCORE_PAGES_PER_STEP = 64
TOKEN_TILE = 512
PROJ_COL_TILE = 1024
ATTN_QUERY_TILE = 256
PEER_TOKEN_TILE = 512


def _dot(a, b):
    return jnp.dot(a, b, preferred_element_type=F32)


def _dot_nt(a, b):
    return lax.dot_general(a, b, (((1,), (1,)), ((), ())), preferred_element_type=F32)


def _dot_tn(a, b):
    return lax.dot_general(a, b, (((0,), (0,)), ((), ())), preferred_element_type=F32)


def _params(sem):
    return pltpu.CompilerParams(dimension_semantics=sem, vmem_limit_bytes=VMEM_LIMIT)


def _rope_tables(pos, d, width):
    half = d // 2
    inv = ROPE_THETA ** (-jnp.arange(half, dtype=F32) / half)
    ang = pos.astype(F32)[:, None] * inv[None, :]
    cos, sin = jnp.cos(ang), jnp.sin(ang)
    cosf = jnp.concatenate([cos, cos], axis=-1)
    sinf = jnp.concatenate([-sin, sin], axis=-1)
    reps = width // d
    return jnp.tile(cosf, (1, reps)), jnp.tile(sinf, (1, reps))


def _partner(y, d):
    if d == 2 * LANES:
        return jnp.concatenate([y[:, LANES:], y[:, :LANES]], axis=-1)
    if d == LANES:
        return pltpu.roll(y, LANES // 2, axis=1)
    lane = lax.broadcasted_iota(jnp.int32, y.shape, 1)
    first_half = (lane & (d - 1)) < (d // 2)
    return jnp.where(first_half, pltpu.roll(y, LANES - d // 2, axis=1), pltpu.roll(y, d // 2, axis=1))


def _rope(y, cosf, sinf, d):
    return y * cosf + _partner(y, d) * sinf


def _ret_qk_kernel(x_ref, w_ref, cos_ref, sin_ref, q_ref, k_ref):
    x = x_ref[...]
    cosf, sinf = cos_ref[...], sin_ref[...]
    for h in range(2 * RET_HEADS):
        y = _dot(x, w_ref[:, h * RET_DK:(h + 1) * RET_DK])
        r = _rope(y, cosf, sinf, RET_DK)
        if h < RET_HEADS:
            q_ref[:, h * RET_DK:(h + 1) * RET_DK] = r.astype(BF16)
        else:
            hh = h - RET_HEADS
            k_ref[:, hh * RET_DK:(hh + 1) * RET_DK] = (r * (RET_DK ** -0.5)).astype(BF16)


def _plain_kernel(x_ref, w_ref, o_ref, *, sigmoid):
    y = _dot(x_ref[...], w_ref[...])
    if sigmoid:
        y = jax.nn.sigmoid(y)
    o_ref[...] = y.astype(o_ref.dtype)


def _att_proj_kernel(x_ref, wq_ref, wk_ref, wv_ref, cos_ref, sin_ref,
                     q_ref, k_ref, v_ref, kb_ref, vb_ref):
    x = x_ref[...]
    cosf, sinf = cos_ref[...], sin_ref[...]
    for h in range(ATT_HEADS):
        sl = slice(h * ATT_HD, (h + 1) * ATT_HD)
        q_ref[:, sl] = _rope(_dot(x, wq_ref[:, sl]), cosf, sinf, ATT_HD).astype(BF16)
    tm = x.shape[0]
    v = _dot(x, wv_ref[...])
    vb_ref[...] = v.astype(BF16)
    for h in range(ATT_KV_HEADS):
        sl = slice(h * ATT_HD, (h + 1) * ATT_HD)
        k = _rope(_dot(x, wk_ref[:, sl]), cosf, sinf, ATT_HD)
        kb_ref[:, sl] = k.astype(BF16)
        head_rows = pl.ds(h, tm, stride=ATT_KV_HEADS)
        k_ref[head_rows, :] = k
        v_ref[head_rows, :] = v[:, sl]


def _idx_proj_kernel(x_ref, wq_ref, wk_ref, ww_ref, cos_ref, sin_ref, g_ref, b_ref,
                     q_ref, k_ref, kb_ref, w_ref):
    x = x_ref[...]
    cosf, sinf = cos_ref[...], sin_ref[...]
    for c in range(IDX_HEADS * IDX_HD // LANES):
        sl = slice(c * LANES, (c + 1) * LANES)
        q_ref[:, sl] = _rope(_dot(x, wq_ref[:, sl]), cosf, sinf, IDX_HD).astype(BF16)
    y = _dot(x, wk_ref[...])
    mu = jnp.mean(y, axis=-1, keepdims=True)
    var = jnp.mean(jnp.square(y - mu), axis=-1, keepdims=True)
    yn = (y - mu) * lax.rsqrt(var + LN_EPS) * g_ref[...] + b_ref[...]
    k = yn * cosf + pltpu.roll(yn, IDX_HD // 2, axis=1) * sinf
    k_ref[...] = k[:, :IDX_HD]
    kb_ref[...] = k[:, :IDX_HD].astype(BF16)
    w = _dot(x, ww_ref[...])
    w_ref[...] = w[:, :IDX_HEADS] * (IDX_HEADS ** -0.5)


def _project(xb, w, ikg, ikb, pos_rows, tm):
    n, dm = xb.shape
    nt = n // tm
    rows = pos_rows.shape[0]
    tab_blocks = rows // tm
    hq = RET_HEADS * RET_DK
    hv = RET_HEADS * RET_DV
    o = 0
    w_rqk = w[:, o:o + 2 * hq]; o += 2 * hq
    w_rv = w[:, o:o + hv]; o += hv
    w_rg = w[:, o:o + hv]; o += hv
    w_aq = w[:, o:o + ATT_HEADS * ATT_HD]; o += ATT_HEADS * ATT_HD
    w_ak = w[:, o:o + ATT_KV_HEADS * ATT_HD]; o += ATT_KV_HEADS * ATT_HD
    w_av = w[:, o:o + ATT_KV_HEADS * ATT_HD]; o += ATT_KV_HEADS * ATT_HD
    w_iq = w[:, o:o + IDX_HEADS * IDX_HD]; o += IDX_HEADS * IDX_HD
    w_ik = w[:, o:o + IDX_HD]; o += IDX_HD
    w_iw = w[:, o:o + IDX_HEADS]; o += IDX_HEADS
    w_gab = w[:, o:]

    x_spec = pl.BlockSpec((tm, dm), lambda i: (i, 0))

    def tab_spec(width):
        return pl.BlockSpec((tm, width), lambda i: (i % tab_blocks, 0))

    def full(shape):
        return pl.BlockSpec(shape, lambda i: (0,) * len(shape))

    def row_spec(width):
        return pl.BlockSpec((tm, width), lambda i: (i, 0))

    cos256, sin256 = _rope_tables(pos_rows, RET_DK, RET_DK)
    rq, rk = pl.pallas_call(
        _ret_qk_kernel,
        grid=(nt,),
        in_specs=[x_spec, full((dm, 2 * hq)), tab_spec(RET_DK), tab_spec(RET_DK)],
        out_specs=[row_spec(hq), row_spec(hq)],
        out_shape=[jax.ShapeDtypeStruct((n, hq), BF16)] * 2,
        compiler_params=_params(("parallel",)),
        name="proj_ret_qk",
    )(xb, w_rqk, cos256, sin256)

    def plain(wc, dtype, sigmoid, name):
        cols = wc.shape[1]
        tn = min(PROJ_COL_TILE, cols)
        return pl.pallas_call(
            functools.partial(_plain_kernel, sigmoid=sigmoid),
            grid=(cols // tn, nt),
            in_specs=[pl.BlockSpec((tm, dm), lambda j, i: (i, 0)),
                      pl.BlockSpec((dm, tn), lambda j, i: (0, j))],
            out_specs=pl.BlockSpec((tm, tn), lambda j, i: (i, j)),
            out_shape=jax.ShapeDtypeStruct((n, cols), dtype),
            compiler_params=_params(("parallel", "parallel")),
            name=name,
        )(xb, wc)

    rv = plain(w_rv, BF16, False, "proj_ret_v")
    rg = plain(w_rg, F32, False, "proj_ret_gate")
    sgab = plain(w_gab, F32, True, "proj_branch_gates")

    cos128, sin128 = _rope_tables(pos_rows, ATT_HD, ATT_HD)
    kvw = ATT_KV_HEADS * ATT_HD
    aq, ak, av, akb, avb = pl.pallas_call(
        _att_proj_kernel,
        grid=(nt,),
        in_specs=[x_spec, full((dm, ATT_HEADS * ATT_HD)), full((dm, kvw)), full((dm, kvw)),
                  tab_spec(ATT_HD), tab_spec(ATT_HD)],
        out_specs=[row_spec(ATT_HEADS * ATT_HD),
                   pl.BlockSpec((tm * ATT_KV_HEADS, ATT_HD), lambda i: (i, 0)),
                   pl.BlockSpec((tm * ATT_KV_HEADS, ATT_HD), lambda i: (i, 0)),
                   row_spec(kvw), row_spec(kvw)],
        out_shape=[jax.ShapeDtypeStruct((n, ATT_HEADS * ATT_HD), BF16),
                   jax.ShapeDtypeStruct((n * ATT_KV_HEADS, ATT_HD), F32),
                   jax.ShapeDtypeStruct((n * ATT_KV_HEADS, ATT_HD), F32),
                   jax.ShapeDtypeStruct((n, kvw), BF16), jax.ShapeDtypeStruct((n, kvw), BF16)],
        compiler_params=_params(("parallel",)),
        name="proj_att",
    )(xb, w_aq, w_ak, w_av, cos128, sin128)

    cos64, sin64 = _rope_tables(pos_rows, IDX_HD, LANES)
    w_ik2 = jnp.concatenate([w_ik, w_ik], axis=1)
    w_iw_pad = jnp.pad(w_iw, ((0, 0), (0, LANES - IDX_HEADS)))
    iq, ik, ikb, iw = pl.pallas_call(
        _idx_proj_kernel,
        grid=(nt,),
        in_specs=[x_spec, full((dm, IDX_HEADS * IDX_HD)), full((dm, LANES)), full((dm, LANES)),
                  tab_spec(LANES), tab_spec(LANES), full((1, LANES)), full((1, LANES))],
        out_specs=[row_spec(IDX_HEADS * IDX_HD), row_spec(IDX_HD), row_spec(IDX_HD), row_spec(IDX_HEADS)],
        out_shape=[jax.ShapeDtypeStruct((n, IDX_HEADS * IDX_HD), BF16),
                   jax.ShapeDtypeStruct((n, IDX_HD), F32), jax.ShapeDtypeStruct((n, IDX_HD), BF16),
                   jax.ShapeDtypeStruct((n, IDX_HEADS), F32)],
        compiler_params=_params(("parallel",)),
        name="proj_idx",
    )(xb, w_iq, w_ik2, w_iw_pad, cos64, sin64, ikg, ikb)
    return dict(rq=rq, rk=rk, rv=rv, rg=rg, sgab=sgab, aq=aq, ak=ak, av=av, akb=akb, avb=avb,
                iq=iq, ik=ik, ikb=ikb, iw=iw)


def _retention_kernel(sdec_ref, q_ref, k_ref, v_ref, gate_ref, gg_ref, gb_ref,
                      d_ref, cdec_ref, kdec_ref, si_ref, o_ref, so_ref):
    h = pl.program_id(1)
    c = pl.program_id(2)

    @pl.when(c == 0)
    def _():
        so_ref[0, 0] = si_ref[0, 0]

    q = q_ref[...]
    k = k_ref[...]
    v = v_ref[...]
    state = so_ref[0, 0]
    s = _dot_nt(q, k)
    inner = _dot((s * d_ref[0]).astype(BF16), v)
    cross = _dot(q, state.astype(BF16)) * cdec_ref[0]
    o = inner + cross
    kd = (k.astype(F32) * kdec_ref[0]).astype(BF16)
    so_ref[0, 0] = state * sdec_ref[h] + _dot_tn(kd, v)
    mu = jnp.mean(o, axis=-1, keepdims=True)
    var = jnp.mean(jnp.square(o - mu), axis=-1, keepdims=True)
    on = (o - mu) * lax.rsqrt(var + LN_EPS) * gg_ref[...] + gb_ref[...]
    o_ref[...] = (jax.nn.silu(gate_ref[...]) * on).astype(o_ref.dtype)


def _retention(rq, rk, rv, rg, gn_g, gn_b, state_in, chunk, chunk_eff):
    n = rq.shape[0]
    nb = state_in.shape[0]
    nc = n // (nb * chunk)
    lg = jnp.log1p(-jnp.exp2(-5.0 - jnp.arange(RET_HEADS, dtype=F32)))[:, None]
    i = jnp.arange(chunk, dtype=F32)
    diff = i[:, None] - i[None, :]
    decay = jnp.where(diff >= 0, jnp.exp(lg[:, :, None] * jnp.maximum(diff, 0.0)), 0.0)
    cdec = jnp.exp(lg * (i + 1.0))[:, :, None]
    kdec = jnp.where(i < chunk_eff, jnp.exp(lg * jnp.maximum(chunk_eff - 1.0 - i, 0.0)), 0.0)[:, :, None]
    sdec = jnp.exp(lg[:, 0] * chunk_eff)

    o, s_out = pl.pallas_call(
        _retention_kernel,
        grid=(nb, RET_HEADS, nc),
        in_specs=[
            pl.BlockSpec(memory_space=pltpu.SMEM),
            pl.BlockSpec((chunk, RET_DK), lambda b, h, c: (b * nc + c, h)),
            pl.BlockSpec((chunk, RET_DK), lambda b, h, c: (b * nc + c, h)),
            pl.BlockSpec((chunk, RET_DV), lambda b, h, c: (b * nc + c, h)),
            pl.BlockSpec((chunk, RET_DV), lambda b, h, c: (b * nc + c, h)),
            pl.BlockSpec((1, RET_DV), lambda b, h, c: (0, h)),
            pl.BlockSpec((1, RET_DV), lambda b, h, c: (0, h)),
            pl.BlockSpec((1, chunk, chunk), lambda b, h, c: (h, 0, 0)),
            pl.BlockSpec((1, chunk, 1), lambda b, h, c: (h, 0, 0)),
            pl.BlockSpec((1, chunk, 1), lambda b, h, c: (h, 0, 0)),
            pl.BlockSpec((1, 1, RET_DK, RET_DV), lambda b, h, c: (b, h, 0, 0)),
        ],
        out_specs=[
            pl.BlockSpec((chunk, RET_DV), lambda b, h, c: (b * nc + c, h)),
            pl.BlockSpec((1, 1, RET_DK, RET_DV), lambda b, h, c: (b, h, 0, 0)),
        ],
        out_shape=[jax.ShapeDtypeStruct((n, RET_HEADS * RET_DV), BF16),
                   jax.ShapeDtypeStruct(state_in.shape, F32)],
        compiler_params=_params(("parallel", "parallel", "arbitrary")),
        name="retention",
    )(sdec, rq, rk, rv, rg, gn_g, gn_b, decay, cdec, kdec, state_in)
    return o, s_out


M_INIT = -1e29


def _lanes(col):
    return jnp.broadcast_to(col, (col.shape[0], LANES))


def _kth_largest(reduce_tiles, mn, mx, kq):
    kq_b = _lanes(kq)

    def bisect_body(_, carry):
        lo, hi = carry
        mid = 0.5 * (lo + hi)
        cnt = reduce_tiles(lambda a, t: a + jnp.where(t >= mid, 1.0, 0.0), 0.0)
        ge = _lanes(jnp.sum(cnt, axis=-1, keepdims=True)) >= kq_b
        return jnp.where(ge, mid, lo), jnp.where(ge, hi, mid)

    lo, _ = lax.fori_loop(0, BISECT_ITERS, bisect_body, (_lanes(mn), _lanes(mx + 1.0)))
    tau = jnp.min(reduce_tiles(lambda a, t: jnp.minimum(a, jnp.where(t >= lo, t, INF)), INF),
                  axis=-1, keepdims=True)
    tau_b = _lanes(tau)
    cgt = jnp.sum(reduce_tiles(lambda a, t: a + jnp.where(t > tau_b, 1.0, 0.0), 0.0),
                  axis=-1, keepdims=True)
    return tau, kq - cgt


def _softmax_step(s, m_ref, l_ref, acc_ref, idx, v):
    m_old = m_ref[idx]
    m_new = jnp.maximum(m_old, jnp.max(s, axis=-1, keepdims=True))
    p = jnp.exp(s - m_new)
    alpha = jnp.exp(m_old - m_new)
    l_ref[idx] = alpha * l_ref[idx] + jnp.sum(p, axis=-1, keepdims=True)
    acc_ref[idx] = alpha * acc_ref[idx] + _dot(p.astype(BF16), v)
    m_ref[idx] = m_new


def _fold_rows(x):
    return jnp.sum(x.reshape(x.shape[0] // 8, 8, x.shape[1]), axis=0)


def _attn_prompt_kernel(aqt_ref, iqt_ref, iwt_ref, k_ref, vt_ref, kidx_ref, tril_ref, o_ref,
                        sc_ref, m_ref, l_ref, acc_ref, *, tq, topk):
    i = pl.program_id(1)
    nkv = i + 1
    qpos = i * tq + lax.broadcasted_iota(jnp.int32, (1, tq), 1)
    kq = jnp.minimum(topk, qpos + 1).astype(F32)
    group = ATT_HEADS // ATT_KV_HEADS

    def score_body(j, carry):
        mn, mx = carry
        kx = kidx_ref[pl.ds(pl.multiple_of(j * tq, tq), tq), :]
        s = jnp.zeros((tq, tq), F32)
        for h in range(IDX_HEADS):
            d = _dot(kx, iqt_ref[h * IDX_HD:(h + 1) * IDX_HD, :]) * (IDX_HD ** -0.5)
            s = s + iwt_ref[h:h + 1, :] * jnp.maximum(d, 0.0)
        kpos = j * tq + lax.broadcasted_iota(jnp.int32, (tq, tq), 0)
        vis = kpos <= qpos
        sc_ref[j] = jnp.where(vis, s, -INF)
        mn = jnp.minimum(mn, jnp.min(jnp.where(vis, s, INF), axis=0, keepdims=True))
        mx = jnp.maximum(mx, jnp.max(jnp.where(vis, s, -INF), axis=0, keepdims=True))
        return mn, mx

    mn, mx = lax.fori_loop(0, nkv, score_body,
                           (jnp.full((1, tq), INF, F32), jnp.full((1, tq), -INF, F32)))

    def count(pred):
        def body(j, acc):
            return acc + _fold_rows(jnp.where(pred(sc_ref[j]), 1.0, 0.0))
        return jnp.sum(lax.fori_loop(0, nkv, body, jnp.zeros((8, tq), F32)), axis=0, keepdims=True)

    def bisect_body(_, carry):
        lo, hi = carry
        mid = 0.5 * (lo + hi)
        ge = count(lambda t: t >= mid) >= kq
        return jnp.where(ge, mid, lo), jnp.where(ge, hi, mid)

    lo, _ = lax.fori_loop(0, BISECT_ITERS, bisect_body, (mn, mx + 1.0))

    def snap_body(j, acc):
        t = sc_ref[j]
        return jnp.minimum(acc, jnp.min(jnp.where(t >= lo, t, INF), axis=0, keepdims=True))

    tau = lax.fori_loop(0, nkv, snap_body, jnp.full((1, tq), INF, F32))
    need = kq - count(lambda t: t > tau)

    tril = tril_ref[...]

    def mask_body(j, run):
        t = sc_ref[j]
        eqf = jnp.where(t == tau, 1.0, 0.0)
        prefix = _dot(tril, eqf.astype(BF16))
        sel = (t > tau) | ((t == tau) & ((run + prefix) <= need))
        sc_ref[j] = jnp.where(sel, 0.0, NEG)
        return run + jnp.sum(eqf, axis=0, keepdims=True)

    lax.fori_loop(0, nkv, mask_body, jnp.zeros((1, tq), F32))

    m_ref[...] = jnp.full(m_ref.shape, M_INIT, F32)
    l_ref[...] = jnp.zeros(l_ref.shape, F32)
    acc_ref[...] = jnp.zeros(acc_ref.shape, F32)

    def att_body(j, carry):
        rows = pl.ds(pl.multiple_of(j * tq, tq), tq)
        for h in range(ATT_HEADS):
            g = h // group
            kg = k_ref[rows, g * ATT_HD:(g + 1) * ATT_HD]
            vg = vt_ref[0, j, g * ATT_HD:(g + 1) * ATT_HD, :]
            for c in range(tq // LANES):
                ls = slice(c * LANES, (c + 1) * LANES)
                s = _dot(kg, aqt_ref[h * ATT_HD:(h + 1) * ATT_HD, ls]) * (ATT_HD ** -0.5) + sc_ref[j, :, ls]
                m_old = m_ref[h, :, ls]
                m_new = jnp.maximum(m_old, jnp.max(s, axis=0, keepdims=True))
                p = jnp.exp(s - m_new)
                alpha = jnp.exp(m_old - m_new)
                l_ref[h, :, ls] = alpha * l_ref[h, :, ls] + jnp.sum(p, axis=0, keepdims=True)
                acc_ref[h, :, ls] = alpha * acc_ref[h, :, ls] + _dot(vg, p.astype(BF16))
                m_ref[h, :, ls] = m_new
        return carry

    lax.fori_loop(0, nkv, att_body, 0)
    for h in range(ATT_HEADS):
        o_ref[h * ATT_HD:(h + 1) * ATT_HD, :] = (acc_ref[h] / l_ref[h]).astype(o_ref.dtype)


def _attention_prompt(aq, iq, iw, akb, avb, ikb, nb, seq):
    tq = min(ATTN_QUERY_TILE, seq)
    assert seq % tq == 0 and tq % LANES == 0
    nq = seq // tq
    topk = min(TOPK_ATTN, seq // 4)
    n = aq.shape[0]
    tril = jnp.tril(jnp.ones((tq, tq), F32)).astype(BF16)
    kvw = ATT_KV_HEADS * ATT_HD
    vt = avb.reshape(nb, nq, tq, kvw).transpose(0, 1, 3, 2)
    ot = pl.pallas_call(
        functools.partial(_attn_prompt_kernel, tq=tq, topk=topk),
        grid=(nb, nq),
        in_specs=[
            pl.BlockSpec((ATT_HEADS * ATT_HD, tq), lambda b, i: (0, b * nq + i)),
            pl.BlockSpec((IDX_HEADS * IDX_HD, tq), lambda b, i: (0, b * nq + i)),
            pl.BlockSpec((IDX_HEADS, tq), lambda b, i: (0, b * nq + i)),
            pl.BlockSpec((seq, kvw), lambda b, i: (b, 0)),
            pl.BlockSpec((1, nq, kvw, tq), lambda b, i: (b, 0, 0, 0)),
            pl.BlockSpec((seq, IDX_HD), lambda b, i: (b, 0)),
            pl.BlockSpec((tq, tq), lambda b, i: (0, 0)),
        ],
        out_specs=pl.BlockSpec((ATT_HEADS * ATT_HD, tq), lambda b, i: (0, b * nq + i)),
        out_shape=jax.ShapeDtypeStruct((ATT_HEADS * ATT_HD, n), BF16),
        scratch_shapes=[
            pltpu.VMEM((nq, tq, tq), F32),
            pltpu.VMEM((ATT_HEADS, 1, tq), F32),
            pltpu.VMEM((ATT_HEADS, 1, tq), F32),
            pltpu.VMEM((ATT_HEADS, ATT_HD, tq), F32),
        ],
        compiler_params=_params(("parallel", "arbitrary")),
        name="attention_prompt",
    )(aq.T, iq.T, iw.T, akb, vt, ikb, tril)
    return ot.T


def _page_scores(iq_ref, iw_ref, kxt):
    tp = iq_ref.shape[1] // IDX_HEADS
    d = _dot(iq_ref[0], kxt) * (IDX_HD ** -0.5)
    d = iw_ref[0] * jnp.maximum(d, 0.0)
    s = d[:tp]
    for h in range(1, IDX_HEADS):
        s = s + d[h * tp:(h + 1) * tp]
    return s


def _sample_scores_kernel(pt_ref, iq_ref, iw_ref, *refs, npp, n_new, topk):
    page_refs = refs[:npp]
    knew_ref = refs[npp]
    sc_ref, tau_ref, need_ref = refs[npp + 1:]
    s = pl.program_id(1)
    npages = sc_ref.shape[1] - 1
    tp, page = sc_ref.shape[2], sc_ref.shape[3]

    kxt = jnp.concatenate([r[0] for r in page_refs], axis=1).astype(BF16)
    sc = _page_scores(iq_ref, iw_ref, kxt)
    for p in range(npp):
        sc_ref[0, s * npp + p] = sc[:, p * page:(p + 1) * page]

    @pl.when(s == pl.num_programs(1) - 1)
    def _():
        snew = _page_scores(iq_ref, iw_ref, knew_ref[0].astype(BF16))
        t = jnp.minimum(lax.broadcasted_iota(jnp.int32, (tp, page), 0), n_new - 1)
        j = lax.broadcasted_iota(jnp.int32, (tp, page), 1)
        sc_ref[0, npages] = jnp.where(j <= t, snew, -INF)

        def reduce_tiles(fn, init):
            def body(c, acc):
                for p in range(npp):
                    acc = fn(acc, sc_ref[0, c * npp + p])
                return acc
            acc = lax.fori_loop(0, npages // npp, body, jnp.full((tp, page), init, F32))
            return fn(acc, sc_ref[0, npages])

        mn = jnp.min(reduce_tiles(lambda a, x: jnp.minimum(a, jnp.where(x == -INF, INF, x)), INF),
                     axis=-1, keepdims=True)
        mx = jnp.max(reduce_tiles(jnp.maximum, -INF), axis=-1, keepdims=True)
        tau, need = _kth_largest(reduce_tiles, mn, mx, jnp.full((tp, 1), float(topk), F32))
        tau_ref[0] = _lanes(tau)
        need_ref[0] = _lanes(need)


def _sample_attend_kernel(pt_ref, q_ref, sc_ref, tau_ref, need_ref, tri_ref, *refs, npp):
    k_refs = refs[:npp]
    v_refs = refs[npp:2 * npp]
    knew_ref, vnew_ref, o_ref, m_ref, l_ref, acc_ref, run_ref = refs[2 * npp:]
    s = pl.program_id(1)
    npages = sc_ref.shape[1] - 1
    page = sc_ref.shape[3]
    group = ATT_HEADS // ATT_KV_HEADS
    grows = q_ref.shape[1] // ATT_KV_HEADS

    @pl.when(s == 0)
    def _():
        m_ref[...] = jnp.full(m_ref.shape, M_INIT, F32)
        l_ref[...] = jnp.zeros(l_ref.shape, F32)
        acc_ref[...] = jnp.zeros(acc_ref.shape, F32)
        run_ref[...] = jnp.zeros(run_ref.shape, F32)

    tau = tau_ref[0][:, :1]
    need = need_ref[0][:, :1]
    tri = tri_ref[...]

    def attend(pages, krefs, vrefs):
        tp = sc_ref.shape[2]
        scs = [sc_ref[0, pg] for pg in pages]
        eqs = [jnp.where(t == tau, 1.0, 0.0) for t in scs]
        prefix = _dot(jnp.concatenate(eqs, axis=0).astype(BF16), tri)
        run = run_ref[...]
        madds = []
        for k, t in enumerate(scs):
            sel = (t > tau) | ((t == tau) & ((run + prefix[k * tp:(k + 1) * tp]) <= need))
            madds.append(jnp.where(sel, 0.0, NEG))
            run = run + jnp.sum(eqs[k], axis=-1, keepdims=True)
        run_ref[...] = run
        madd = jnp.concatenate(madds, axis=1)
        maddg = jnp.concatenate([madd] * group, axis=0)
        for g in range(ATT_KV_HEADS):
            head_rows = pl.ds(g, page, stride=ATT_KV_HEADS)
            kg = jnp.concatenate([r[0, head_rows, :] for r in krefs], axis=0).astype(BF16)
            vg = jnp.concatenate([r[0, head_rows, :] for r in vrefs], axis=0).astype(BF16)
            sg = _dot_nt(q_ref[0, g * grows:(g + 1) * grows, :], kg) * (ATT_HD ** -0.5) + maddg
            _softmax_step(sg, m_ref, l_ref, acc_ref, g, vg)

    attend([s * npp + p for p in range(npp)], k_refs, v_refs)

    @pl.when(s == pl.num_programs(1) - 1)
    def _():
        attend([npages], [knew_ref], [vnew_ref])
        for g in range(ATT_KV_HEADS):
            o_ref[0, g * grows:(g + 1) * grows, :] = acc_ref[g] / l_ref[g]


def _attention_sample(aq, iq, iw, ak, av, ik, cache_k, cache_v, cache_kidx, page_table, nb, t_new):
    n_phys, page = cache_kidx.shape[0], cache_kidx.shape[1]
    npages = page_table.shape[1]
    npp = SAMPLE_PAGES_PER_STEP if npages % SAMPLE_PAGES_PER_STEP == 0 else 1
    nsteps = npages // npp
    npp_sc = SAMPLE_SCORE_PAGES_PER_STEP if npages % SAMPLE_SCORE_PAGES_PER_STEP == 0 else npp
    tp = BF16_SUBLANES
    topk = min(TOPK_ATTN, (npages * page + t_new) // 4)
    prow = page * ATT_KV_HEADS

    def pad_rows(a, per_seq, rows):
        a = a.reshape(nb, per_seq, a.shape[-1])
        return jnp.pad(a, ((0, 0), (0, rows - per_seq), (0, 0)))

    iq_p = pad_rows(iq, t_new, tp).reshape(nb, tp, IDX_HEADS, IDX_HD).transpose(0, 2, 1, 3)
    iq_p = iq_p.reshape(nb, IDX_HEADS * tp, IDX_HD)
    iw_p = pad_rows(iw, t_new, tp).transpose(0, 2, 1).reshape(nb, IDX_HEADS * tp, 1)
    q_p = pad_rows(aq, t_new, tp).reshape(nb, tp, ATT_HEADS, ATT_HD).transpose(0, 2, 1, 3)
    q_p = q_p.reshape(nb, ATT_HEADS * tp, ATT_HD)
    knew = pad_rows(ak, t_new * ATT_KV_HEADS, prow)
    vnew = pad_rows(av, t_new * ATT_KV_HEADS, prow)
    kinew = pad_rows(ik, t_new, page).swapaxes(1, 2)
    kidx_t = cache_kidx.swapaxes(1, 2)
    ck = cache_k.reshape(n_phys, prow, ATT_HD)
    cv = cache_v.reshape(n_phys, prow, ATT_HD)
    tri = jnp.triu(jnp.ones((page, page), F32)).astype(BF16)

    def page_spec(rows, width, p, per_step=npp):
        return pl.BlockSpec((1, rows, width), lambda b, s, pt, p=p: (pt[b, s * per_step + p], 0, 0))

    def seq_spec(shape):
        return pl.BlockSpec((1,) + shape, lambda b, s, pt: (b,) + (0,) * len(shape))

    sc, tau, need = pl.pallas_call(
        functools.partial(_sample_scores_kernel, npp=npp_sc, n_new=t_new, topk=topk),
        grid_spec=pltpu.PrefetchScalarGridSpec(
            num_scalar_prefetch=1,
            grid=(nb, npages // npp_sc),
            in_specs=[seq_spec((IDX_HEADS * tp, IDX_HD)), seq_spec((IDX_HEADS * tp, 1))]
                     + [page_spec(IDX_HD, page, p, npp_sc) for p in range(npp_sc)] + [seq_spec((IDX_HD, page))],
            out_specs=[seq_spec((npages + 1, tp, page)), seq_spec((tp, page)), seq_spec((tp, page))],
        ),
        out_shape=[jax.ShapeDtypeStruct((nb, npages + 1, tp, page), F32),
                   jax.ShapeDtypeStruct((nb, tp, page), F32),
                   jax.ShapeDtypeStruct((nb, tp, page), F32)],
        compiler_params=_params(("parallel", "arbitrary")),
        name="sample_scores",
    )(page_table, iq_p, iw_p, *([kidx_t] * npp_sc), kinew)

    grows = (ATT_HEADS // ATT_KV_HEADS) * tp
    o = pl.pallas_call(
        functools.partial(_sample_attend_kernel, npp=npp),
        grid_spec=pltpu.PrefetchScalarGridSpec(
            num_scalar_prefetch=1,
            grid=(nb, nsteps),
            in_specs=[seq_spec((ATT_HEADS * tp, ATT_HD)), seq_spec((npages + 1, tp, page)),
                      seq_spec((tp, page)), seq_spec((tp, page)),
                      pl.BlockSpec((page, page), lambda b, s, pt: (0, 0))]
                     + [page_spec(prow, ATT_HD, p) for p in range(npp)]
                     + [page_spec(prow, ATT_HD, p) for p in range(npp)]
                     + [seq_spec((prow, ATT_HD)), seq_spec((prow, ATT_HD))],
            out_specs=seq_spec((ATT_HEADS * tp, ATT_HD)),
            scratch_shapes=[
                pltpu.VMEM((ATT_KV_HEADS, grows, 1), F32),
                pltpu.VMEM((ATT_KV_HEADS, grows, 1), F32),
                pltpu.VMEM((ATT_KV_HEADS, grows, ATT_HD), F32),
                pltpu.VMEM((tp, 1), F32),
            ],
        ),
        out_shape=jax.ShapeDtypeStruct((nb, ATT_HEADS * tp, ATT_HD), F32),
        compiler_params=_params(("parallel", "arbitrary")),
        name="sample_attend",
    )(page_table, q_p, sc, tau, need, tri, *([ck] * npp), *([cv] * npp), knew, vnew)
    o = o.reshape(nb, ATT_HEADS, tp, ATT_HD)[:, :, :t_new].transpose(0, 2, 1, 3)
    return o.reshape(nb * t_new, ATT_HEADS * ATT_HD).astype(BF16)


def _layer_norm(y, g, b):
    mu = jnp.mean(y, axis=-1, keepdims=True)
    var = jnp.mean(jnp.square(y - mu), axis=-1, keepdims=True)
    return (y - mu) * lax.rsqrt(var + LN_EPS) * g + b


def _mix_kernel(x_ref, ret_ref, att_ref, sga_ref, sgb_ref, wr_ref, wa_ref, wo_ref, g_ref, b_ref,
                o_ref, ot_ref, *, alpha):
    branch = sga_ref[...] * _dot(ret_ref[...], wr_ref[...]) + sgb_ref[...] * _dot(att_ref[...], wa_ref[...])
    y = alpha * x_ref[...] + _dot(branch.astype(BF16), wo_ref[...])
    x1 = _layer_norm(y, g_ref[...], b_ref[...])
    o_ref[...] = x1
    ot_ref[...] = x1.T.astype(BF16)


def _mix(x, ret_o, att_o, sgab, w_ret_o, w_att_o, w_out, g, b, tm, alpha):
    n, dm = x.shape
    hv = RET_HEADS * RET_DV
    ha = ATT_HEADS * ATT_HD

    def full(shape):
        return pl.BlockSpec(shape, lambda i: (0,) * len(shape))

    return pl.pallas_call(
        functools.partial(_mix_kernel, alpha=alpha),
        grid=(n // tm,),
        in_specs=[pl.BlockSpec((tm, dm), lambda i: (i, 0)),
                  pl.BlockSpec((tm, hv), lambda i: (i, 0)),
                  pl.BlockSpec((tm, ha), lambda i: (i, 0)),
                  pl.BlockSpec((tm, dm), lambda i: (i, 0)),
                  pl.BlockSpec((tm, dm), lambda i: (i, 1)),
                  full((hv, dm)), full((ha, dm)), full((dm, dm)), full((1, dm)), full((1, dm))],
        out_specs=[pl.BlockSpec((tm, dm), lambda i: (i, 0)),
                   pl.BlockSpec((dm, tm), lambda i: (0, i))],
        out_shape=[jax.ShapeDtypeStruct((n, dm), F32), jax.ShapeDtypeStruct((dm, n), BF16)],
        compiler_params=_params(("parallel",)),
        name="branch_mix_ln1",
    )(x, ret_o, att_o, sgab, sgab, w_ret_o, w_att_o, w_out, g, b)


PEER_EXPERT_ROWS = 16
PEER_ROW_TILE = 32


def _top_values(cur, k):
    rows = []
    for _ in range(k):
        m = jnp.max(cur, axis=0, keepdims=True)
        rows.append(m)
        cur = jnp.where(cur == m, -INF, cur)
    return rows


def _peer_route_kernel(xt_ref, wqt_ref, sk_ref, c_ref, a_ref, s2_ref, e2_ref):
    xt = xt_ref[...]
    half = PEER_DKEY // 2
    k1 = PEER_TOPK + 1
    pairs = [(a, b) for a in range(k1) for b in range(k1 // (a + 1))]
    for h in range(PEER_HEADS):
        qt = _dot(wqt_ref[h * PEER_DKEY:(h + 1) * PEER_DKEY, :], xt).astype(BF16)
        s1_ref = c_ref.at[h]
        s1_ref[...] = _dot(sk_ref[h, 0], qt[:half])
        s2_ref[h] = _dot(sk_ref[h, 1], qt[half:])
        for lc in range(xt.shape[1] // LANES):
            ls = slice(lc * LANES, (lc + 1) * LANES)
            s1 = s1_ref[:, ls]
            s2 = s2_ref[h, :, ls]
            v1 = _top_values(s1, k1)
            v2 = _top_values(s2, k1)
            rows = [v1[a] + v2[b] for a, b in pairs]
            rows += [jnp.full_like(rows[0], -INF)] * ((-len(rows)) % 8)
            cand = jnp.concatenate(rows, axis=0)
            best = _top_values(cand, k1)
            theta = 0.5 * (best[PEER_TOPK - 1] + best[PEER_TOPK])
            z = jnp.sum(jnp.where(cand > theta, jnp.exp(cand - best[0]), 0.0), axis=0, keepdims=True)
            c_ref[h, :, ls] = theta - s1
            a_ref[h, :, ls] = jnp.exp(s1 - v1[0]) / z
            e2_ref[h, :, ls] = jnp.exp(s2 - v2[0])


def _peer_dense_kernel(xt_ref, c_ref, a_ref, s2_ref, e2_ref, u_ref, vt_ref, o_ref, act_ref, gm_ref):
    e = pl.program_id(1)
    nk = PEER_NKEYS
    nsub = PEER_EXPERT_ROWS
    tt = xt_ref.shape[1]

    @pl.when(e == 0)
    def _():
        o_ref[...] = jnp.zeros(o_ref.shape, F32)

    def activations(rows, ls):
        act = _dot(u_ref[rows, :], xt_ref[:, ls])
        act_ref[rows, ls] = 0.5 * act * (1.0 + lax.erf(act * (2.0 ** -0.5)))

    def weights_tile(lc, rt):
        ls = slice(lc * LANES, (lc + 1) * LANES)
        rs = slice(rt * PEER_ROW_TILE, (rt + 1) * PEER_ROW_TILE)
        g = [jnp.zeros((PEER_ROW_TILE, LANES), F32)] * nsub
        for h in range(PEER_HEADS):
            s2 = s2_ref[h, rs, ls]
            e2 = e2_ref[h, rs, ls]
            for c in range(nsub):
                g[c] = g[c] + jnp.where(s2 > c_ref[h, c:c + 1, ls], e2 * a_ref[h, c:c + 1, ls], 0.0)
        for c in range(nsub):
            er = slice(c * nk + rt * PEER_ROW_TILE, c * nk + (rt + 1) * PEER_ROW_TILE)
            gm_ref[er, ls] = (g[c] * act_ref[er, ls]).astype(BF16)

    def outputs(rows, ls):
        o_ref[rows, ls] += _dot(vt_ref[rows, :], gm_ref[:, ls])

    nrt = nk // PEER_ROW_TILE
    nlc = tt // LANES
    if nlc % 2:
        activations(slice(None), slice(None))
        for lc in range(nlc):
            for rt in range(nrt):
                weights_tile(lc, rt)
        outputs(slice(None), slice(None))
        return

    halves = [slice(0, tt // 2), slice(tt // 2, tt)]
    npiece = 2
    arows = [slice(q * (nsub * nk // npiece), (q + 1) * (nsub * nk // npiece)) for q in range(npiece)]
    orows = [slice(q * (o_ref.shape[0] // npiece), (q + 1) * (o_ref.shape[0] // npiece)) for q in range(npiece)]
    tiles = [[(lc, rt) for lc in range(hf * nlc // 2, (hf + 1) * nlc // 2) for rt in range(nrt)] for hf in range(2)]
    per = len(tiles[0]) // npiece
    for q in range(npiece):
        activations(arows[q], halves[0])
    for q in range(npiece):
        activations(arows[q], halves[1])
        for lc, rt in tiles[0][q * per:(q + 1) * per]:
            weights_tile(lc, rt)
    for q in range(npiece):
        outputs(orows[q], halves[0])
        for lc, rt in tiles[1][q * per:(q + 1) * per]:
            weights_tile(lc, rt)
    for q in range(npiece):
        outputs(orows[q], halves[1])


def _peer(x1t, wqt, subkeys, u, vt, tt):
    dm, n = x1t.shape
    assert n % tt == 0 and tt % LANES == 0
    nt = n // tt
    nk = PEER_NKEYS
    nsub = PEER_EXPERT_ROWS
    hshape = (PEER_HEADS, nk, n)
    hspec = pl.BlockSpec((PEER_HEADS, nk, tt), lambda i: (0, 0, i))
    c, a, s2, e2 = pl.pallas_call(
        _peer_route_kernel,
        grid=(nt,),
        in_specs=[pl.BlockSpec((dm, tt), lambda i: (0, i)),
                  pl.BlockSpec(wqt.shape, lambda i: (0, 0)),
                  pl.BlockSpec(subkeys.shape, lambda i: (0, 0, 0, 0))],
        out_specs=[hspec] * 4,
        out_shape=[jax.ShapeDtypeStruct(hshape, F32)] * 4,
        compiler_params=_params(("parallel",)),
        name="peer_route",
    )(x1t, wqt, subkeys)

    full_spec = pl.BlockSpec((PEER_HEADS, nk, tt), lambda i, e: (0, 0, i))
    row_spec = pl.BlockSpec((PEER_HEADS, nsub, tt), lambda i, e: (0, e, i))
    return pl.pallas_call(
        _peer_dense_kernel,
        grid=(nt, nk // nsub),
        in_specs=[pl.BlockSpec((dm, tt), lambda i, e: (0, i)),
                  row_spec, row_spec, full_spec, full_spec,
                  pl.BlockSpec((nsub * nk, dm), lambda i, e: (e, 0)),
                  pl.BlockSpec((dm, nsub * nk), lambda i, e: (0, e))],
        out_specs=pl.BlockSpec((dm, tt), lambda i, e: (0, i)),
        out_shape=jax.ShapeDtypeStruct((dm, n), F32),
        scratch_shapes=[pltpu.VMEM((nsub * nk, tt), F32), pltpu.VMEM((nsub * nk, tt), BF16)],
        compiler_params=_params(("parallel", "arbitrary")),
        name="peer_dense",
    )(x1t, c, a, s2, e2, u, vt)


def _final_kernel(x1_ref, pt_ref, p_ref, wg_ref, wp_ref, g_ref, b_ref, o_ref, *, alpha):
    x1 = x1_ref[...]
    ple = jax.nn.sigmoid(_dot(x1.astype(BF16), wg_ref[...])) * _dot(p_ref[...], wp_ref[...])
    y = alpha * x1 + pt_ref[...].T + ple
    o_ref[...] = _layer_norm(y, g_ref[...], b_ref[...])


def _final(x1, peer_t, p_emb, w_gate, w_ple, g, b, tm, alpha):
    n, dm = x1.shape
    pd = p_emb.shape[1]

    def full(shape):
        return pl.BlockSpec(shape, lambda i: (0,) * len(shape))

    return pl.pallas_call(
        functools.partial(_final_kernel, alpha=alpha),
        grid=(n // tm,),
        in_specs=[pl.BlockSpec((tm, dm), lambda i: (i, 0)),
                  pl.BlockSpec((dm, tm), lambda i: (0, i)),
                  pl.BlockSpec((tm, pd), lambda i: (i, 0)),
                  full((dm, dm)), full((pd, dm)), full((1, dm)), full((1, dm))],
        out_specs=pl.BlockSpec((tm, dm), lambda i: (i, 0)),
        out_shape=jax.ShapeDtypeStruct((n, dm), F32),
        compiler_params=_params(("parallel",)),
        name="ple_ln2",
    )(x1, peer_t, p_emb, w_gate, w_ple, g, b)


def _layer(x, p_emb, pos_rows, tm, weights, attention_fn, state_in, chunk, chunk_eff, alpha):
    n = x.shape[0]
    nb = state_in.shape[0]
    t_seq = n // nb
    pr = _project(x.astype(BF16), weights["w_in"], weights["ikg"], weights["ikb"], pos_rows, tm)

    def pad_chunks(a):
        if t_seq % chunk == 0:
            return a
        a = a.reshape(nb, t_seq, a.shape[-1])
        return jnp.pad(a, ((0, 0), (0, chunk - t_seq), (0, 0))).reshape(nb * chunk, a.shape[-1])

    ret_o, s_out = _retention(pad_chunks(pr["rq"]), pad_chunks(pr["rk"]), pad_chunks(pr["rv"]),
                              pad_chunks(pr["rg"]), weights["gn_g"], weights["gn_b"],
                              state_in, chunk, chunk_eff)
    if t_seq % chunk != 0:
        ret_o = ret_o.reshape(nb, chunk, -1)[:, :t_seq].reshape(n, -1)
    att_o = attention_fn(pr)
    x1, x1t = _mix(x, ret_o, att_o, pr["sgab"], weights["w_ret_o"], weights["w_att_o"], weights["w_out"],
                   weights["ln1_g"], weights["ln1_b"], tm, alpha)
    peer_t = _peer(x1t, weights["peer_wqt"], weights["peer_subkeys"], weights["peer_u"], weights["peer_vt"],
                   min(PEER_TOKEN_TILE, n))
    y = _final(x1, peer_t, p_emb.astype(BF16), weights["w_ple_gate"], weights["w_ple"],
               weights["ln2_g"], weights["ln2_b"], tm, alpha)
    return y, pr["ak"], pr["av"], pr["ik"], s_out


def kernel(x_prompt, x_sample, cache_k, cache_v, cache_kidx, state_ret, page_table, p_prompt, p_sample,
           w_in, idx_k_g, idx_k_b, gn_g, gn_b, w_ret_o, w_att_o, w_out, ln1_g, ln1_b, peer_wq,
           peer_subkeys, peer_u, peer_v, w_ple_gate, w_ple, ln2_g, ln2_b):
    depth = w_in.shape[0]
    assert depth == 1, "single-layer trunk"
    nbp, seq, dm = x_prompt.shape
    nbs, t_new, _ = x_sample.shape
    past_len = page_table.shape[1] * cache_k.shape[2]
    alpha = (2.0 * depth) ** 0.25
    i = 0
    weights = dict(
        w_in=w_in[i].astype(BF16),
        ikg=jnp.concatenate([idx_k_g[i], idx_k_g[i]])[None, :],
        ikb=jnp.concatenate([idx_k_b[i], idx_k_b[i]])[None, :],
        gn_g=gn_g[i][None, :], gn_b=gn_b[i][None, :],
        w_ret_o=w_ret_o[i].astype(BF16), w_att_o=w_att_o[i].astype(BF16), w_out=w_out[i].astype(BF16),
        ln1_g=ln1_g[i][None, :], ln1_b=ln1_b[i][None, :],
        peer_wqt=peer_wq[i].T.astype(BF16), peer_subkeys=peer_subkeys[i].astype(BF16),
        peer_u=peer_u[i].astype(BF16), peer_vt=peer_v[i].T.astype(BF16),
        w_ple_gate=w_ple_gate[i].astype(BF16), w_ple=w_ple[i].astype(BF16),
        ln2_g=ln2_g[i][None, :], ln2_b=ln2_b[i][None, :],
    )

    def attn_p(pr):
        return _attention_prompt(pr["aq"], pr["iq"], pr["iw"], pr["akb"], pr["avb"], pr["ikb"], nbp, seq)

    yp, kp, vp, ikp, rp = _layer(
        x_prompt.reshape(nbp * seq, dm), p_prompt[i].reshape(nbp * seq, -1),
        jnp.arange(seq, dtype=jnp.int32), min(TOKEN_TILE, seq), weights, attn_p,
        jnp.zeros((nbp,) + state_ret.shape[2:], F32), RET_CHUNK, RET_CHUNK, alpha)

    def attn_s(pr):
        return _attention_sample(pr["aq"], pr["iq"], pr["iw"], pr["ak"], pr["av"], pr["ik"],
                                 cache_k[i], cache_v[i], cache_kidx[i], page_table, nbs, t_new)

    ns = nbs * t_new
    pos_s = jnp.tile(past_len + jnp.arange(t_new, dtype=jnp.int32), nbs)
    ys, ks, vs, iks, rs = _layer(
        x_sample.reshape(ns, dm), p_sample[i].reshape(ns, -1), pos_s, ns, weights, attn_s,
        state_ret[i].astype(F32), BF16_SUBLANES, t_new, alpha)

    return (yp.reshape(nbp, seq, dm), ys.reshape(nbs, t_new, dm),
            kp.reshape(1, nbp, seq, ATT_KV_HEADS, ATT_HD), vp.reshape(1, nbp, seq, ATT_KV_HEADS, ATT_HD),
            ikp.reshape(1, nbp, seq, IDX_HD), rp[None].astype(state_ret.dtype),
            ks.reshape(1, nbs, t_new, ATT_KV_HEADS, ATT_HD), vs.reshape(1, nbs, t_new, ATT_KV_HEADS, ATT_HD),
            iks.reshape(1, nbs, t_new, IDX_HD), rs[None].astype(state_ret.dtype))
```
